```python
import math
import jax, jax.numpy as jnp
from jax import lax
import numpy as np

D_MODEL = 2048
BATCH = 8
SEQ = 2048
DEPTH = 1
DEC_BATCH = 2
DEC_SEQ = 8192
PAST_LEN = 128

SSM_WIDTH = D_MODEL // 2
SSM_GROUP = 16
SSM_GROUPS = SSM_WIDTH // SSM_GROUP
SSM_STATE = 64
ATTN_WIDTH = D_MODEL // 2
HEAD_DIM = 64
N_HEADS = ATTN_WIDTH // HEAD_DIM
GRID_W = 64
WIN_ROWS = 8
WIN_COLS = 16
NORM_EPS = 1e-6
IN_WIDTH = 2 * SSM_WIDTH + 4 * ATTN_WIDTH + 2 * D_MODEL
MASK_VALUE = -1e30
LOG_DT_MIN = math.log(1e-3)
LOG_DT_MAX = math.log(1e-1)

kernel_name = "hybrid_s5_natten_encoder"


def _rmsnorm(x, g):
    x32 = x.astype(jnp.float32)
    y = x32 * lax.rsqrt(jnp.mean(x32 * x32, axis=-1, keepdims=True) + NORM_EPS)
    return (y * g.astype(jnp.float32)).astype(x.dtype)


def _complex_affine_combine(e1, e2):
    a1r, a1i, b1r, b1i = e1
    a2r, a2i, b2r, b2i = e2
    return (a2r * a1r - a2i * a1i,
            a2r * a1i + a2i * a1r,
            a2r * b1r - a2i * b1i + b2r,
            a2r * b1i + a2i * b1r + b2i)


def _s5_bidirectional(u, a_re, a_im, log_dt, b_re, b_im, c_re, c_im, d_skip):
    f32 = jnp.float32
    bsz, seq_len, _ = u.shape
    ug = u.reshape(bsz, seq_len, SSM_GROUPS, SSM_GROUP)
    lam_re = jnp.minimum(a_re.astype(f32), -1e-4)
    lam_im = a_im.astype(f32)
    dt = jnp.exp(log_dt.astype(f32))[..., None]
    mag = jnp.exp(lam_re * dt)
    abar_re = mag * jnp.cos(lam_im * dt)
    abar_im = mag * jnp.sin(lam_im * dt)
    den = lam_re * lam_re + lam_im * lam_im
    f_re = ((abar_re - 1.0) * lam_re + abar_im * lam_im) / den
    f_im = (abar_im * lam_re - (abar_re - 1.0) * lam_im) / den
    b_re = b_re.astype(f32); b_im = b_im.astype(f32)
    bb_re = f_re[..., None] * b_re - f_im[..., None] * b_im
    bb_im = f_re[..., None] * b_im + f_im[..., None] * b_re
    c_re = c_re.astype(f32); c_im = c_im.astype(f32)
    y_dirs = []
    for direction in range(2):
        bu_re = jnp.einsum('blgh,gph->blgp', ug, bb_re[direction])
        bu_im = jnp.einsum('blgh,gph->blgp', ug, bb_im[direction])
        a_r = jnp.broadcast_to(abar_re[direction][None, None], (1, seq_len, SSM_GROUPS, SSM_STATE))
        a_i = jnp.broadcast_to(abar_im[direction][None, None], (1, seq_len, SSM_GROUPS, SSM_STATE))
        _, _, s_re, s_im = lax.associative_scan(
            _complex_affine_combine, (a_r, a_i, bu_re, bu_im), reverse=(direction == 1), axis=1)
        y_dirs.append(jnp.einsum('blgp,ghp->blgh', s_re, c_re[direction])
                      - jnp.einsum('blgp,ghp->blgh', s_im, c_im[direction]))
    y = (y_dirs[0] + y_dirs[1]).reshape(bsz, seq_len, SSM_WIDTH)
    return y + d_skip.astype(f32) * u


def _neighbourhood_attention(q, k, v, rel_bias):
    bsz, seq_len, n_heads, hd = q.shape
    rows = seq_len // GRID_W
    kh = min(WIN_ROWS, rows)
    r = jnp.arange(rows)
    rstart = jnp.clip(r - kh // 2, 0, rows - kh)
    krow = rstart[:, None] + jnp.arange(kh)[None, :]
    qg = q.reshape(bsz, rows, GRID_W, n_heads, hd)
    kg = k.reshape(bsz, rows, GRID_W, n_heads, hd)[:, krow]
    vg = v.reshape(bsz, rows, GRID_W, n_heads, hd)[:, krow]
    scores = jnp.einsum('brqhd,brkchd->brhqkc', qg, kg).astype(jnp.float32) * (hd ** -0.5)
    c = jnp.arange(GRID_W)
    cstart = jnp.clip(c - WIN_COLS // 2, 0, GRID_W - WIN_COLS)
    valid = (c[None, :] >= cstart[:, None]) & (c[None, :] < cstart[:, None] + WIN_COLS)
    dr_idx = krow - r[:, None] + (WIN_ROWS - 1)
    dc_idx = jnp.clip(c[None, :] - c[:, None] + (WIN_COLS - 1), 0, 2 * WIN_COLS - 2)
    rb = rel_bias.astype(jnp.float32)[:, dr_idx][..., dc_idx]
    rb = rb.transpose(1, 0, 3, 2, 4)
    scores = jnp.where(valid[None, None, None, :, None, :], scores + rb[None], MASK_VALUE)
    probs = jax.nn.softmax(scores.reshape(bsz, rows, n_heads, GRID_W, kh * GRID_W), axis=-1)
    probs = probs.reshape(bsz, rows, n_heads, GRID_W, kh, GRID_W).astype(v.dtype)
    out = jnp.einsum('brhqkc,brkchd->brqhd', probs, vg)
    return out.reshape(bsz, seq_len, n_heads * hd)


def _hybrid_layer(x, norm_g, w_in, ssm_a_re, ssm_a_im, ssm_log_dt, ssm_b_re, ssm_b_im,
                  ssm_c_re, ssm_c_im, ssm_d, w_glu, b_glu, q_norm_g, k_norm_g, rel_bias,
                  w_branch_ssm, w_branch_attn, w_out):
    bsz, seq_len, _ = x.shape
    h = _rmsnorm(x, norm_g)
    proj = h @ w_in
    cuts = np.cumsum([SSM_WIDTH, SSM_WIDTH, ATTN_WIDTH, ATTN_WIDTH, ATTN_WIDTH, ATTN_WIDTH, D_MODEL]).tolist()
    u_s, z_s, q, k, v, z_a, g_s, g_a = jnp.split(proj, cuts, axis=-1)
    y_s = _s5_bidirectional(u_s.astype(jnp.float32), ssm_a_re, ssm_a_im, ssm_log_dt,
                            ssm_b_re, ssm_b_im, ssm_c_re, ssm_c_im, ssm_d)
    y_s = jax.nn.gelu(y_s)
    y_s = y_s * jax.nn.sigmoid(y_s @ w_glu + b_glu)
    o_s = (y_s * jax.nn.silu(z_s.astype(jnp.float32))).astype(x.dtype)
    q = _rmsnorm(q.reshape(bsz, seq_len, N_HEADS, HEAD_DIM), q_norm_g)
    k = _rmsnorm(k.reshape(bsz, seq_len, N_HEADS, HEAD_DIM), k_norm_g)
    v = v.reshape(bsz, seq_len, N_HEADS, HEAD_DIM)
    o_a = _neighbourhood_attention(q, k, v, rel_bias) * jax.nn.silu(z_a)
    m = jax.nn.sigmoid(g_s) * (o_s @ w_branch_ssm) + jax.nn.sigmoid(g_a) * (o_a @ w_branch_attn)
    return (x + m @ w_out).astype(x.dtype)


def _trunk(x, norm_g, w_in, ssm_a_re, ssm_a_im, ssm_log_dt, ssm_b_re, ssm_b_im,
           ssm_c_re, ssm_c_im, ssm_d, w_glu, b_glu, q_norm_g, k_norm_g, rel_bias,
           w_branch_ssm, w_branch_attn, w_out):
    for layer in range(DEPTH):
        x = _hybrid_layer(x, norm_g[layer], w_in[layer], ssm_a_re[layer], ssm_a_im[layer],
                          ssm_log_dt[layer], ssm_b_re[layer], ssm_b_im[layer], ssm_c_re[layer],
                          ssm_c_im[layer], ssm_d[layer], w_glu[layer], b_glu[layer],
                          q_norm_g[layer], k_norm_g[layer], rel_bias[layer],
                          w_branch_ssm[layer], w_branch_attn[layer], w_out[layer])
    return x


def setup_inputs(seed: int = 0) -> dict:
    key = jax.random.key(seed)
    ks = jax.random.split(key, 24)
    f32 = jnp.float32
    G, P, H = SSM_GROUPS, SSM_STATE, SSM_GROUP
    nrm = lambda k_, shape, scale: jax.random.normal(k_, shape, f32) * scale
    a_im_init = jnp.pi * jnp.arange(P, dtype=f32)
    return {
        "x_prompt": nrm(ks[0], (BATCH, SEQ, D_MODEL), 1.0),
        "x_sample": nrm(ks[1], (DEC_BATCH, DEC_SEQ, D_MODEL), 1.0),
        "norm_g": 1.0 + nrm(ks[2], (DEPTH, D_MODEL), 0.02),
        "w_in": nrm(ks[3], (DEPTH, D_MODEL, IN_WIDTH), D_MODEL ** -0.5),
        "ssm_a_re": -0.5 + nrm(ks[4], (DEPTH, 2, G, P), 0.01),
        "ssm_a_im": a_im_init + nrm(ks[5], (DEPTH, 2, G, P), 0.01),
        "ssm_log_dt": jax.random.uniform(ks[6], (DEPTH, 2, G), f32, LOG_DT_MIN, LOG_DT_MAX),
        "ssm_b_re": nrm(ks[7], (DEPTH, 2, G, P, H), (2 * H) ** -0.5),
        "ssm_b_im": nrm(ks[8], (DEPTH, 2, G, P, H), (2 * H) ** -0.5),
        "ssm_c_re": nrm(ks[9], (DEPTH, 2, G, H, P), P ** -0.5),
        "ssm_c_im": nrm(ks[10], (DEPTH, 2, G, H, P), P ** -0.5),
        "ssm_d": nrm(ks[11], (DEPTH, SSM_WIDTH), 1.0),
        "w_glu": nrm(ks[12], (DEPTH, SSM_WIDTH, SSM_WIDTH), SSM_WIDTH ** -0.5),
        "b_glu": nrm(ks[13], (DEPTH, SSM_WIDTH), 0.01),
        "q_norm_g": 1.0 + nrm(ks[14], (DEPTH, HEAD_DIM), 0.02),
        "k_norm_g": 1.0 + nrm(ks[15], (DEPTH, HEAD_DIM), 0.02),
        "rel_bias": nrm(ks[16], (DEPTH, N_HEADS, 2 * WIN_ROWS - 1, 2 * WIN_COLS - 1), 0.02),
        "w_branch_ssm": nrm(ks[17], (DEPTH, SSM_WIDTH, D_MODEL), SSM_WIDTH ** -0.5),
        "w_branch_attn": nrm(ks[18], (DEPTH, ATTN_WIDTH, D_MODEL), ATTN_WIDTH ** -0.5),
        "w_out": nrm(ks[19], (DEPTH, D_MODEL, D_MODEL), D_MODEL ** -0.5),
    }


def reference(x_prompt, x_sample, norm_g, w_in, ssm_a_re, ssm_a_im, ssm_log_dt, ssm_b_re,
              ssm_b_im, ssm_c_re, ssm_c_im, ssm_d, w_glu, b_glu, q_norm_g, k_norm_g, rel_bias,
              w_branch_ssm, w_branch_attn, w_out):
    y_prompt = _trunk(x_prompt, norm_g, w_in, ssm_a_re, ssm_a_im, ssm_log_dt, ssm_b_re, ssm_b_im,
                      ssm_c_re, ssm_c_im, ssm_d, w_glu, b_glu, q_norm_g, k_norm_g, rel_bias,
                      w_branch_ssm, w_branch_attn, w_out)
    y_sample = _trunk(x_sample, norm_g, w_in, ssm_a_re, ssm_a_im, ssm_log_dt, ssm_b_re, ssm_b_im,
                      ssm_c_re, ssm_c_im, ssm_d, w_glu, b_glu, q_norm_g, k_norm_g, rel_bias,
                      w_branch_ssm, w_branch_attn, w_out)
    return (y_prompt, y_sample)
```

```python
import functools
import math

import jax
import jax.numpy as jnp
from jax import lax
from jax.experimental import pallas as pl
from jax.experimental.pallas import tpu as pltpu

NORM_EPS = 1e-6
MASK_VALUE = -1e30
GRID_W = 64
WIN_ROWS = 8
WIN_COLS = 16
CHUNK = 64
LANES = 128
SCAN_LEVELS = 7
VMEM_LIMIT_BYTES = 56 * 1024 * 1024

F32 = jnp.float32
BF16 = jnp.bfloat16
HIGHEST = lax.Precision.HIGHEST


def _params(*sem):
    return pltpu.CompilerParams(dimension_semantics=sem, vmem_limit_bytes=VMEM_LIMIT_BYTES)


def _sigmoid(x):
    return jax.nn.sigmoid(x)


def _inproj_kernel(x_ref, g_ref, wt_ref, w_ref, outt_ref, out_ref, h_ref, *, n_t, n_u, n_plain, n_silu):
    j = pl.program_id(1)

    @pl.when(j == 0)
    def _():
        x = x_ref[...]
        ms = jnp.mean(x * x, axis=-1, keepdims=True)
        h_ref[...] = (x * lax.rsqrt(ms + NORM_EPS) * g_ref[...]).astype(BF16)

    def transposed():
        return lax.dot_general(wt_ref[...], h_ref[...], (((1,), (1,)), ((), ())),
                               preferred_element_type=F32)

    def plain():
        return jnp.dot(h_ref[...], w_ref[...], preferred_element_type=F32)

    @pl.when(j < n_u)
    def _():
        outt_ref[...] = transposed().astype(BF16)

    @pl.when((j >= n_u) & (j < n_t))
    def _():
        r = transposed()
        outt_ref[...] = (r * _sigmoid(r)).astype(BF16)

    jr = j - n_t

    @pl.when((j >= n_t) & (jr < n_plain))
    def _():
        out_ref[...] = plain().astype(BF16)

    @pl.when((jr >= n_plain) & (jr < n_plain + n_silu))
    def _():
        r = plain()
        out_ref[...] = (r * _sigmoid(r)).astype(BF16)

    @pl.when(jr >= n_plain + n_silu)
    def _():
        out_ref[...] = _sigmoid(plain()).astype(BF16)


def _inproj(x, g, wt, w, *, sw, aw, tm, tn):
    t, d = x.shape
    n_t = (2 * sw) // tn
    n_r = w.shape[1] // tn
    kern = functools.partial(_inproj_kernel, n_t=n_t, n_u=sw // tn, n_plain=(3 * aw) // tn, n_silu=aw // tn)
    return pl.pallas_call(
        kern,
        grid=(t // tm, n_t + n_r),
        in_specs=[
            pl.BlockSpec((tm, d), lambda i, j: (i, 0)),
            pl.BlockSpec((1, d), lambda i, j: (0, 0)),
            pl.BlockSpec((tn, d), lambda i, j: (jnp.minimum(j, n_t - 1), 0)),
            pl.BlockSpec((d, tn), lambda i, j: (0, jnp.maximum(j - n_t, 0))),
        ],
        out_specs=[
            pl.BlockSpec((tn, tm), lambda i, j: (jnp.minimum(j, n_t - 1), i)),
            pl.BlockSpec((tm, tn), lambda i, j: (i, jnp.maximum(j - n_t, 0))),
        ],
        out_shape=[jax.ShapeDtypeStruct((2 * sw, t), BF16), jax.ShapeDtypeStruct((t, w.shape[1]), BF16)],
        scratch_shapes=[pltpu.VMEM((tm, d), BF16)],
        compiler_params=_params("arbitrary", "arbitrary"),
        name="inproj",
    )(x, g, wt, w)


def _cmul(ar, ai, br, bi):
    return ar * br - ai * bi, ar * bi + ai * br


def _discretize(a_re, a_im, log_dt):
    lam_re = jnp.minimum(a_re, -1e-4)
    lam_im = a_im
    dt = jnp.exp(log_dt)
    mag = jnp.exp(lam_re * dt)
    abar_re = mag * jnp.cos(lam_im * dt)
    abar_im = mag * jnp.sin(lam_im * dt)
    den = lam_re * lam_re + lam_im * lam_im
    f_re = ((abar_re - 1.0) * lam_re + abar_im * lam_im) / den
    f_im = (abar_im * lam_re - (abar_re - 1.0) * lam_im) / den
    return abar_re, abar_im, f_re, f_im


def _squarings(ar, ai, n):
    out = [(ar, ai)]
    for _ in range(n - 1):
        ar, ai = _cmul(ar, ai, ar, ai)
        out.append((ar, ai))
    return out


def _power_table(sq, exps, shape):
    pr = jnp.ones(shape, F32)
    pi = jnp.zeros(shape, F32)
    for k, (ar, ai) in enumerate(sq):
        bit = ((exps >> k) & 1) == 1
        fr = jnp.where(bit, ar, 1.0)
        fi = jnp.where(bit, ai, 0.0)
        pr, pi = _cmul(pr, pi, fr, fi)
    return pr, pi


def _ssm_ops_kernel(arow_ref, acol_ref, bt_ref, c_ref, ct_ref, w_ref, f_ref, e_ref, dec_ref, k_ref, *, h_dim, p_dim):
    lc = CHUNK
    nbits = lc.bit_length()
    lane = lax.broadcasted_iota(jnp.int32, (1, LANES), 1)
    left = lane < lc
    tloc = lane & (lc - 1)
    sub = lax.broadcasted_iota(jnp.int32, (lc, 1), 0)

    q2 = []
    rmat = []
    bb_rows = []
    sq_rows = []
    for d in range(2):
        abr, abi, fr, fi = _discretize(arow_ref[d, 0], arow_ref[d, 1], arow_ref[d, 2])
        bbr, bbi = _cmul(fr, fi, bt_ref[d, 0], bt_ref[d, 1])
        bb_rows.append((bbr, bbi))
        sq_row = _squarings(abr, abi, nbits + SCAN_LEVELS)
        sq_rows.append(sq_row)
        cr, ci = c_ref[d, 0], c_ref[d, 1]
        blocks = []
        for h in range(h_dim):
            qr, qi = _cmul(cr, ci, bbr[h:h + 1, :], bbi[h:h + 1, :])
            blocks.append(jnp.where(left, qr, -qi))
        q2.append(jnp.concatenate(blocks, axis=0))
        cabr, cabi, _, _ = _discretize(acol_ref[d, 0], acol_ref[d, 1], acol_ref[d, 2])
        sq_col = _squarings(cabr, cabi, nbits)
        lag = lane - lc
        if d == 0:
            exps, live = jnp.maximum(lag, 0), lag >= 0
        else:
            exps, live = jnp.maximum(-lag, 0), lag <= 0
        vr, vi = _power_table(sq_col, exps, (p_dim, LANES))
        rmat.append(jnp.concatenate([jnp.where(live, vr, 0.0), jnp.where(live, vi, 0.0)], axis=0))
        eexp = tloc + 1 if d == 0 else lc - tloc
        er, ei = _power_table(sq_col, eexp, (p_dim, LANES))
        n_rep = (h_dim * lc) // LANES
        er = jnp.concatenate([er] * n_rep, axis=1)
        ei = jnp.concatenate([ei] * n_rep, axis=1)
        rep_r = lax.broadcasted_iota(jnp.int32, (h_dim, h_dim * lc), 0)
        rep_c = lax.broadcasted_iota(jnp.int32, (h_dim, h_dim * lc), 1)
        rep = ((rep_c >> (lc.bit_length() - 1)) == rep_r).astype(F32)
        cer = jnp.dot(ct_ref[d, 0], rep, precision=HIGHEST, preferred_element_type=F32)
        cei = jnp.dot(ct_ref[d, 1], rep, precision=HIGHEST, preferred_element_type=F32)
        gr, gi = _cmul(cer, cei, er, ei)
        e_ref[2 * d * p_dim:(2 * d + 1) * p_dim, :] = gr.astype(BF16)
        e_ref[(2 * d + 1) * p_dim:(2 * d + 2) * p_dim, :] = (-gi).astype(BF16)
        for k in range(SCAN_LEVELS):
            dr_, di_ = sq_row[nbits - 1 + k]
            dec_ref[(d * SCAN_LEVELS + k) * 2:(d * SCAN_LEVELS + k) * 2 + 1, :] = dr_
            dec_ref[(d * SCAN_LEVELS + k) * 2 + 1:(d * SCAN_LEVELS + k) * 2 + 2, :] = jnp.where(left, -di_, di_)

    k_ref[...] = (jnp.dot(q2[0], rmat[0], precision=HIGHEST, preferred_element_type=F32)
                  + jnp.dot(q2[1], rmat[1], precision=HIGHEST, preferred_element_type=F32))

    fexp = (lc - 1 - sub, sub)
    vrow = [_power_table(sq_rows[d][:nbits], fexp[d], (lc, LANES)) for d in range(2)]

    def per_h(h, carry):
        for d in range(2):
            bbr, bbi = bb_rows[d]
            sel = (lax.broadcasted_iota(jnp.int32, (h_dim, 1), 0) == h).astype(F32)
            br = jnp.sum(bbr * sel, axis=0, keepdims=True)
            bi = jnp.sum(bbi * sel, axis=0, keepdims=True)
            zr, zi = _cmul(vrow[d][0], vrow[d][1], br, bi)
            f_ref[h, :, d * LANES:(d + 1) * LANES] = jnp.where(left, zr, zi).astype(BF16)
        for m in range(h_dim // 2):
            ra = k_ref[pl.ds(h * h_dim + 2 * m, 1), :]
            rb = k_ref[pl.ds(h * h_dim + 2 * m + 1, 1), :]
            ta = pltpu.roll(jnp.broadcast_to(ra, (lc, LANES)), lc, 1, stride=1, stride_axis=0)
            tb = pltpu.roll(jnp.broadcast_to(rb, (lc, LANES)), 0, 1, stride=1, stride_axis=0)
            w_ref[h, :, m * LANES:(m + 1) * LANES] = jnp.where(left, ta, tb).astype(BF16)
        return carry

    lax.fori_loop(0, h_dim, per_h, 0)


def _ssm_ops(arow, acol, bt, c2, ct, *, groups, h_dim, p_dim):
    lc = CHUNK
    kern = functools.partial(_ssm_ops_kernel, h_dim=h_dim, p_dim=p_dim)
    blk = lambda shape: pl.BlockSpec((None,) + shape, lambda g: (g,) + (0,) * len(shape))
    return pl.pallas_call(
        kern,
        grid=(groups,),
        in_specs=[blk((2, 3, 1, LANES)), blk((2, 3, p_dim, 1)), blk((2, 2, h_dim, LANES)),
                  blk((2, 2, h_dim, LANES)), blk((2, 2, p_dim, h_dim))],
        out_specs=[blk((h_dim, lc, h_dim * lc)), blk((h_dim, lc, 2 * LANES)),
                   blk((4 * p_dim, h_dim * lc)), blk((4 * SCAN_LEVELS, LANES))],
        out_shape=[jax.ShapeDtypeStruct((groups, h_dim, lc, h_dim * lc), BF16),
                   jax.ShapeDtypeStruct((groups, h_dim, lc, 2 * LANES), BF16),
                   jax.ShapeDtypeStruct((groups, 4 * p_dim, h_dim * lc), BF16),
                   jax.ShapeDtypeStruct((groups, 4 * SCAN_LEVELS, LANES), F32)],
        scratch_shapes=[pltpu.VMEM((h_dim * h_dim, LANES), F32)],
        compiler_params=_params("arbitrary"),
        name="ssm_ops",
    )(arow, acol, bt, c2, ct)


def _chunk_scan(z, dec_ref, d, n_chunks):
    m = z.shape[0]
    cidx = lax.broadcasted_iota(jnp.int32, (m, 1), 0) & (n_chunks - 1)
    lc = CHUNK

    def shifted(x, sh):
        if d == 0:
            return jnp.where(cidx >= sh, pltpu.roll(x, sh, 0), 0.0)
        return jnp.where(cidx < n_chunks - sh, pltpu.roll(x, m - sh, 0), 0.0)

    x = z
    for k in range(n_chunks.bit_length() - 1):
        xs = shifted(x, 1 << k)
        row = (d * SCAN_LEVELS + k) * 2
        x = x + xs * dec_ref[row:row + 1, :] + pltpu.roll(xs, lc, 1) * dec_ref[row + 1:row + 2, :]
    return shifted(x, 1)


def _ssm_apply_kernel(*refs, n_seq, n_chunks, h_dim):
    u_refs = refs[:n_seq]
    w_ref, f_ref, e_ref, dec_ref = refs[n_seq:n_seq + 4]
    y_refs = refs[n_seq + 4:]
    for u_ref, y_ref, nc in zip(u_refs, y_refs, n_chunks):
        acc = None
        z = None
        for h in range(h_dim):
            uh = u_ref[h]
            a = jnp.dot(uh, w_ref[h], preferred_element_type=F32)
            b = jnp.dot(uh, f_ref[h], preferred_element_type=F32)
            acc = a if acc is None else acc + a
            z = b if z is None else z + b
        s = jnp.concatenate([_chunk_scan(z[:, :LANES], dec_ref, 0, nc),
                             _chunk_scan(z[:, LANES:], dec_ref, 1, nc)], axis=1)
        acc = acc + jnp.dot(s.astype(BF16), e_ref[...], preferred_element_type=F32)
        y_ref[...] = acc.astype(BF16)


def _ssm_apply(u3s, w, f, e, dec, *, n_chunks, h_dim):
    groups = w.shape[0]
    lc = CHUNK
    blk = lambda shape: pl.BlockSpec((None,) + shape, lambda g: (g,) + (0,) * len(shape))
    kern = functools.partial(_ssm_apply_kernel, n_seq=len(u3s), n_chunks=tuple(n_chunks), h_dim=h_dim)
    return pl.pallas_call(
        kern,
        grid=(groups,),
        in_specs=[pl.BlockSpec((h_dim, u.shape[1], lc), lambda g: (g, 0, 0)) for u in u3s]
        + [blk(w.shape[1:]), blk(f.shape[1:]), blk(e.shape[1:]), blk(dec.shape[1:])],
        out_specs=[blk((u.shape[1], h_dim * lc)) for u in u3s],
        out_shape=[jax.ShapeDtypeStruct((groups, u.shape[1], h_dim * lc), BF16) for u in u3s],
        compiler_params=_params("arbitrary"),
        name="ssm_apply",
    )(*u3s, w, f, e, dec)


def _attn_bias_kernel(rb_ref, out_ref, pair_ref, *, hd):
    hp = pl.program_id(0)
    gw = GRID_W
    cq = lax.broadcasted_iota(jnp.int32, (gw, LANES), 0)
    lane = lax.broadcasted_iota(jnp.int32, (gw, LANES), 1)
    left = lane < gw
    ck = lane & (gw - 1)
    cstart = jnp.clip(cq - WIN_COLS // 2, 0, gw - WIN_COLS)
    valid = (ck >= cstart) & (ck < cstart + WIN_COLS)
    dc = ck - cq + (WIN_COLS - 1)
    n_dr = 2 * WIN_ROWS - 1
    for head in range(2):
        hidx = hp * 2 + head
        for dr in range(n_dr - 1):
            acc = jnp.zeros((gw, LANES), F32)
            for i in range(2 * WIN_COLS - 1):
                coef = jnp.where(left, rb_ref[hidx, dr, i], rb_ref[hidx, dr + 1, i])
                acc = jnp.where(dc == i, coef, acc)
            pair_ref[head, dr] = jnp.where(valid, acc, MASK_VALUE)
    for dr0 in range(WIN_ROWS):
        for head in range(2):
            for wq in range(WIN_ROWS // 2):
                out_ref[dr0, head * gw:(head + 1) * gw, wq * LANES:(wq + 1) * LANES] = pair_ref[head, dr0 + 2 * wq]


def _attn_bias(rel_bias):
    nh = rel_bias.shape[0]
    gw = GRID_W
    return pl.pallas_call(
        functools.partial(_attn_bias_kernel, hd=gw),
        grid=(nh // 2,),
        in_specs=[pl.BlockSpec(memory_space=pltpu.SMEM)],
        out_specs=pl.BlockSpec((None, WIN_ROWS, 2 * gw, WIN_ROWS * gw), lambda h: (h, 0, 0, 0)),
        out_shape=jax.ShapeDtypeStruct((nh // 2, WIN_ROWS, 2 * gw, WIN_ROWS * gw), F32),
        scratch_shapes=[pltpu.VMEM((2, 2 * WIN_ROWS - 2, gw, LANES), F32)],
        compiler_params=_params("arbitrary"),
        name="attn_bias",
    )(rel_bias)


def _attn_kernel(q_ref, k_ref, v_ref, za_ref, bias_ref, qg_ref, kg_ref, o_ref, qn_ref, kn_ref, *, rows, hd):
    gw = GRID_W
    lane = lax.broadcasted_iota(jnp.int32, (1, 2 * hd), 1)
    left = lane < hd

    def head_norm(x, g):
        x2 = x * x
        sa = jnp.sum(jnp.where(left, x2, 0.0), axis=-1, keepdims=True)
        sb = jnp.sum(jnp.where(left, 0.0, x2), axis=-1, keepdims=True)
        ms = jnp.where(left, sa, sb) * (1.0 / hd)
        return x * lax.rsqrt(ms + NORM_EPS) * g

    def norm_body(r, carry):
        sl = pl.ds(pl.multiple_of(r * gw, gw), gw)
        qn_ref[sl, :] = (head_norm(q_ref[sl, :].astype(F32), qg_ref[...]) * (hd ** -0.5)).astype(BF16)
        kn_ref[sl, :] = head_norm(k_ref[sl, :].astype(F32), kg_ref[...]).astype(BF16)
        return carry

    lax.fori_loop(0, rows, norm_body, 0)

    def row_body(r, carry):
        rstart = jnp.clip(r - WIN_ROWS // 2, 0, rows - WIN_ROWS)
        dr0 = rstart - r + (WIN_ROWS - 1)
        qs = pl.ds(pl.multiple_of(r * gw, gw), gw)
        ks = pl.ds(pl.multiple_of(rstart * gw, gw), WIN_ROWS * gw)
        q = qn_ref[qs, :]
        zero = jnp.zeros_like(q)
        q2 = jnp.concatenate([jnp.where(left, q, zero), jnp.where(left, zero, q)], axis=0)
        s = lax.dot_general(q2, kn_ref[ks, :], (((1,), (1,)), ((), ())), preferred_element_type=F32)
        s = s + bias_ref[dr0]
        m = jnp.max(s, axis=-1, keepdims=True)
        p = jnp.exp(s - m)
        l = jnp.sum(p, axis=-1, keepdims=True)
        o2 = jnp.dot(p.astype(BF16), v_ref[ks, :], preferred_element_type=F32) / l
        o = jnp.where(left, o2[:gw], o2[gw:]) * za_ref[qs, :].astype(F32)
        o_ref[qs, :] = o.astype(BF16)
        return carry

    lax.fori_loop(0, rows, row_body, 0)


def _attention(proj3, bias, qg2, kg2, *, aw, hd):
    b, l, _ = proj3.shape
    rows = l // GRID_W
    nhp = aw // (2 * hd)
    col = lambda off: pl.BlockSpec((None, l, 2 * hd), lambda h, i, off=off: (i, 0, off * nhp + h))
    kern = functools.partial(_attn_kernel, rows=rows, hd=hd)
    return pl.pallas_call(
        kern,
        grid=(nhp, b),
        in_specs=[col(0), col(1), col(2), col(3),
                  pl.BlockSpec((None,) + bias.shape[1:], lambda h, i: (h, 0, 0, 0)),
                  pl.BlockSpec((1, 2 * hd), lambda h, i: (0, 0)),
                  pl.BlockSpec((1, 2 * hd), lambda h, i: (0, 0))],
        out_specs=pl.BlockSpec((None, l, 2 * hd), lambda h, i: (i, 0, h)),
        out_shape=jax.ShapeDtypeStruct((b, l, aw), BF16),
        scratch_shapes=[pltpu.VMEM((l, 2 * hd), BF16), pltpu.VMEM((l, 2 * hd), BF16)],
        compiler_params=_params("arbitrary", "arbitrary"),
        name="attention",
    )(proj3, proj3, proj3, proj3, bias, qg2, kg2)


def _gelu_tanh(x):
    c = math.sqrt(2.0 / math.pi)
    return 0.5 * x * (1.0 + jnp.tanh(c * (x + 0.044715 * (x * x * x))))


def _out_kernel(yt_ref, ut_ref, zt_ref, oa_ref, gs_ref, ga_ref, x_ref, d_ref, bg_ref,
                wg_ref, wbs_ref, wba_ref, wo_ref, out_ref):
    y = yt_ref[...].astype(F32) + d_ref[...] * ut_ref[...].astype(F32)
    a = _gelu_tanh(y)
    gate = jnp.dot(wg_ref[...], a.astype(BF16), preferred_element_type=F32) + bg_ref[...]
    ost = (a * _sigmoid(gate) * zt_ref[...].astype(F32)).astype(BF16)
    ms = lax.dot_general(ost, wbs_ref[...], (((0,), (0,)), ((), ())), preferred_element_type=F32)
    ma = jnp.dot(oa_ref[...], wba_ref[...], preferred_element_type=F32)
    m = gs_ref[...].astype(F32) * ms + ga_ref[...].astype(F32) * ma
    out_ref[...] = x_ref[...] + jnp.dot(m.astype(BF16), wo_ref[...], preferred_element_type=F32)


def _output(yt, projt, oa, proj, x, dcol, bgcol, wgt, wbs, wba, wo, *, sw, aw, tm):
    t, d = x.shape
    n_gs = (4 * aw) // d
    const = lambda shape: pl.BlockSpec(shape, lambda i: (0,) * len(shape), pipeline_mode=pl.Buffered(1))
    return pl.pallas_call(
        _out_kernel,
        grid=(t // tm,),
        in_specs=[
            pl.BlockSpec((sw, tm), lambda i: (0, i)),
            pl.BlockSpec((sw, tm), lambda i: (0, i)),
            pl.BlockSpec((sw, tm), lambda i: (1, i)),
            pl.BlockSpec((tm, aw), lambda i: (i, 0)),
            pl.BlockSpec((tm, d), lambda i: (i, n_gs)),
            pl.BlockSpec((tm, d), lambda i: (i, n_gs + 1)),
            pl.BlockSpec((tm, d), lambda i: (i, 0)),
            const((sw, 1)), const((sw, 1)),
            const(wgt.shape), const(wbs.shape), const(wba.shape), const(wo.shape),
        ],
        out_specs=pl.BlockSpec((tm, d), lambda i: (i, 0)),
        out_shape=jax.ShapeDtypeStruct((t, d), x.dtype),
        compiler_params=_params("arbitrary"),
        name="output",
    )(yt, projt, projt, oa, proj, proj, x, dcol, bgcol, wgt, wbs, wba, wo)


def _tile(n, pref):
    t = min(n, pref)
    while n % t:
        t //= 2
    return t


def _layer(xs, norm_g, w_in, a_re, a_im, log_dt, b_re, b_im, c_re, c_im, d_skip, w_glu, b_glu,
           q_g, k_g, rel_bias, w_bs, w_ba, w_out):
    d_model = w_in.shape[0]
    sw = d_skip.shape[0]
    groups, p_dim = a_re.shape[1], a_re.shape[2]
    h_dim = sw // groups
    hd = q_g.shape[0]
    aw = w_ba.shape[0]
    lc = CHUNK
    assert h_dim * lc % LANES == 0 and lc * 2 == LANES and h_dim % 2 == 0

    wt = w_in[:, :2 * sw].T.astype(BF16)
    wr = w_in[:, 2 * sw:].astype(BF16)
    g2 = norm_g.reshape(1, d_model).astype(F32)
    dup = lambda v: jnp.concatenate([v, v], axis=-1)
    abase = jnp.stack([a_re, a_im, jnp.broadcast_to(log_dt[..., None], a_re.shape)], axis=1)
    abase = abase.transpose(2, 0, 1, 3).astype(F32)
    arow = dup(abase)[:, :, :, None, :]
    acol = abase[..., None]
    bt = dup(jnp.stack([b_re, b_im], axis=1).transpose(2, 0, 1, 4, 3).astype(F32))
    cc = jnp.stack([c_re, c_im], axis=1).transpose(2, 0, 1, 3, 4).astype(F32)
    c2 = dup(cc)
    ct = cc.transpose(0, 1, 2, 4, 3)
    qg2 = dup(q_g.astype(F32)).reshape(1, 2 * hd)
    kg2 = dup(k_g.astype(F32)).reshape(1, 2 * hd)
    dcol = d_skip.astype(F32).reshape(sw, 1)
    bgcol = b_glu.astype(F32).reshape(sw, 1)
    wgt = w_glu.T.astype(BF16)
    wbs = w_bs.astype(BF16)
    wba = w_ba.astype(BF16)
    wo = w_out.astype(BF16)

    w_t, f_t, e_t, dec = _ssm_ops(arow, acol, bt, c2, ct, groups=groups, h_dim=h_dim, p_dim=p_dim)
    bias = _attn_bias(rel_bias.astype(F32))

    projts, projs, u3s, ncs = [], [], [], []
    for x in xs:
        b, l, _ = x.shape
        t = b * l
        assert l % (WIN_ROWS * GRID_W) == 0 and (l // lc) & (l // lc - 1) == 0 and l // lc <= 2 ** SCAN_LEVELS
        tm = _tile(t, 1024)
        tn = _tile(math.gcd(sw, aw), 1024)
        projt, proj = _inproj(x.reshape(t, d_model), g2, wt, wr, sw=sw, aw=aw, tm=tm, tn=tn)
        projts.append(projt)
        projs.append(proj)
        u3s.append(projt[:sw].reshape(sw, t // lc, lc))
        ncs.append(l // lc)

    y4s = _ssm_apply(u3s, w_t, f_t, e_t, dec, n_chunks=ncs, h_dim=h_dim)

    outs = []
    for x, projt, proj, y4 in zip(xs, projts, projs, y4s):
        b, l, _ = x.shape
        t = b * l
        yt = y4.reshape(groups, t // lc, h_dim, lc).transpose(0, 2, 1, 3).reshape(sw, t)
        oa = _attention(proj.reshape(b, l, -1), bias, qg2, kg2, aw=aw, hd=hd).reshape(t, aw)
        out = _output(yt, projt, oa, proj, x.reshape(t, d_model), dcol, bgcol, wgt, wbs, wba, wo,
                      sw=sw, aw=aw, tm=_tile(t, 256))
        outs.append(out.reshape(b, l, d_model))
    return outs


def kernel(x_prompt, x_sample, norm_g, w_in, ssm_a_re, ssm_a_im, ssm_log_dt, ssm_b_re, ssm_b_im, ssm_c_re, ssm_c_im, ssm_d, w_glu, b_glu, q_norm_g, k_norm_g, rel_bias, w_branch_ssm, w_branch_attn, w_out):
    xs = [x_prompt, x_sample]
    for layer in range(norm_g.shape[0]):
        xs = _layer(xs, norm_g[layer], w_in[layer], ssm_a_re[layer], ssm_a_im[layer], ssm_log_dt[layer],
                    ssm_b_re[layer], ssm_b_im[layer], ssm_c_re[layer], ssm_c_im[layer], ssm_d[layer],
                    w_glu[layer], b_glu[layer], q_norm_g[layer], k_norm_g[layer], rel_bias[layer],
                    w_branch_ssm[layer], w_branch_attn[layer], w_out[layer])
    return (xs[0], xs[1])
```

```python
import functools
import math

import numpy as np
import jax
import jax.numpy as jnp
from jax import lax
from jax.experimental import pallas as pl
from jax.experimental.pallas import tpu as pltpu

NORM_EPS = 1e-6
MASK_VALUE = -1e30
LOG2E = math.log2(math.e)
GRID_W = 64
WIN_ROWS = 8
WIN_COLS = 16
CHUNK = 64
LANES = 128
ROW_UNROLL = 8
SCAN_LEVELS = 7
VMEM_LIMIT_BYTES = 56 * 1024 * 1024

F32 = jnp.float32
BF16 = jnp.bfloat16
HIGHEST = lax.Precision.HIGHEST


def _params(*sem):
    return pltpu.CompilerParams(dimension_semantics=sem, vmem_limit_bytes=VMEM_LIMIT_BYTES)


def _sigmoid(x):
    return jax.nn.sigmoid(x)


def _activate(r, act):
    if act == "silu":
        return r * _sigmoid(r)
    if act == "sigmoid":
        return _sigmoid(r)
    return r


def _inproj_kernel(x_ref, g_ref, wt_ref, w_ref, outt_ref, out_ref, h_ref, *, acts_t, acts_r):
    j = pl.program_id(1)
    n_t = len(acts_t)

    @pl.when(j == 0)
    def _():
        x = x_ref[...]
        ms = jnp.mean(x * x, axis=-1, keepdims=True)
        h_ref[...] = (x * lax.rsqrt(ms + NORM_EPS) * g_ref[...]).astype(BF16)

    def tiles_with(acts, act, base):
        cond = None
        for idx, a in enumerate(acts):
            if a == act:
                c = j == base + idx
                cond = c if cond is None else cond | c
        return cond

    for act in ("none", "silu", "sigmoid"):
        cond = tiles_with(acts_t, act, 0)
        if cond is not None:
            @pl.when(cond)
            def _(act=act):
                r = lax.dot_general(wt_ref[...], h_ref[...], (((1,), (1,)), ((), ())), preferred_element_type=F32)
                outt_ref[...] = _activate(r, act).astype(BF16)
        cond = tiles_with(acts_r, act, n_t)
        if cond is not None:
            @pl.when(cond)
            def _(act=act):
                r = jnp.dot(h_ref[...], w_ref[...], preferred_element_type=F32)
                out_ref[...] = _activate(r, act).astype(BF16)


def _inproj(x, g, wt, w, *, acts_t, acts_r, tm, tn):
    t, d = x.shape
    n_t, n_r = len(acts_t), len(acts_r)
    assert wt.shape == (n_t * tn, d) and w.shape == (d, n_r * tn)
    kern = functools.partial(_inproj_kernel, acts_t=tuple(acts_t), acts_r=tuple(acts_r))
    return pl.pallas_call(
        kern,
        grid=(t // tm, n_t + n_r),
        in_specs=[
            pl.BlockSpec((tm, d), lambda i, j: (i, 0)),
            pl.BlockSpec((1, d), lambda i, j: (0, 0)),
            pl.BlockSpec((tn, d), lambda i, j: (jnp.minimum(j, n_t - 1), 0)),
            pl.BlockSpec((d, tn), lambda i, j: (0, jnp.maximum(j - n_t, 0))),
        ],
        out_specs=[
            pl.BlockSpec((tn, tm), lambda i, j: (jnp.minimum(j, n_t - 1), i)),
            pl.BlockSpec((tm, tn), lambda i, j: (i, jnp.maximum(j - n_t, 0))),
        ],
        out_shape=[jax.ShapeDtypeStruct((n_t * tn, t), BF16), jax.ShapeDtypeStruct((t, n_r * tn), BF16)],
        scratch_shapes=[pltpu.VMEM((tm, d), BF16)],
        compiler_params=_params("arbitrary", "arbitrary"),
        name="inproj",
    )(x, g, wt, w)


def _cmul(ar, ai, br, bi):
    return ar * br - ai * bi, ar * bi + ai * br


def _discretize(a_re, a_im, log_dt):
    lam_re = jnp.minimum(a_re, -1e-4)
    lam_im = a_im
    dt = jnp.exp(log_dt)
    mag = jnp.exp(lam_re * dt)
    abar_re = mag * jnp.cos(lam_im * dt)
    abar_im = mag * jnp.sin(lam_im * dt)
    den = lam_re * lam_re + lam_im * lam_im
    f_re = ((abar_re - 1.0) * lam_re + abar_im * lam_im) / den
    f_im = (abar_im * lam_re - (abar_re - 1.0) * lam_im) / den
    return abar_re, abar_im, f_re, f_im


def _squarings(ar, ai, n):
    out = [(ar, ai)]
    for _ in range(n - 1):
        ar, ai = _cmul(ar, ai, ar, ai)
        out.append((ar, ai))
    return out


def _power_table(sq, exps, shape):
    pr = jnp.ones(shape, F32)
    pi = jnp.zeros(shape, F32)
    for k, (ar, ai) in enumerate(sq):
        bit = ((exps >> k) & 1) == 1
        fr = jnp.where(bit, ar, 1.0)
        fi = jnp.where(bit, ai, 0.0)
        pr, pi = _cmul(pr, pi, fr, fi)
    return pr, pi


def _ssm_ops_kernel(arow_ref, acol_ref, bt_ref, c_ref, ct_ref, w_ref, f_ref, e_ref, dec_ref, k_ref, *, h_dim, p_dim):
    lc = CHUNK
    nbits = lc.bit_length()
    lane = lax.broadcasted_iota(jnp.int32, (1, LANES), 1)
    left = lane < lc
    tloc = lane & (lc - 1)
    sub = lax.broadcasted_iota(jnp.int32, (lc, 1), 0)

    q2 = []
    rmat = []
    bb_rows = []
    sq_rows = []
    for d in range(2):
        abr, abi, fr, fi = _discretize(arow_ref[d, 0], arow_ref[d, 1], arow_ref[d, 2])
        bbr, bbi = _cmul(fr, fi, bt_ref[d, 0], bt_ref[d, 1])
        bb_rows.append((bbr, bbi))
        sq_row = _squarings(abr, abi, nbits + SCAN_LEVELS)
        sq_rows.append(sq_row)
        cr, ci = c_ref[d, 0], c_ref[d, 1]
        blocks = []
        for h in range(h_dim):
            qr, qi = _cmul(cr, ci, bbr[h:h + 1, :], bbi[h:h + 1, :])
            blocks.append(jnp.where(left, qr, -qi))
        q2.append(jnp.concatenate(blocks, axis=0))
        cabr, cabi, _, _ = _discretize(acol_ref[d, 0], acol_ref[d, 1], acol_ref[d, 2])
        sq_col = _squarings(cabr, cabi, nbits)
        lag = lane - lc
        if d == 0:
            exps, live = jnp.maximum(lag, 0), lag >= 0
        else:
            exps, live = jnp.maximum(-lag, 0), lag <= 0
        vr, vi = _power_table(sq_col, exps, (p_dim, LANES))
        rmat.append(jnp.concatenate([jnp.where(live, vr, 0.0), jnp.where(live, vi, 0.0)], axis=0))
        eexp = tloc + 1 if d == 0 else lc - tloc
        er, ei = _power_table(sq_col, eexp, (p_dim, LANES))
        n_rep = (h_dim * lc) // LANES
        er = jnp.concatenate([er] * n_rep, axis=1)
        ei = jnp.concatenate([ei] * n_rep, axis=1)
        rep_r = lax.broadcasted_iota(jnp.int32, (h_dim, h_dim * lc), 0)
        rep_c = lax.broadcasted_iota(jnp.int32, (h_dim, h_dim * lc), 1)
        rep = ((rep_c >> (lc.bit_length() - 1)) == rep_r).astype(F32)
        cer = jnp.dot(ct_ref[d, 0], rep, precision=HIGHEST, preferred_element_type=F32)
        cei = jnp.dot(ct_ref[d, 1], rep, precision=HIGHEST, preferred_element_type=F32)
        gr, gi = _cmul(cer, cei, er, ei)
        e_ref[2 * d * p_dim:(2 * d + 1) * p_dim, :] = gr.astype(BF16)
        e_ref[(2 * d + 1) * p_dim:(2 * d + 2) * p_dim, :] = (-gi).astype(BF16)
        for k in range(SCAN_LEVELS):
            dr_, di_ = sq_row[nbits - 1 + k]
            dec_ref[(d * SCAN_LEVELS + k) * 2:(d * SCAN_LEVELS + k) * 2 + 1, :] = dr_
            dec_ref[(d * SCAN_LEVELS + k) * 2 + 1:(d * SCAN_LEVELS + k) * 2 + 2, :] = jnp.where(left, -di_, di_)

    k_ref[...] = (jnp.dot(q2[0], rmat[0], precision=HIGHEST, preferred_element_type=F32)
                  + jnp.dot(q2[1], rmat[1], precision=HIGHEST, preferred_element_type=F32))

    fexp = (lc - 1 - sub, sub)
    vrow = [_power_table(sq_rows[d][:nbits], fexp[d], (lc, LANES)) for d in range(2)]

    def per_h(h, carry):
        for d in range(2):
            bbr, bbi = bb_rows[d]
            sel = (lax.broadcasted_iota(jnp.int32, (h_dim, 1), 0) == h).astype(F32)
            br = jnp.sum(bbr * sel, axis=0, keepdims=True)
            bi = jnp.sum(bbi * sel, axis=0, keepdims=True)
            zr, zi = _cmul(vrow[d][0], vrow[d][1], br, bi)
            f_ref[h, :, d * LANES:(d + 1) * LANES] = jnp.where(left, zr, zi).astype(BF16)
        for m in range(h_dim // 2):
            ra = k_ref[pl.ds(h * h_dim + 2 * m, 1), :]
            rb = k_ref[pl.ds(h * h_dim + 2 * m + 1, 1), :]
            ta = pltpu.roll(jnp.broadcast_to(ra, (lc, LANES)), lc, 1, stride=1, stride_axis=0)
            tb = pltpu.roll(jnp.broadcast_to(rb, (lc, LANES)), 0, 1, stride=1, stride_axis=0)
            w_ref[h, :, m * LANES:(m + 1) * LANES] = jnp.where(left, ta, tb).astype(BF16)
        return carry

    lax.fori_loop(0, h_dim, per_h, 0)


def _ssm_ops(arow, acol, bt, c2, ct, *, groups, h_dim, p_dim):
    lc = CHUNK
    kern = functools.partial(_ssm_ops_kernel, h_dim=h_dim, p_dim=p_dim)
    blk = lambda shape: pl.BlockSpec((None,) + shape, lambda g: (g,) + (0,) * len(shape))
    return pl.pallas_call(
        kern,
        grid=(groups,),
        in_specs=[blk((2, 3, 1, LANES)), blk((2, 3, p_dim, 1)), blk((2, 2, h_dim, LANES)),
                  blk((2, 2, h_dim, LANES)), blk((2, 2, p_dim, h_dim))],
        out_specs=[blk((h_dim, lc, h_dim * lc)), blk((h_dim, lc, 2 * LANES)),
                   blk((4 * p_dim, h_dim * lc)), blk((4 * SCAN_LEVELS, LANES))],
        out_shape=[jax.ShapeDtypeStruct((groups, h_dim, lc, h_dim * lc), BF16),
                   jax.ShapeDtypeStruct((groups, h_dim, lc, 2 * LANES), BF16),
                   jax.ShapeDtypeStruct((groups, 4 * p_dim, h_dim * lc), BF16),
                   jax.ShapeDtypeStruct((groups, 4 * SCAN_LEVELS, LANES), F32)],
        scratch_shapes=[pltpu.VMEM((h_dim * h_dim, LANES), F32)],
        compiler_params=_params("arbitrary"),
        name="ssm_ops",
    )(arow, acol, bt, c2, ct)


def _chunk_scan(z, dec_ref, d, n_chunks):
    m = z.shape[0]
    cidx = lax.broadcasted_iota(jnp.int32, (m, 1), 0) & (n_chunks - 1)
    lc = CHUNK

    def shifted(x, sh):
        if d == 0:
            return jnp.where(cidx >= sh, pltpu.roll(x, sh, 0), 0.0)
        return jnp.where(cidx < n_chunks - sh, pltpu.roll(x, m - sh, 0), 0.0)

    x = z
    for k in range(n_chunks.bit_length() - 1):
        xs = shifted(x, 1 << k)
        row = (d * SCAN_LEVELS + k) * 2
        x = x + xs * dec_ref[row:row + 1, :] + pltpu.roll(xs, lc, 1) * dec_ref[row + 1:row + 2, :]
    return shifted(x, 1)


def _ssm_apply_kernel(*refs, n_seq, n_chunks):
    u_refs = refs[:n_seq]
    w_ref, f_ref, e_ref, dec_ref = refs[n_seq:n_seq + 4]
    y_refs = refs[n_seq + 4:]
    for u_ref, y_ref, nc in zip(u_refs, y_refs, n_chunks):
        u = u_ref[...]
        z = jnp.dot(u, f_ref[...], preferred_element_type=F32)
        s = jnp.concatenate([_chunk_scan(z[:, :LANES], dec_ref, 0, nc),
                             _chunk_scan(z[:, LANES:], dec_ref, 1, nc)], axis=1)
        acc = jnp.dot(u, w_ref[...], preferred_element_type=F32)
        acc = acc + jnp.dot(s.astype(BF16), e_ref[...], preferred_element_type=F32)
        y_ref[...] = acc.astype(BF16)


def _ssm_apply(u4s, w, f, e, dec, *, n_chunks):
    groups = w.shape[0]
    blk = lambda shape: pl.BlockSpec((None,) + shape, lambda g: (g,) + (0,) * len(shape))
    kern = functools.partial(_ssm_apply_kernel, n_seq=len(u4s), n_chunks=tuple(n_chunks))
    return pl.pallas_call(
        kern,
        grid=(groups,),
        in_specs=[blk(u.shape[1:]) for u in u4s]
        + [blk(w.shape[1:]), blk(f.shape[1:]), blk(e.shape[1:]), blk(dec.shape[1:])],
        out_specs=[blk(u.shape[1:]) for u in u4s],
        out_shape=[jax.ShapeDtypeStruct(u.shape, BF16) for u in u4s],
        compiler_params=_params("arbitrary"),
        name="ssm_apply",
    )(*u4s, w, f, e, dec)


def _attn_bias_kernel(rb_ref, out_ref, pair_ref, *, hd):
    hp = pl.program_id(0)
    gw = GRID_W
    cq = lax.broadcasted_iota(jnp.int32, (gw, LANES), 0)
    lane = lax.broadcasted_iota(jnp.int32, (gw, LANES), 1)
    left = lane < gw
    ck = lane & (gw - 1)
    cstart = jnp.clip(cq - WIN_COLS // 2, 0, gw - WIN_COLS)
    valid = (ck >= cstart) & (ck < cstart + WIN_COLS)
    dc = ck - cq + (WIN_COLS - 1)
    n_dr = 2 * WIN_ROWS - 1
    for head in range(2):
        hidx = hp * 2 + head
        for dr in range(n_dr - 1):
            acc = jnp.zeros((gw, LANES), F32)
            for i in range(2 * WIN_COLS - 1):
                coef = jnp.where(left, rb_ref[hidx, dr, i], rb_ref[hidx, dr + 1, i])
                acc = jnp.where(dc == i, coef, acc)
            pair_ref[head, dr] = jnp.where(valid, acc * LOG2E, MASK_VALUE)
    for dr0 in range(WIN_ROWS):
        for head in range(2):
            for wq in range(WIN_ROWS // 2):
                out_ref[dr0, head * gw:(head + 1) * gw, wq * LANES:(wq + 1) * LANES] = pair_ref[head, dr0 + 2 * wq]


def _attn_bias(rel_bias):
    nh = rel_bias.shape[0]
    gw = GRID_W
    return pl.pallas_call(
        functools.partial(_attn_bias_kernel, hd=gw),
        grid=(nh // 2,),
        in_specs=[pl.BlockSpec(memory_space=pltpu.SMEM)],
        out_specs=pl.BlockSpec((None, WIN_ROWS, 2 * gw, WIN_ROWS * gw), lambda h: (h, 0, 0, 0)),
        out_shape=jax.ShapeDtypeStruct((nh // 2, WIN_ROWS, 2 * gw, WIN_ROWS * gw), F32),
        scratch_shapes=[pltpu.VMEM((2, 2 * WIN_ROWS - 2, gw, LANES), F32)],
        compiler_params=_params("arbitrary"),
        name="attn_bias",
    )(rel_bias)


def _attn_kernel(q_ref, k_ref, v_ref, za_ref, bias_ref, qg_ref, kg_ref, o_ref, qn_ref, kn_ref, *, rows, hd):
    gw = GRID_W
    win = WIN_ROWS * gw
    blk = ROW_UNROLL * gw
    n_blk = rows // ROW_UNROLL
    lane = lax.broadcasted_iota(jnp.int32, (1, 2 * hd), 1)
    left = lane < hd

    def norm_block(src_ref, dst_ref, g, src_b, dst_b, scale):
        x = src_ref[pl.ds(pl.multiple_of(src_b * blk, blk), blk), :].astype(F32)
        x2 = x * x
        sa = jnp.sum(jnp.where(left, x2, 0.0), axis=-1, keepdims=True)
        sb = jnp.sum(jnp.where(left, 0.0, x2), axis=-1, keepdims=True)
        ms = jnp.where(left, sa, sb) * (1.0 / hd)
        y = x * lax.rsqrt(ms + NORM_EPS) * g
        dst_ref[pl.ds(pl.multiple_of(dst_b * blk, blk), blk), :] = (y * scale).astype(BF16)

    qscale = hd ** -0.5 * LOG2E
    norm_block(q_ref, qn_ref, qg_ref[...], 0, 0, qscale)
    norm_block(k_ref, kn_ref, kg_ref[...], 0, 0, 1.0)
    norm_block(k_ref, kn_ref, kg_ref[...], min(1, n_blk - 1), 1, 1.0)

    def rows_body(gi, carry):
        slices, scores, probs = [], [], []
        for i in range(ROW_UNROLL):
            r = gi * ROW_UNROLL + i
            rstart = jnp.clip(r - WIN_ROWS // 2, 0, rows - WIN_ROWS)
            dr0 = rstart - r + (WIN_ROWS - 1)
            qs = pl.ds(pl.multiple_of(r * gw, gw), gw)
            ks = pl.ds(pl.multiple_of(rstart * gw, gw), win)
            slices.append((qs, ks))
            q = qn_ref[qs, :]
            zero = jnp.zeros_like(q)
            q2 = jnp.concatenate([jnp.where(left, q, zero), jnp.where(left, zero, q)], axis=0)
            s = lax.dot_general(q2, kn_ref[ks, :], (((1,), (1,)), ((), ())), preferred_element_type=F32)
            scores.append(s + bias_ref[dr0])
        for s in scores:
            m = jnp.max(s, axis=-1, keepdims=True)
            p = jnp.exp2(s - m)
            probs.append((p.astype(BF16), jnp.sum(p, axis=-1, keepdims=True)))
        for (qs, ks), (p, l) in zip(slices, probs):
            o2 = jnp.dot(p, v_ref[ks, :], preferred_element_type=F32) / l
            o = jnp.where(left, o2[:gw], o2[gw:]) * za_ref[qs, :].astype(F32)
            o_ref[qs, :] = o.astype(BF16)
        norm_block(q_ref, qn_ref, qg_ref[...], jnp.minimum(gi + 1, n_blk - 1), gi + 1, qscale)
        norm_block(k_ref, kn_ref, kg_ref[...], jnp.minimum(gi + 2, n_blk - 1), gi + 2, 1.0)
        return carry

    lax.fori_loop(0, n_blk, rows_body, 0)


def _attention(proj3, bias, qg2, kg2, *, cols, aw, hd):
    b, l, _ = proj3.shape
    rows = l // GRID_W
    nhp = aw // (2 * hd)
    assert ROW_UNROLL >= WIN_ROWS // 2 and rows % ROW_UNROLL == 0
    col = lambda off: pl.BlockSpec((None, l, 2 * hd), lambda h, i, off=off: (i, 0, off + h))
    kern = functools.partial(_attn_kernel, rows=rows, hd=hd)
    blk = ROW_UNROLL * GRID_W
    return pl.pallas_call(
        kern,
        grid=(nhp, b),
        in_specs=[col(cols[0]), col(cols[1]), col(cols[2]), col(cols[3]),
                  pl.BlockSpec((None,) + bias.shape[1:], lambda h, i: (h, 0, 0, 0)),
                  pl.BlockSpec((1, 2 * hd), lambda h, i: (0, 0)),
                  pl.BlockSpec((1, 2 * hd), lambda h, i: (0, 0))],
        out_specs=pl.BlockSpec((None, l, 2 * hd), lambda h, i: (i, 0, h)),
        out_shape=jax.ShapeDtypeStruct((b, l, aw), BF16),
        scratch_shapes=[pltpu.VMEM((l + blk, 2 * hd), BF16), pltpu.VMEM((l + 2 * blk, 2 * hd), BF16)],
        compiler_params=_params("arbitrary", "arbitrary"),
        name="attention",
    )(proj3, proj3, proj3, proj3, bias, qg2, kg2)


def _gelu_tanh(x):
    c = math.sqrt(2.0 / math.pi)
    return 0.5 * x * (1.0 + jnp.tanh(c * (x + 0.044715 * (x * x * x))))


def _out_kernel(yt_ref, ut_ref, zt_ref, oa_ref, gs_ref, ga_ref, x_ref, d_ref, bg_ref,
                wg_ref, wbs_ref, wba_ref, wo_ref, out_ref):
    y = yt_ref[...].astype(F32) + d_ref[...] * ut_ref[...].astype(F32)
    a = _gelu_tanh(y)
    gate = jnp.dot(wg_ref[...], a.astype(BF16), preferred_element_type=F32) + bg_ref[...]
    ost = (a * _sigmoid(gate) * zt_ref[...].astype(F32)).astype(BF16)
    ms = lax.dot_general(ost, wbs_ref[...], (((0,), (0,)), ((), ())), preferred_element_type=F32)
    ma = jnp.dot(oa_ref[...], wba_ref[...], preferred_element_type=F32)
    m = gs_ref[...].astype(F32) * ms + ga_ref[...].astype(F32) * ma
    out_ref[...] = x_ref[...] + jnp.dot(m.astype(BF16), wo_ref[...], preferred_element_type=F32)


def _output(yt, projt, oa, proj, x, dcol, bgcol, wgt, wbs, wba, wo, *, sw, aw, tm):
    t, d = x.shape
    const = lambda shape: pl.BlockSpec(shape, lambda i: (0,) * len(shape), pipeline_mode=pl.Buffered(1))
    return pl.pallas_call(
        _out_kernel,
        grid=(t // tm,),
        in_specs=[
            pl.BlockSpec((sw, tm), lambda i: (0, i)),
            pl.BlockSpec((sw, tm), lambda i: (0, i)),
            pl.BlockSpec((sw, tm), lambda i: (1, i)),
            pl.BlockSpec((tm, aw), lambda i: (i, 0)),
            pl.BlockSpec((tm, d), lambda i: (i, 0)),
            pl.BlockSpec((tm, d), lambda i: (i, 1)),
            pl.BlockSpec((tm, d), lambda i: (i, 0)),
            const((sw, 1)), const((sw, 1)),
            const(wgt.shape), const(wbs.shape), const(wba.shape), const(wo.shape),
        ],
        out_specs=pl.BlockSpec((tm, d), lambda i: (i, 0)),
        out_shape=jax.ShapeDtypeStruct((t, d), x.dtype),
        compiler_params=_params("arbitrary"),
        name="output",
    )(yt, projt, projt, oa, proj, proj, x, dcol, bgcol, wgt, wbs, wba, wo)


def _tile(n, pref):
    t = min(n, pref)
    while n % t:
        t //= 2
    return t


def _layer(xs, norm_g, w_in, a_re, a_im, log_dt, b_re, b_im, c_re, c_im, d_skip, w_glu, b_glu,
           q_g, k_g, rel_bias, w_bs, w_ba, w_out):
    d_model = w_in.shape[0]
    sw = d_skip.shape[0]
    groups, p_dim = a_re.shape[1], a_re.shape[2]
    h_dim = sw // groups
    hd = q_g.shape[0]
    aw = w_ba.shape[0]
    lc = CHUNK
    assert h_dim * lc % LANES == 0 and lc * 2 == LANES and h_dim % 2 == 0

    o_q, o_gs, o_end = 2 * sw, 2 * sw + 4 * aw, 2 * sw + 4 * aw + 2 * d_model
    wt = w_in[:, :o_q].T.astype(BF16)
    wr = jnp.concatenate([w_in[:, o_gs:o_end], w_in[:, o_q:o_gs]], axis=1).astype(BF16)
    tn = _tile(math.gcd(sw, aw), 1024)
    acts_t = ["none"] * (sw // tn) + ["silu"] * (sw // tn)
    acts_r = ["sigmoid"] * (2 * d_model // tn) + ["none"] * (3 * aw // tn) + ["silu"] * (aw // tn)
    pair = 2 * hd
    attn_cols = tuple((2 * d_model + n * aw) // pair for n in range(4))
    g2 = norm_g.reshape(1, d_model).astype(F32)
    dup = lambda v: jnp.concatenate([v, v], axis=-1)
    abase = jnp.stack([a_re, a_im, jnp.broadcast_to(log_dt[..., None], a_re.shape)], axis=1)
    abase = abase.transpose(2, 0, 1, 3).astype(F32)
    arow = dup(abase)[:, :, :, None, :]
    acol = abase[..., None]
    bt = dup(jnp.stack([b_re, b_im], axis=1).transpose(2, 0, 1, 4, 3).astype(F32))
    cc = jnp.stack([c_re, c_im], axis=1).transpose(2, 0, 1, 3, 4).astype(F32)
    c2 = dup(cc)
    ct = cc.transpose(0, 1, 2, 4, 3)
    qg2 = dup(q_g.astype(F32)).reshape(1, pair)
    kg2 = dup(k_g.astype(F32)).reshape(1, pair)
    dcol = d_skip.astype(F32).reshape(sw, 1)
    bgcol = b_glu.astype(F32).reshape(sw, 1)
    wgt = w_glu.T.astype(BF16)
    wbs = w_bs.astype(BF16)
    wba = w_ba.astype(BF16)
    wo = w_out.astype(BF16)

    w_t, f_t, e_t, dec = _ssm_ops(arow, acol, bt, c2, ct, groups=groups, h_dim=h_dim, p_dim=p_dim)
    bias = _attn_bias(rel_bias.astype(F32))

    projts, projs, u3s, ncs = [], [], [], []
    for x in xs:
        b, l, _ = x.shape
        t = b * l
        assert l % (GRID_W * ROW_UNROLL) == 0 and l % (WIN_ROWS * GRID_W) == 0
        assert (l // lc) & (l // lc - 1) == 0 and l // lc <= 2 ** SCAN_LEVELS
        projt, proj = _inproj(x.reshape(t, d_model), g2, wt, wr, acts_t=acts_t, acts_r=acts_r,
                              tm=_tile(t, 1024), tn=tn)
        projts.append(projt)
        projs.append(proj)
        u3s.append(projt[:sw].reshape(groups, h_dim, t // lc, lc).transpose(0, 2, 1, 3).reshape(groups, t // lc, h_dim * lc))
        ncs.append(l // lc)

    y4s = _ssm_apply(u3s, w_t.reshape(groups, h_dim * lc, h_dim * lc), f_t.reshape(groups, h_dim * lc, 2 * LANES),
                     e_t, dec, n_chunks=ncs)

    outs = []
    for x, projt, proj, y4 in zip(xs, projts, projs, y4s):
        b, l, _ = x.shape
        t = b * l
        yt = y4.reshape(groups, t // lc, h_dim, lc).transpose(0, 2, 1, 3).reshape(sw, t)
        oa = _attention(proj.reshape(b, l, -1), bias, qg2, kg2, cols=attn_cols, aw=aw, hd=hd).reshape(t, aw)
        out = _output(yt, projt, oa, proj, x.reshape(t, d_model), dcol, bgcol, wgt, wbs, wba, wo,
                      sw=sw, aw=aw, tm=_tile(t, 256))
        outs.append(out.reshape(b, l, d_model))
    return outs


def kernel(x_prompt, x_sample, norm_g, w_in, ssm_a_re, ssm_a_im, ssm_log_dt, ssm_b_re, ssm_b_im, ssm_c_re, ssm_c_im, ssm_d, w_glu, b_glu, q_norm_g, k_norm_g, rel_bias, w_branch_ssm, w_branch_attn, w_out):
    xs = [x_prompt, x_sample]
    for layer in range(norm_g.shape[0]):
        xs = _layer(xs, norm_g[layer], w_in[layer], ssm_a_re[layer], ssm_a_im[layer], ssm_log_dt[layer],
                    ssm_b_re[layer], ssm_b_im[layer], ssm_c_re[layer], ssm_c_im[layer], ssm_d[layer],
                    w_glu[layer], b_glu[layer], q_norm_g[layer], k_norm_g[layer], rel_bias[layer],
                    w_branch_ssm[layer], w_branch_attn[layer], w_out[layer])
    return (xs[0], xs[1])
```

```python
import functools
import math

import numpy as np
import jax
import jax.numpy as jnp
from jax import lax
from jax.experimental import pallas as pl
from jax.experimental.pallas import tpu as pltpu

NORM_EPS = 1e-6
MASK_VALUE = -1e30
LOG2E = math.log2(math.e)
GRID_W = 64
WIN_ROWS = 8
WIN_COLS = 16
CHUNK = 64
LANES = 128
ROW_UNROLL = 8
SCAN_LEVELS = 7
VMEM_LIMIT_BYTES = 56 * 1024 * 1024

F32 = jnp.float32
BF16 = jnp.bfloat16
HIGHEST = lax.Precision.HIGHEST


def _params(*sem):
    return pltpu.CompilerParams(dimension_semantics=sem, vmem_limit_bytes=VMEM_LIMIT_BYTES)


def _sigmoid(x):
    return jax.nn.sigmoid(x)


def _activate(r, act):
    if act == "silu":
        return r * _sigmoid(r)
    if act == "sigmoid":
        return _sigmoid(r)
    return r


def _inproj_kernel(x_ref, g_ref, wt_ref, w_ref, outt_ref, out_ref, h_ref, *, acts_t, acts_r):
    j = pl.program_id(1)
    n_t = len(acts_t)

    @pl.when(j == 0)
    def _():
        x = x_ref[...]
        ms = jnp.mean(x * x, axis=-1, keepdims=True)
        h_ref[...] = (x * lax.rsqrt(ms + NORM_EPS) * g_ref[...]).astype(BF16)

    def tiles_with(acts, act, base):
        cond = None
        for idx, a in enumerate(acts):
            if a == act:
                c = j == base + idx
                cond = c if cond is None else cond | c
        return cond

    for act in ("none", "silu", "sigmoid"):
        cond = tiles_with(acts_t, act, 0)
        if cond is not None:
            @pl.when(cond)
            def _(act=act):
                r = lax.dot_general(wt_ref[...], h_ref[...], (((1,), (1,)), ((), ())), preferred_element_type=F32)
                outt_ref[...] = _activate(r, act).astype(BF16)
        cond = tiles_with(acts_r, act, n_t)
        if cond is not None:
            @pl.when(cond)
            def _(act=act):
                r = jnp.dot(h_ref[...], w_ref[...], preferred_element_type=F32)
                out_ref[...] = _activate(r, act).astype(BF16)


def _inproj(x, g, wt, w, *, acts_t, acts_r, tm, tn):
    t, d = x.shape
    n_t, n_r = len(acts_t), len(acts_r)
    assert wt.shape == (n_t * tn, d) and w.shape == (d, n_r * tn)
    kern = functools.partial(_inproj_kernel, acts_t=tuple(acts_t), acts_r=tuple(acts_r))
    return pl.pallas_call(
        kern,
        grid=(t // tm, n_t + n_r),
        in_specs=[
            pl.BlockSpec((tm, d), lambda i, j: (i, 0)),
            pl.BlockSpec((1, d), lambda i, j: (0, 0)),
            pl.BlockSpec((tn, d), lambda i, j: (jnp.minimum(j, n_t - 1), 0)),
            pl.BlockSpec((d, tn), lambda i, j: (0, jnp.maximum(j - n_t, 0))),
        ],
        out_specs=[
            pl.BlockSpec((tn, tm), lambda i, j: (jnp.minimum(j, n_t - 1), i)),
            pl.BlockSpec((tm, tn), lambda i, j: (i, jnp.maximum(j - n_t, 0))),
        ],
        out_shape=[jax.ShapeDtypeStruct((n_t * tn, t), BF16), jax.ShapeDtypeStruct((t, n_r * tn), BF16)],
        scratch_shapes=[pltpu.VMEM((tm, d), BF16)],
        compiler_params=_params("arbitrary", "arbitrary"),
        name="inproj",
    )(x, g, wt, w)


def _cmul(ar, ai, br, bi):
    return ar * br - ai * bi, ar * bi + ai * br


def _hi_lo(x):
    if x.dtype == BF16:
        return x, None
    hi = x.astype(BF16)
    return hi, (x - hi.astype(F32)).astype(BF16)


def _dot_hi_lo(a, b):
    ah, al = _hi_lo(a)
    bh, bl = _hi_lo(b)
    acc = jnp.dot(ah, bh, preferred_element_type=F32)
    if bl is not None:
        acc = acc + jnp.dot(ah, bl, preferred_element_type=F32)
    if al is not None:
        acc = acc + jnp.dot(al, bh, preferred_element_type=F32)
    return acc


def _discretize(a_re, a_im, log_dt):
    lam_re = jnp.minimum(a_re, -1e-4)
    lam_im = a_im
    dt = jnp.exp(log_dt)
    mag = jnp.exp(lam_re * dt)
    abar_re = mag * jnp.cos(lam_im * dt)
    abar_im = mag * jnp.sin(lam_im * dt)
    den = lam_re * lam_re + lam_im * lam_im
    f_re = ((abar_re - 1.0) * lam_re + abar_im * lam_im) / den
    f_im = (abar_im * lam_re - (abar_re - 1.0) * lam_im) / den
    return abar_re, abar_im, f_re, f_im


def _squarings(ar, ai, n):
    out = [(ar, ai)]
    for _ in range(n - 1):
        ar, ai = _cmul(ar, ai, ar, ai)
        out.append((ar, ai))
    return out


def _power_table(sq, exps, shape):
    pr = jnp.ones(shape, F32)
    pi = jnp.zeros(shape, F32)
    for k, (ar, ai) in enumerate(sq):
        bit = ((exps >> k) & 1) == 1
        fr = jnp.where(bit, ar, 1.0)
        fi = jnp.where(bit, ai, 0.0)
        pr, pi = _cmul(pr, pi, fr, fi)
    return pr, pi


def _ssm_ops_kernel(arow_ref, acol_ref, bt_ref, c_ref, ct_ref, w_ref, f_ref, e_ref, dec_ref, k_ref, *, h_dim, p_dim):
    lc = CHUNK
    nbits = lc.bit_length()
    lane = lax.broadcasted_iota(jnp.int32, (1, LANES), 1)
    left = lane < lc
    tloc = lane & (lc - 1)
    sub = lax.broadcasted_iota(jnp.int32, (lc, 1), 0)

    q2 = []
    rmat = []
    bb_rows = []
    sq_rows = []
    for d in range(2):
        abr, abi, fr, fi = _discretize(arow_ref[d, 0], arow_ref[d, 1], arow_ref[d, 2])
        bbr, bbi = _cmul(fr, fi, bt_ref[d, 0], bt_ref[d, 1])
        bb_rows.append((bbr, bbi))
        sq_row = _squarings(abr, abi, nbits + SCAN_LEVELS)
        sq_rows.append(sq_row)
        cr, ci = c_ref[d, 0], c_ref[d, 1]
        blocks = []
        for h in range(h_dim):
            qr, qi = _cmul(cr, ci, bbr[h:h + 1, :], bbi[h:h + 1, :])
            blocks.append(jnp.where(left, qr, -qi))
        q2.append(jnp.concatenate(blocks, axis=0))
        cabr, cabi, _, _ = _discretize(acol_ref[d, 0], acol_ref[d, 1], acol_ref[d, 2])
        sq_col = _squarings(cabr, cabi, nbits)
        lag = lane - lc
        if d == 0:
            exps, live = jnp.maximum(lag, 0), lag >= 0
        else:
            exps, live = jnp.maximum(-lag, 0), lag <= 0
        vr, vi = _power_table(sq_col, exps, (p_dim, LANES))
        rmat.append(jnp.concatenate([jnp.where(live, vr, 0.0), jnp.where(live, vi, 0.0)], axis=0))
        eexp = tloc + 1 if d == 0 else lc - tloc
        er, ei = _power_table(sq_col, eexp, (p_dim, LANES))
        n_rep = (h_dim * lc) // LANES
        er = jnp.concatenate([er] * n_rep, axis=1)
        ei = jnp.concatenate([ei] * n_rep, axis=1)
        rep_r = lax.broadcasted_iota(jnp.int32, (h_dim, h_dim * lc), 0)
        rep_c = lax.broadcasted_iota(jnp.int32, (h_dim, h_dim * lc), 1)
        rep = ((rep_c >> (lc.bit_length() - 1)) == rep_r).astype(BF16)
        cer = _dot_hi_lo(ct_ref[d, 0], rep)
        cei = _dot_hi_lo(ct_ref[d, 1], rep)
        gr, gi = _cmul(cer, cei, er, ei)
        e_ref[2 * d * p_dim:(2 * d + 1) * p_dim, :] = gr.astype(BF16)
        e_ref[(2 * d + 1) * p_dim:(2 * d + 2) * p_dim, :] = (-gi).astype(BF16)
        for k in range(SCAN_LEVELS):
            dr_, di_ = sq_row[nbits - 1 + k]
            dec_ref[(d * SCAN_LEVELS + k) * 2:(d * SCAN_LEVELS + k) * 2 + 1, :] = dr_
            dec_ref[(d * SCAN_LEVELS + k) * 2 + 1:(d * SCAN_LEVELS + k) * 2 + 2, :] = jnp.where(left, -di_, di_)

    k_ref[...] = _dot_hi_lo(q2[0], rmat[0]) + _dot_hi_lo(q2[1], rmat[1])

    fexp = (lc - 1 - sub, sub)
    vrow = [_power_table(sq_rows[d][:nbits], fexp[d], (lc, LANES)) for d in range(2)]

    def per_h(h, carry):
        for d in range(2):
            bbr, bbi = bb_rows[d]
            sel = (lax.broadcasted_iota(jnp.int32, (h_dim, 1), 0) == h).astype(F32)
            br = jnp.sum(bbr * sel, axis=0, keepdims=True)
            bi = jnp.sum(bbi * sel, axis=0, keepdims=True)
            zr, zi = _cmul(vrow[d][0], vrow[d][1], br, bi)
            f_ref[h, :, d * LANES:(d + 1) * LANES] = jnp.where(left, zr, zi).astype(BF16)
        for m in range(h_dim // 2):
            ra = k_ref[pl.ds(h * h_dim + 2 * m, 1), :]
            rb = k_ref[pl.ds(h * h_dim + 2 * m + 1, 1), :]
            ta = pltpu.roll(jnp.broadcast_to(ra, (lc, LANES)), lc, 1, stride=1, stride_axis=0)
            tb = pltpu.roll(jnp.broadcast_to(rb, (lc, LANES)), 0, 1, stride=1, stride_axis=0)
            w_ref[h, :, m * LANES:(m + 1) * LANES] = jnp.where(left, ta, tb).astype(BF16)
        return carry

    lax.fori_loop(0, h_dim, per_h, 0)


def _ssm_ops(arow, acol, bt, c2, ct, *, groups, h_dim, p_dim):
    lc = CHUNK
    kern = functools.partial(_ssm_ops_kernel, h_dim=h_dim, p_dim=p_dim)
    blk = lambda shape: pl.BlockSpec((None,) + shape, lambda g: (g,) + (0,) * len(shape))
    return pl.pallas_call(
        kern,
        grid=(groups,),
        in_specs=[blk((2, 3, 1, LANES)), blk((2, 3, p_dim, 1)), blk((2, 2, h_dim, LANES)),
                  blk((2, 2, h_dim, LANES)), blk((2, 2, p_dim, h_dim))],
        out_specs=[blk((h_dim, lc, h_dim * lc)), blk((h_dim, lc, 2 * LANES)),
                   blk((4 * p_dim, h_dim * lc)), blk((4 * SCAN_LEVELS, LANES))],
        out_shape=[jax.ShapeDtypeStruct((groups, h_dim, lc, h_dim * lc), BF16),
                   jax.ShapeDtypeStruct((groups, h_dim, lc, 2 * LANES), BF16),
                   jax.ShapeDtypeStruct((groups, 4 * p_dim, h_dim * lc), BF16),
                   jax.ShapeDtypeStruct((groups, 4 * SCAN_LEVELS, LANES), F32)],
        scratch_shapes=[pltpu.VMEM((h_dim * h_dim, LANES), F32)],
        compiler_params=_params("arbitrary"),
        name="ssm_ops",
    )(arow, acol, bt, c2, ct)


def _decay_mul(x, dec_ref, d, level):
    row = (d * SCAN_LEVELS + level) * 2
    return x * dec_ref[row:row + 1, :] + pltpu.roll(x, CHUNK, 1) * dec_ref[row + 1:row + 2, :]


def _pair_scan(z, dec_ref, d, n_pairs):
    m = z.shape[0]
    cidx = lax.broadcasted_iota(jnp.int32, (m, 1), 0) & (n_pairs - 1)

    def shifted(x, sh):
        if d == 0:
            return jnp.where(cidx >= sh, pltpu.roll(x, sh, 0), 0.0)
        return jnp.where(cidx < n_pairs - sh, pltpu.roll(x, m - sh, 0), 0.0)

    x = z
    for k in range(n_pairs.bit_length() - 1):
        x = x + _decay_mul(shifted(x, 1 << k), dec_ref, d, k + 1)
    return shifted(x, 1)


def _lane_halves(a, b, left):
    return jnp.where(left, a, pltpu.roll(b, CHUNK, 1)), jnp.where(left, pltpu.roll(a, CHUNK, 1), b)


def _ssm_apply_kernel(*refs, n_seq, n_chunks, h_dim):
    u_refs = refs[:n_seq]
    w_ref, f_ref, e_ref, dec_ref = refs[n_seq:n_seq + 4]
    y_refs = refs[n_seq + 4:]
    left = lax.broadcasted_iota(jnp.int32, (1, LANES), 1) < CHUNK
    tiles = lambda x: [x[:, i * LANES:(i + 1) * LANES] for i in range(x.shape[1] // LANES)]
    for u_ref, y_ref, nc in zip(u_refs, y_refs, n_chunks):
        half = u_ref.shape[0]
        uw = tiles(pltpu.bitcast(u_ref[...], jnp.uint32))
        ev, od = zip(*[_lane_halves(uw[2 * i], uw[2 * i + 1], left) for i in range(h_dim // 2)])
        u = jnp.concatenate([pltpu.bitcast(jnp.concatenate(ev, axis=1), BF16),
                             pltpu.bitcast(jnp.concatenate(od, axis=1), BF16)], axis=0)
        z = jnp.dot(u, f_ref[...], preferred_element_type=F32)
        ze, zo = z[:half], z[half:]
        pf = _pair_scan(_decay_mul(ze[:, :LANES], dec_ref, 0, 0) + zo[:, :LANES], dec_ref, 0, nc // 2)
        sf = jnp.concatenate([pf, _decay_mul(pf, dec_ref, 0, 0) + ze[:, :LANES]], axis=0)
        pb = _pair_scan(ze[:, LANES:] + _decay_mul(zo[:, LANES:], dec_ref, 1, 0), dec_ref, 1, nc // 2)
        sb = jnp.concatenate([_decay_mul(pb, dec_ref, 1, 0) + zo[:, LANES:], pb], axis=0)
        s = jnp.concatenate([sf, sb], axis=1)
        acc = jnp.dot(u, w_ref[...], preferred_element_type=F32)
        acc = acc + jnp.dot(s.astype(BF16), e_ref[...], preferred_element_type=F32)
        yb = acc.astype(BF16)
        ye = tiles(pltpu.bitcast(yb[:half], jnp.uint32))
        yo = tiles(pltpu.bitcast(yb[half:], jnp.uint32))
        out = []
        for i in range(h_dim // 2):
            out.extend(_lane_halves(ye[i], yo[i], left))
        y_ref[...] = pltpu.bitcast(jnp.concatenate(out, axis=1), BF16)


def _ssm_apply(u4s, w, f, e, dec, *, n_chunks, h_dim):
    groups = w.shape[0]
    blk = lambda shape: pl.BlockSpec((None,) + shape, lambda g: (g,) + (0,) * len(shape))
    kern = functools.partial(_ssm_apply_kernel, n_seq=len(u4s), n_chunks=tuple(n_chunks), h_dim=h_dim)
    return pl.pallas_call(
        kern,
        grid=(groups,),
        in_specs=[blk(u.shape[1:]) for u in u4s]
        + [blk(w.shape[1:]), blk(f.shape[1:]), blk(e.shape[1:]), blk(dec.shape[1:])],
        out_specs=[blk(u.shape[1:]) for u in u4s],
        out_shape=[jax.ShapeDtypeStruct(u.shape, BF16) for u in u4s],
        compiler_params=_params("arbitrary"),
        name="ssm_apply",
    )(*u4s, w, f, e, dec)


def _attn_bias_kernel(rb_ref, out_ref, pair_ref, *, hd):
    hp = pl.program_id(0)
    gw = GRID_W
    cq = lax.broadcasted_iota(jnp.int32, (gw, LANES), 0)
    lane = lax.broadcasted_iota(jnp.int32, (gw, LANES), 1)
    left = lane < gw
    ck = lane & (gw - 1)
    cstart = jnp.clip(cq - WIN_COLS // 2, 0, gw - WIN_COLS)
    valid = (ck >= cstart) & (ck < cstart + WIN_COLS)
    dc = ck - cq + (WIN_COLS - 1)
    n_dr = 2 * WIN_ROWS - 1
    for head in range(2):
        hidx = hp * 2 + head
        for dr in range(n_dr - 1):
            acc = jnp.zeros((gw, LANES), F32)
            for i in range(2 * WIN_COLS - 1):
                coef = jnp.where(left, rb_ref[hidx, dr, i], rb_ref[hidx, dr + 1, i])
                acc = jnp.where(dc == i, coef, acc)
            pair_ref[head, dr] = jnp.where(valid, acc * LOG2E, MASK_VALUE)
    for dr0 in range(WIN_ROWS):
        for head in range(2):
            for wq in range(WIN_ROWS // 2):
                out_ref[dr0, head * gw:(head + 1) * gw, wq * LANES:(wq + 1) * LANES] = pair_ref[head, dr0 + 2 * wq]


def _attn_bias(rel_bias):
    nh = rel_bias.shape[0]
    gw = GRID_W
    return pl.pallas_call(
        functools.partial(_attn_bias_kernel, hd=gw),
        grid=(nh // 2,),
        in_specs=[pl.BlockSpec(memory_space=pltpu.SMEM)],
        out_specs=pl.BlockSpec((None, WIN_ROWS, 2 * gw, WIN_ROWS * gw), lambda h: (h, 0, 0, 0)),
        out_shape=jax.ShapeDtypeStruct((nh // 2, WIN_ROWS, 2 * gw, WIN_ROWS * gw), F32),
        scratch_shapes=[pltpu.VMEM((2, 2 * WIN_ROWS - 2, gw, LANES), F32)],
        compiler_params=_params("arbitrary"),
        name="attn_bias",
    )(rel_bias)


def _attn_kernel(q_ref, k_ref, v_ref, za_ref, bias_ref, qg_ref, kg_ref, o_ref, qn_ref, kn_ref, *, rows, hd):
    gw = GRID_W
    win = WIN_ROWS * gw
    blk = ROW_UNROLL * gw
    n_blk = rows // ROW_UNROLL
    lane = lax.broadcasted_iota(jnp.int32, (1, 2 * hd), 1)
    left = lane < hd

    def norm_block(src_ref, dst_ref, g, src_b, dst_b, scale):
        x = src_ref[pl.ds(pl.multiple_of(src_b * blk, blk), blk), :].astype(F32)
        x2 = x * x
        sa = jnp.sum(jnp.where(left, x2, 0.0), axis=-1, keepdims=True)
        sb = jnp.sum(jnp.where(left, 0.0, x2), axis=-1, keepdims=True)
        ms = jnp.where(left, sa, sb) * (1.0 / hd)
        y = x * lax.rsqrt(ms + NORM_EPS) * g
        dst_ref[pl.ds(pl.multiple_of(dst_b * blk, blk), blk), :] = (y * scale).astype(BF16)

    qscale = hd ** -0.5 * LOG2E
    norm_block(q_ref, qn_ref, qg_ref[...], 0, 0, qscale)
    norm_block(k_ref, kn_ref, kg_ref[...], 0, 0, 1.0)
    norm_block(k_ref, kn_ref, kg_ref[...], min(1, n_blk - 1), 1, 1.0)

    def rows_body(gi, carry):
        slices, scores, probs = [], [], []
        for i in range(ROW_UNROLL):
            r = gi * ROW_UNROLL + i
            rstart = jnp.clip(r - WIN_ROWS // 2, 0, rows - WIN_ROWS)
            dr0 = rstart - r + (WIN_ROWS - 1)
            qs = pl.ds(pl.multiple_of(r * gw, gw), gw)
            ks = pl.ds(pl.multiple_of(rstart * gw, gw), win)
            slices.append((qs, ks))
            q = qn_ref[qs, :]
            zero = jnp.zeros_like(q)
            q2 = jnp.concatenate([jnp.where(left, q, zero), jnp.where(left, zero, q)], axis=0)
            s = lax.dot_general(q2, kn_ref[ks, :], (((1,), (1,)), ((), ())), preferred_element_type=F32)
            scores.append(s + bias_ref[dr0])
        for s in scores:
            m = jnp.max(s, axis=-1, keepdims=True)
            p = jnp.exp2(s - m)
            probs.append((p.astype(BF16), jnp.sum(p, axis=-1, keepdims=True)))
        for (qs, ks), (p, l) in zip(slices, probs):
            o2 = jnp.dot(p, v_ref[ks, :], preferred_element_type=F32) / l
            o = jnp.where(left, o2[:gw], o2[gw:]) * za_ref[qs, :].astype(F32)
            o_ref[qs, :] = o.astype(BF16)
        norm_block(q_ref, qn_ref, qg_ref[...], jnp.minimum(gi + 1, n_blk - 1), gi + 1, qscale)
        norm_block(k_ref, kn_ref, kg_ref[...], jnp.minimum(gi + 2, n_blk - 1), gi + 2, 1.0)
        return carry

    lax.fori_loop(0, n_blk, rows_body, 0)


def _attention(proj3, bias, qg2, kg2, *, cols, aw, hd):
    b, l, _ = proj3.shape
    rows = l // GRID_W
    nhp = aw // (2 * hd)
    assert ROW_UNROLL >= WIN_ROWS // 2 and rows % ROW_UNROLL == 0
    col = lambda off: pl.BlockSpec((None, l, 2 * hd), lambda h, i, off=off: (i, 0, off + h))
    kern = functools.partial(_attn_kernel, rows=rows, hd=hd)
    blk = ROW_UNROLL * GRID_W
    return pl.pallas_call(
        kern,
        grid=(nhp, b),
        in_specs=[col(cols[0]), col(cols[1]), col(cols[2]), col(cols[3]),
                  pl.BlockSpec((None,) + bias.shape[1:], lambda h, i: (h, 0, 0, 0)),
                  pl.BlockSpec((1, 2 * hd), lambda h, i: (0, 0)),
                  pl.BlockSpec((1, 2 * hd), lambda h, i: (0, 0))],
        out_specs=pl.BlockSpec((None, l, 2 * hd), lambda h, i: (i, 0, h)),
        out_shape=jax.ShapeDtypeStruct((b, l, aw), BF16),
        scratch_shapes=[pltpu.VMEM((l + blk, 2 * hd), BF16), pltpu.VMEM((l + 2 * blk, 2 * hd), BF16)],
        compiler_params=_params("arbitrary", "arbitrary"),
        name="attention",
    )(proj3, proj3, proj3, proj3, bias, qg2, kg2)


def _gelu_tanh(x):
    c = math.sqrt(2.0 / math.pi)
    return 0.5 * x * (1.0 + jnp.tanh(c * (x + 0.044715 * (x * x * x))))


def _out_kernel(yt_ref, ut_ref, zt_ref, oa_ref, gs_ref, ga_ref, x_ref, d_ref, bg_ref,
                wg_ref, wbs_ref, wba_ref, wo_ref, out_ref):
    y = yt_ref[...].astype(F32) + d_ref[...] * ut_ref[...].astype(F32)
    a = _gelu_tanh(y)
    gate = jnp.dot(wg_ref[...], a.astype(BF16), preferred_element_type=F32) + bg_ref[...]
    ost = (a * _sigmoid(gate) * zt_ref[...].astype(F32)).astype(BF16)
    ms = lax.dot_general(ost, wbs_ref[...], (((0,), (0,)), ((), ())), preferred_element_type=F32)
    ma = jnp.dot(oa_ref[...], wba_ref[...], preferred_element_type=F32)
    m = gs_ref[...].astype(F32) * ms + ga_ref[...].astype(F32) * ma
    out_ref[...] = x_ref[...] + jnp.dot(m.astype(BF16), wo_ref[...], preferred_element_type=F32)


def _output(yt, projt, oa, proj, x, dcol, bgcol, wgt, wbs, wba, wo, *, sw, aw, tm):
    t, d = x.shape
    assert (4 * aw) % d == 0
    n_gs = (4 * aw) // d
    const = lambda shape: pl.BlockSpec(shape, lambda i: (0,) * len(shape), pipeline_mode=pl.Buffered(1))
    return pl.pallas_call(
        _out_kernel,
        grid=(t // tm,),
        in_specs=[
            pl.BlockSpec((sw, tm), lambda i: (0, i)),
            pl.BlockSpec((sw, tm), lambda i: (0, i)),
            pl.BlockSpec((sw, tm), lambda i: (1, i)),
            pl.BlockSpec((tm, aw), lambda i: (i, 0)),
            pl.BlockSpec((tm, d), lambda i: (i, n_gs)),
            pl.BlockSpec((tm, d), lambda i: (i, n_gs + 1)),
            pl.BlockSpec((tm, d), lambda i: (i, 0)),
            const((sw, 1)), const((sw, 1)),
            const(wgt.shape), const(wbs.shape), const(wba.shape), const(wo.shape),
        ],
        out_specs=pl.BlockSpec((tm, d), lambda i: (i, 0)),
        out_shape=jax.ShapeDtypeStruct((t, d), x.dtype),
        compiler_params=_params("arbitrary"),
        name="output",
    )(yt, projt, projt, oa, proj, proj, x, dcol, bgcol, wgt, wbs, wba, wo)


def _tile(n, pref):
    t = min(n, pref)
    while n % t:
        t //= 2
    return t


def _layer(xs, norm_g, w_in, a_re, a_im, log_dt, b_re, b_im, c_re, c_im, d_skip, w_glu, b_glu,
           q_g, k_g, rel_bias, w_bs, w_ba, w_out):
    d_model = w_in.shape[0]
    sw = d_skip.shape[0]
    groups, p_dim = a_re.shape[1], a_re.shape[2]
    h_dim = sw // groups
    hd = q_g.shape[0]
    aw = w_ba.shape[0]
    lc = CHUNK
    assert h_dim * lc % LANES == 0 and lc * 2 == LANES and h_dim % 2 == 0

    wt = w_in[:, :2 * sw].T.astype(BF16)
    wr = w_in[:, 2 * sw:].astype(BF16)
    tn = _tile(math.gcd(sw, aw), 1024)
    acts_t = ["none"] * (sw // tn) + ["silu"] * (sw // tn)
    acts_r = ["none"] * (3 * aw // tn) + ["silu"] * (aw // tn) + ["sigmoid"] * (2 * d_model // tn)
    pair = 2 * hd
    attn_cols = tuple(n * aw // pair for n in range(4))
    g2 = norm_g.reshape(1, d_model).astype(F32)
    dup = lambda v: jnp.concatenate([v, v], axis=-1)
    abase = jnp.stack([a_re, a_im, jnp.broadcast_to(log_dt[..., None], a_re.shape)], axis=1)
    abase = abase.transpose(2, 0, 1, 3).astype(F32)
    arow = dup(abase)[:, :, :, None, :]
    acol = abase[..., None]
    bt = dup(jnp.stack([b_re, b_im], axis=1).transpose(2, 0, 1, 4, 3).astype(F32))
    cc = jnp.stack([c_re, c_im], axis=1).transpose(2, 0, 1, 3, 4).astype(F32)
    c2 = dup(cc)
    ct = cc.transpose(0, 1, 2, 4, 3)
    qg2 = dup(q_g.astype(F32)).reshape(1, pair)
    kg2 = dup(k_g.astype(F32)).reshape(1, pair)
    dcol = d_skip.astype(F32).reshape(sw, 1)
    bgcol = b_glu.astype(F32).reshape(sw, 1)
    wgt = w_glu.T.astype(BF16)
    wbs = w_bs.astype(BF16)
    wba = w_ba.astype(BF16)
    wo = w_out.astype(BF16)

    w_t, f_t, e_t, dec = _ssm_ops(arow, acol, bt, c2, ct, groups=groups, h_dim=h_dim, p_dim=p_dim)
    bias = _attn_bias(rel_bias.astype(F32))

    projts, projs, u3s, ncs = [], [], [], []
    for x in xs:
        b, l, _ = x.shape
        t = b * l
        assert l % (GRID_W * ROW_UNROLL) == 0 and l % (WIN_ROWS * GRID_W) == 0
        assert (l // lc) & (l // lc - 1) == 0 and l // lc <= 2 ** SCAN_LEVELS
        projt, proj = _inproj(x.reshape(t, d_model), g2, wt, wr, acts_t=acts_t, acts_r=acts_r,
                              tm=_tile(t, 1024), tn=tn)
        projts.append(projt)
        projs.append(proj)
        u3s.append(projt[:sw].reshape(groups, h_dim, t // LANES, LANES).transpose(0, 2, 1, 3)
                   .reshape(groups, t // LANES, h_dim * LANES))
        ncs.append(l // lc)

    y4s = _ssm_apply(u3s, w_t.reshape(groups, h_dim * lc, h_dim * lc), f_t.reshape(groups, h_dim * lc, 2 * LANES),
                     e_t, dec, n_chunks=ncs, h_dim=h_dim)

    outs = []
    for x, projt, proj, y4 in zip(xs, projts, projs, y4s):
        b, l, _ = x.shape
        t = b * l
        yt = y4.reshape(groups, t // LANES, h_dim, LANES).transpose(0, 2, 1, 3).reshape(sw, t)
        oa = _attention(proj.reshape(b, l, -1), bias, qg2, kg2, cols=attn_cols, aw=aw, hd=hd).reshape(t, aw)
        out = _output(yt, projt, oa, proj, x.reshape(t, d_model), dcol, bgcol, wgt, wbs, wba, wo,
                      sw=sw, aw=aw, tm=_tile(t, 256))
        outs.append(out.reshape(b, l, d_model))
    return outs


def kernel(x_prompt, x_sample, norm_g, w_in, ssm_a_re, ssm_a_im, ssm_log_dt, ssm_b_re, ssm_b_im, ssm_c_re, ssm_c_im, ssm_d, w_glu, b_glu, q_norm_g, k_norm_g, rel_bias, w_branch_ssm, w_branch_attn, w_out):
    xs = [x_prompt, x_sample]
    for layer in range(norm_g.shape[0]):
        xs = _layer(xs, norm_g[layer], w_in[layer], ssm_a_re[layer], ssm_a_im[layer], ssm_log_dt[layer],
                    ssm_b_re[layer], ssm_b_im[layer], ssm_c_re[layer], ssm_c_im[layer], ssm_d[layer],
                    w_glu[layer], b_glu[layer], q_norm_g[layer], k_norm_g[layer], rel_bias[layer],
                    w_branch_ssm[layer], w_branch_attn[layer], w_out[layer])
    return (xs[0], xs[1])
```

```python
import functools
import math

import numpy as np
import jax
import jax.numpy as jnp
from jax import lax
from jax.experimental import pallas as pl
from jax.experimental.pallas import tpu as pltpu

NORM_EPS = 1e-6
MASK_VALUE = -1e30
LOG2E = math.log2(math.e)
GRID_W = 64
WIN_ROWS = 8
WIN_COLS = 16
CHUNK = 64
LANES = 128
ROW_UNROLL = 8
SCAN_LEVELS = 7
VMEM_LIMIT_BYTES = 56 * 1024 * 1024

F32 = jnp.float32
BF16 = jnp.bfloat16
HIGHEST = lax.Precision.HIGHEST


def _params(*sem):
    return pltpu.CompilerParams(dimension_semantics=sem, vmem_limit_bytes=VMEM_LIMIT_BYTES)


def _sigmoid(x):
    return jax.nn.sigmoid(x)


def _activate(r, act):
    if act == "silu":
        return r * _sigmoid(r)
    if act == "sigmoid":
        return _sigmoid(r)
    return r


def _inproj_kernel(x_ref, g_ref, wt_ref, w_ref, outt_ref, out_ref, h_ref, *, acts_t, acts_r):
    j = pl.program_id(1)
    n_t = len(acts_t)

    @pl.when(j == 0)
    def _():
        x = x_ref[...]
        ms = jnp.mean(x * x, axis=-1, keepdims=True)
        h_ref[...] = (x * lax.rsqrt(ms + NORM_EPS) * g_ref[...]).astype(BF16)

    def tiles_with(acts, act, base):
        cond = None
        for idx, a in enumerate(acts):
            if a == act:
                c = j == base + idx
                cond = c if cond is None else cond | c
        return cond

    for act in ("none", "silu", "sigmoid"):
        cond = tiles_with(acts_t, act, 0)
        if cond is not None:
            @pl.when(cond)
            def _(act=act):
                r = lax.dot_general(wt_ref[...], h_ref[...], (((1,), (1,)), ((), ())), preferred_element_type=F32)
                outt_ref[...] = _activate(r, act).astype(BF16)
        cond = tiles_with(acts_r, act, n_t)
        if cond is not None:
            @pl.when(cond)
            def _(act=act):
                r = jnp.dot(h_ref[...], w_ref[...], preferred_element_type=F32)
                out_ref[...] = _activate(r, act).astype(BF16)


def _inproj(x, g, wt, w, *, acts_t, acts_r, tm, tn):
    t, d = x.shape
    n_t, n_r = len(acts_t), len(acts_r)
    assert wt.shape == (n_t * tn, d) and w.shape == (d, n_r * tn)
    kern = functools.partial(_inproj_kernel, acts_t=tuple(acts_t), acts_r=tuple(acts_r))
    return pl.pallas_call(
        kern,
        grid=(t // tm, n_t + n_r),
        in_specs=[
            pl.BlockSpec((tm, d), lambda i, j: (i, 0)),
            pl.BlockSpec((1, d), lambda i, j: (0, 0)),
            pl.BlockSpec((tn, d), lambda i, j: (jnp.minimum(j, n_t - 1), 0)),
            pl.BlockSpec((d, tn), lambda i, j: (0, jnp.maximum(j - n_t, 0))),
        ],
        out_specs=[
            pl.BlockSpec((tn, tm), lambda i, j: (jnp.minimum(j, n_t - 1), i)),
            pl.BlockSpec((tm, tn), lambda i, j: (i, jnp.maximum(j - n_t, 0))),
        ],
        out_shape=[jax.ShapeDtypeStruct((n_t * tn, t), BF16), jax.ShapeDtypeStruct((t, n_r * tn), BF16)],
        scratch_shapes=[pltpu.VMEM((tm, d), BF16)],
        compiler_params=_params("arbitrary", "arbitrary"),
        name="inproj",
    )(x, g, wt, w)


def _cmul(ar, ai, br, bi):
    return ar * br - ai * bi, ar * bi + ai * br


def _hi_lo(x):
    if x.dtype == BF16:
        return x, None
    hi = x.astype(BF16)
    return hi, (x - hi.astype(F32)).astype(BF16)


def _dot_hi_lo(a, b):
    ah, al = _hi_lo(a)
    bh, bl = _hi_lo(b)
    acc = jnp.dot(ah, bh, preferred_element_type=F32)
    if bl is not None:
        acc = acc + jnp.dot(ah, bl, preferred_element_type=F32)
    if al is not None:
        acc = acc + jnp.dot(al, bh, preferred_element_type=F32)
    return acc


def _discretize(a_re, a_im, log_dt):
    lam_re = jnp.minimum(a_re, -1e-4)
    lam_im = a_im
    dt = jnp.exp(log_dt)
    mag = jnp.exp(lam_re * dt)
    abar_re = mag * jnp.cos(lam_im * dt)
    abar_im = mag * jnp.sin(lam_im * dt)
    den = lam_re * lam_re + lam_im * lam_im
    f_re = ((abar_re - 1.0) * lam_re + abar_im * lam_im) / den
    f_im = (abar_im * lam_re - (abar_re - 1.0) * lam_im) / den
    return abar_re, abar_im, f_re, f_im


def _squarings(ar, ai, n):
    out = [(ar, ai)]
    for _ in range(n - 1):
        ar, ai = _cmul(ar, ai, ar, ai)
        out.append((ar, ai))
    return out


def _power_table(sq, exps, shape):
    pr = jnp.ones(shape, F32)
    pi = jnp.zeros(shape, F32)
    for k, (ar, ai) in enumerate(sq):
        bit = ((exps >> k) & 1) == 1
        fr = jnp.where(bit, ar, 1.0)
        fi = jnp.where(bit, ai, 0.0)
        pr, pi = _cmul(pr, pi, fr, fi)
    return pr, pi


def _ssm_ops_kernel(arow_ref, acol_ref, bt_ref, c_ref, ct_ref, w_ref, f_ref, e_ref, dec_ref, k_ref, *, h_dim, p_dim):
    lc = CHUNK
    nbits = lc.bit_length()
    lane = lax.broadcasted_iota(jnp.int32, (1, LANES), 1)
    left = lane < lc
    tloc = lane & (lc - 1)
    sub = lax.broadcasted_iota(jnp.int32, (lc, 1), 0)

    q2 = []
    rmat = []
    bb_rows = []
    sq_rows = []
    for d in range(2):
        abr, abi, fr, fi = _discretize(arow_ref[d, 0], arow_ref[d, 1], arow_ref[d, 2])
        bbr, bbi = _cmul(fr, fi, bt_ref[d, 0], bt_ref[d, 1])
        bb_rows.append((bbr, bbi))
        sq_row = _squarings(abr, abi, nbits + SCAN_LEVELS)
        sq_rows.append(sq_row)
        cr, ci = c_ref[d, 0], c_ref[d, 1]
        blocks = []
        for h in range(h_dim):
            qr, qi = _cmul(cr, ci, bbr[h:h + 1, :], bbi[h:h + 1, :])
            blocks.append(jnp.where(left, qr, -qi))
        q2.append(jnp.concatenate(blocks, axis=0))
        cabr, cabi, _, _ = _discretize(acol_ref[d, 0], acol_ref[d, 1], acol_ref[d, 2])
        sq_col = _squarings(cabr, cabi, nbits)
        lag = lane - lc
        if d == 0:
            exps, live = jnp.maximum(lag, 0), lag >= 0
        else:
            exps, live = jnp.maximum(-lag, 0), lag <= 0
        vr, vi = _power_table(sq_col, exps, (p_dim, LANES))
        rmat.append(jnp.concatenate([jnp.where(live, vr, 0.0), jnp.where(live, vi, 0.0)], axis=0))
        eexp = tloc + 1 if d == 0 else lc - tloc
        er, ei = _power_table(sq_col, eexp, (p_dim, LANES))
        n_rep = (h_dim * lc) // LANES
        er = jnp.concatenate([er] * n_rep, axis=1)
        ei = jnp.concatenate([ei] * n_rep, axis=1)
        rep_r = lax.broadcasted_iota(jnp.int32, (h_dim, h_dim * lc), 0)
        rep_c = lax.broadcasted_iota(jnp.int32, (h_dim, h_dim * lc), 1)
        rep = (_slot_channel(rep_c >> (lc.bit_length() - 1), h_dim) == rep_r).astype(BF16)
        cer = _dot_hi_lo(ct_ref[d, 0], rep)
        cei = _dot_hi_lo(ct_ref[d, 1], rep)
        gr, gi = _cmul(cer, cei, er, ei)
        e_ref[2 * d * p_dim:(2 * d + 1) * p_dim, :] = gr.astype(BF16)
        e_ref[(2 * d + 1) * p_dim:(2 * d + 2) * p_dim, :] = (-gi).astype(BF16)
        for k in range(SCAN_LEVELS):
            dr_, di_ = sq_row[nbits - 1 + k]
            dec_ref[(d * SCAN_LEVELS + k) * 2:(d * SCAN_LEVELS + k) * 2 + 1, :] = dr_
            dec_ref[(d * SCAN_LEVELS + k) * 2 + 1:(d * SCAN_LEVELS + k) * 2 + 2, :] = jnp.where(left, -di_, di_)

    k_ref[...] = _dot_hi_lo(q2[0], rmat[0]) + _dot_hi_lo(q2[1], rmat[1])

    fexp = (lc - 1 - sub, sub)
    vrow = [_power_table(sq_rows[d][:nbits], fexp[d], (lc, LANES)) for d in range(2)]

    hh = h_dim // 2

    def per_h(h, carry):
        slot = 2 * (h % hh) + h // hh
        for d in range(2):
            bbr, bbi = bb_rows[d]
            sel = (lax.broadcasted_iota(jnp.int32, (h_dim, 1), 0) == h).astype(F32)
            br = jnp.sum(bbr * sel, axis=0, keepdims=True)
            bi = jnp.sum(bbi * sel, axis=0, keepdims=True)
            zr, zi = _cmul(vrow[d][0], vrow[d][1], br, bi)
            f_ref[slot, :, d * LANES:(d + 1) * LANES] = jnp.where(left, zr, zi).astype(BF16)
        for m in range(hh):
            ra = k_ref[pl.ds(h * h_dim + m, 1), :]
            rb = k_ref[pl.ds(h * h_dim + m + hh, 1), :]
            ta = pltpu.roll(jnp.broadcast_to(ra, (lc, LANES)), lc, 1, stride=1, stride_axis=0)
            tb = pltpu.roll(jnp.broadcast_to(rb, (lc, LANES)), 0, 1, stride=1, stride_axis=0)
            w_ref[slot, :, m * LANES:(m + 1) * LANES] = jnp.where(left, ta, tb).astype(BF16)
        return carry

    lax.fori_loop(0, h_dim, per_h, 0)


def _ssm_ops(arow, acol, bt, c2, ct, *, groups, h_dim, p_dim):
    lc = CHUNK
    kern = functools.partial(_ssm_ops_kernel, h_dim=h_dim, p_dim=p_dim)
    blk = lambda shape: pl.BlockSpec((None,) + shape, lambda g: (g,) + (0,) * len(shape))
    return pl.pallas_call(
        kern,
        grid=(groups,),
        in_specs=[blk((2, 3, 1, LANES)), blk((2, 3, p_dim, 1)), blk((2, 2, h_dim, LANES)),
                  blk((2, 2, h_dim, LANES)), blk((2, 2, p_dim, h_dim))],
        out_specs=[blk((h_dim, lc, h_dim * lc)), blk((h_dim, lc, 2 * LANES)),
                   blk((4 * p_dim, h_dim * lc)), blk((4 * SCAN_LEVELS, LANES))],
        out_shape=[jax.ShapeDtypeStruct((groups, h_dim, lc, h_dim * lc), BF16),
                   jax.ShapeDtypeStruct((groups, h_dim, lc, 2 * LANES), BF16),
                   jax.ShapeDtypeStruct((groups, 4 * p_dim, h_dim * lc), BF16),
                   jax.ShapeDtypeStruct((groups, 4 * SCAN_LEVELS, LANES), F32)],
        scratch_shapes=[pltpu.VMEM((h_dim * h_dim, LANES), F32)],
        compiler_params=_params("arbitrary"),
        name="ssm_ops",
    )(arow, acol, bt, c2, ct)


def _decay_mul(x, dec_ref, d, level):
    row = (d * SCAN_LEVELS + level) * 2
    return x * dec_ref[row:row + 1, :] + pltpu.roll(x, CHUNK, 1) * dec_ref[row + 1:row + 2, :]


def _pair_scan(z, dec_ref, d, n_pairs):
    m = z.shape[0]
    cidx = lax.broadcasted_iota(jnp.int32, (m, 1), 0) & (n_pairs - 1)

    def shifted(x, sh):
        if d == 0:
            return jnp.where(cidx >= sh, pltpu.roll(x, sh, 0), 0.0)
        return jnp.where(cidx < n_pairs - sh, pltpu.roll(x, m - sh, 0), 0.0)

    x = z
    for k in range(n_pairs.bit_length() - 1):
        x = x + _decay_mul(shifted(x, 1 << k), dec_ref, d, k + 1)
    return shifted(x, 1)


def _lane_halves(a, b, left):
    ax = a.ndim - 1
    return jnp.where(left, a, pltpu.roll(b, CHUNK, ax)), jnp.where(left, pltpu.roll(a, CHUNK, ax), b)


def _ssm_apply_kernel(*refs, n_seq, n_chunks):
    u_refs = refs[:n_seq]
    w_ref, f_ref, e_ref, dec_ref = refs[n_seq:n_seq + 4]
    y_refs = refs[n_seq + 4:]
    for u_ref, y_ref, nc in zip(u_refs, y_refs, n_chunks):
        half = u_ref.shape[1]
        u = u_ref[...].reshape(2 * half, u_ref.shape[2])
        z = jnp.dot(u, f_ref[...], preferred_element_type=F32)
        ze, zo = z[:half], z[half:]
        pf = _pair_scan(_decay_mul(ze[:, :LANES], dec_ref, 0, 0) + zo[:, :LANES], dec_ref, 0, nc // 2)
        sf = jnp.concatenate([pf, _decay_mul(pf, dec_ref, 0, 0) + ze[:, :LANES]], axis=0)
        pb = _pair_scan(ze[:, LANES:] + _decay_mul(zo[:, LANES:], dec_ref, 1, 0), dec_ref, 1, nc // 2)
        sb = jnp.concatenate([_decay_mul(pb, dec_ref, 1, 0) + zo[:, LANES:], pb], axis=0)
        s = jnp.concatenate([sf, sb], axis=1)
        acc = jnp.dot(u, w_ref[...], preferred_element_type=F32)
        acc = acc + jnp.dot(s.astype(BF16), e_ref[...], preferred_element_type=F32)
        y_ref[...] = acc.astype(BF16).reshape(y_ref.shape)


def _slot_channel(slot, h_dim):
    return (slot >> 1) + (h_dim // 2) * (slot & 1)


def _to_chunks_kernel(x_ref, o_ref, *, groups, h_dim):
    tk = o_ref.shape[2]
    hh = h_dim // 2
    left = lax.broadcasted_iota(jnp.int32, (1, 1, LANES), 2) < CHUNK
    x = x_ref[...].astype(F32).reshape(groups, 2, hh, tk * LANES)
    ev, od = [], []
    for k in range(tk):
        e, o = _lane_halves(x[:, 0, :, k * LANES:(k + 1) * LANES], x[:, 1, :, k * LANES:(k + 1) * LANES], left)
        ev.append(e)
        od.append(o)
    for p, parts in enumerate((ev, od)):
        o_ref[:, p] = jnp.stack(parts, axis=1).reshape(groups, tk, hh * LANES).astype(o_ref.dtype)


def _from_chunks_kernel(x_ref, o_ref, *, groups, h_dim):
    tk = x_ref.shape[2]
    hh = h_dim // 2
    left = lax.broadcasted_iota(jnp.int32, (1, 1, LANES), 2) < CHUNK
    ye = x_ref[:, 0].astype(F32).reshape(groups, tk, hh, LANES)
    yo = x_ref[:, 1].astype(F32).reshape(groups, tk, hh, LANES)
    lo, hi = zip(*[_lane_halves(ye[:, k], yo[:, k], left) for k in range(tk)])
    y = jnp.stack([jnp.concatenate(lo, axis=-1), jnp.concatenate(hi, axis=-1)], axis=1)
    o_ref[...] = y.reshape(groups * h_dim, tk * LANES).astype(o_ref.dtype)


def _to_chunks(xt, *, groups, h_dim, tk):
    t = xt.shape[1]
    return pl.pallas_call(
        functools.partial(_to_chunks_kernel, groups=groups, h_dim=h_dim),
        grid=(t // (tk * LANES),),
        in_specs=[pl.BlockSpec((groups * h_dim, tk * LANES), lambda i: (0, i))],
        out_specs=pl.BlockSpec((groups, 2, tk, h_dim * CHUNK), lambda i: (0, 0, i, 0)),
        out_shape=jax.ShapeDtypeStruct((groups, 2, t // LANES, h_dim * CHUNK), BF16),
        compiler_params=_params("arbitrary"),
        name="to_chunks",
    )(xt)


def _from_chunks(y4, *, groups, h_dim, tk):
    t = y4.shape[2] * LANES
    return pl.pallas_call(
        functools.partial(_from_chunks_kernel, groups=groups, h_dim=h_dim),
        grid=(t // (tk * LANES),),
        in_specs=[pl.BlockSpec((groups, 2, tk, h_dim * CHUNK), lambda i: (0, 0, i, 0))],
        out_specs=pl.BlockSpec((groups * h_dim, tk * LANES), lambda i: (0, i)),
        out_shape=jax.ShapeDtypeStruct((groups * h_dim, t), BF16),
        compiler_params=_params("arbitrary"),
        name="from_chunks",
    )(y4)


def _ssm_apply(u4s, w, f, e, dec, *, n_chunks):
    groups = w.shape[0]
    blk = lambda shape: pl.BlockSpec((None,) + shape, lambda g: (g,) + (0,) * len(shape))
    kern = functools.partial(_ssm_apply_kernel, n_seq=len(u4s), n_chunks=tuple(n_chunks))
    return pl.pallas_call(
        kern,
        grid=(groups,),
        in_specs=[blk(u.shape[1:]) for u in u4s]
        + [blk(w.shape[1:]), blk(f.shape[1:]), blk(e.shape[1:]), blk(dec.shape[1:])],
        out_specs=[blk(u.shape[1:]) for u in u4s],
        out_shape=[jax.ShapeDtypeStruct(u.shape, BF16) for u in u4s],
        compiler_params=_params("arbitrary"),
        name="ssm_apply",
    )(*u4s, w, f, e, dec)


def _attn_bias_kernel(rb_ref, out_ref, pair_ref, *, hd):
    hp = pl.program_id(0)
    gw = GRID_W
    cq = lax.broadcasted_iota(jnp.int32, (gw, LANES), 0)
    lane = lax.broadcasted_iota(jnp.int32, (gw, LANES), 1)
    left = lane < gw
    ck = lane & (gw - 1)
    cstart = jnp.clip(cq - WIN_COLS // 2, 0, gw - WIN_COLS)
    valid = (ck >= cstart) & (ck < cstart + WIN_COLS)
    dc = ck - cq + (WIN_COLS - 1)
    n_dr = 2 * WIN_ROWS - 1
    for head in range(2):
        hidx = hp * 2 + head
        for dr in range(n_dr - 1):
            acc = jnp.zeros((gw, LANES), F32)
            for i in range(2 * WIN_COLS - 1):
                coef = jnp.where(left, rb_ref[hidx, dr, i], rb_ref[hidx, dr + 1, i])
                acc = jnp.where(dc == i, coef, acc)
            pair_ref[head, dr] = jnp.where(valid, acc * LOG2E, MASK_VALUE)
    for dr0 in range(WIN_ROWS):
        for head in range(2):
            for wq in range(WIN_ROWS // 2):
                out_ref[dr0, head * gw:(head + 1) * gw, wq * LANES:(wq + 1) * LANES] = pair_ref[head, dr0 + 2 * wq]


def _attn_bias(rel_bias):
    nh = rel_bias.shape[0]
    gw = GRID_W
    return pl.pallas_call(
        functools.partial(_attn_bias_kernel, hd=gw),
        grid=(nh // 2,),
        in_specs=[pl.BlockSpec(memory_space=pltpu.SMEM)],
        out_specs=pl.BlockSpec((None, WIN_ROWS, 2 * gw, WIN_ROWS * gw), lambda h: (h, 0, 0, 0)),
        out_shape=jax.ShapeDtypeStruct((nh // 2, WIN_ROWS, 2 * gw, WIN_ROWS * gw), F32),
        scratch_shapes=[pltpu.VMEM((2, 2 * WIN_ROWS - 2, gw, LANES), F32)],
        compiler_params=_params("arbitrary"),
        name="attn_bias",
    )(rel_bias)


def _attn_kernel(q_ref, k_ref, v_ref, za_ref, bias_ref, qg_ref, kg_ref, o_ref, qn_ref, kn_ref, *, rows, hd):
    gw = GRID_W
    win = WIN_ROWS * gw
    blk = ROW_UNROLL * gw
    n_blk = rows // ROW_UNROLL
    lane = lax.broadcasted_iota(jnp.int32, (1, 2 * hd), 1)
    left = lane < hd

    def norm_block(src_ref, dst_ref, g, src_b, dst_b, scale):
        x = src_ref[pl.ds(pl.multiple_of(src_b * blk, blk), blk), :].astype(F32)
        x2 = x * x
        sa = jnp.sum(jnp.where(left, x2, 0.0), axis=-1, keepdims=True)
        sb = jnp.sum(jnp.where(left, 0.0, x2), axis=-1, keepdims=True)
        ms = jnp.where(left, sa, sb) * (1.0 / hd)
        y = x * lax.rsqrt(ms + NORM_EPS) * g
        dst_ref[pl.ds(pl.multiple_of(dst_b * blk, blk), blk), :] = (y * scale).astype(BF16)

    qscale = hd ** -0.5 * LOG2E
    norm_block(q_ref, qn_ref, qg_ref[...], 0, 0, qscale)
    norm_block(k_ref, kn_ref, kg_ref[...], 0, 0, 1.0)
    norm_block(k_ref, kn_ref, kg_ref[...], min(1, n_blk - 1), 1, 1.0)

    def rows_body(gi, carry):
        slices, scores, probs = [], [], []
        for i in range(ROW_UNROLL):
            r = gi * ROW_UNROLL + i
            rstart = jnp.clip(r - WIN_ROWS // 2, 0, rows - WIN_ROWS)
            dr0 = rstart - r + (WIN_ROWS - 1)
            qs = pl.ds(pl.multiple_of(r * gw, gw), gw)
            ks = pl.ds(pl.multiple_of(rstart * gw, gw), win)
            slices.append((qs, ks))
            q = qn_ref[qs, :]
            zero = jnp.zeros_like(q)
            q2 = jnp.concatenate([jnp.where(left, q, zero), jnp.where(left, zero, q)], axis=0)
            s = lax.dot_general(q2, kn_ref[ks, :], (((1,), (1,)), ((), ())), preferred_element_type=F32)
            scores.append(s + bias_ref[dr0])
        for s in scores:
            m = jnp.max(s, axis=-1, keepdims=True)
            p = jnp.exp2(s - m)
            probs.append((p.astype(BF16), jnp.sum(p, axis=-1, keepdims=True)))
        for (qs, ks), (p, l) in zip(slices, probs):
            o2 = jnp.dot(p, v_ref[ks, :], preferred_element_type=F32) / l
            o = jnp.where(left, o2[:gw], o2[gw:]) * za_ref[qs, :].astype(F32)
            o_ref[qs, :] = o.astype(BF16)
        norm_block(q_ref, qn_ref, qg_ref[...], jnp.minimum(gi + 1, n_blk - 1), gi + 1, qscale)
        norm_block(k_ref, kn_ref, kg_ref[...], jnp.minimum(gi + 2, n_blk - 1), gi + 2, 1.0)
        return carry

    lax.fori_loop(0, n_blk, rows_body, 0)


def _attention(proj3, bias, qg2, kg2, *, cols, aw, hd):
    b, l, _ = proj3.shape
    rows = l // GRID_W
    nhp = aw // (2 * hd)
    assert ROW_UNROLL >= WIN_ROWS // 2 and rows % ROW_UNROLL == 0
    col = lambda off: pl.BlockSpec((None, l, 2 * hd), lambda h, i, off=off: (i, 0, off + h))
    kern = functools.partial(_attn_kernel, rows=rows, hd=hd)
    blk = ROW_UNROLL * GRID_W
    return pl.pallas_call(
        kern,
        grid=(nhp, b),
        in_specs=[col(cols[0]), col(cols[1]), col(cols[2]), col(cols[3]),
                  pl.BlockSpec((None,) + bias.shape[1:], lambda h, i: (h, 0, 0, 0)),
                  pl.BlockSpec((1, 2 * hd), lambda h, i: (0, 0)),
                  pl.BlockSpec((1, 2 * hd), lambda h, i: (0, 0))],
        out_specs=pl.BlockSpec((None, l, 2 * hd), lambda h, i: (i, 0, h)),
        out_shape=jax.ShapeDtypeStruct((b, l, aw), BF16),
        scratch_shapes=[pltpu.VMEM((l + blk, 2 * hd), BF16), pltpu.VMEM((l + 2 * blk, 2 * hd), BF16)],
        compiler_params=_params("arbitrary", "arbitrary"),
        name="attention",
    )(proj3, proj3, proj3, proj3, bias, qg2, kg2)


def _gelu_tanh(x):
    c = math.sqrt(2.0 / math.pi)
    return 0.5 * x * (1.0 + jnp.tanh(c * (x + 0.044715 * (x * x * x))))


def _merge_kernel(yt_ref, ut_ref, zt_ref, oa_ref, gs_ref, ga_ref, d_ref, bg_ref, wg_ref, wbs_ref, wba_ref, m_ref):
    y = yt_ref[...].astype(F32) + d_ref[...] * ut_ref[...].astype(F32)
    a = _gelu_tanh(y)
    gate = jnp.dot(wg_ref[...], a.astype(BF16), preferred_element_type=F32) + bg_ref[...]
    ost = (a * _sigmoid(gate) * zt_ref[...].astype(F32)).astype(BF16)
    ms = lax.dot_general(ost, wbs_ref[...], (((0,), (0,)), ((), ())), preferred_element_type=F32)
    ma = jnp.dot(oa_ref[...], wba_ref[...], preferred_element_type=F32)
    m_ref[...] = (gs_ref[...].astype(F32) * ms + ga_ref[...].astype(F32) * ma).astype(BF16)


def _outproj_kernel(m_ref, x_ref, wo_ref, out_ref):
    out_ref[...] = x_ref[...] + jnp.dot(m_ref[...], wo_ref[...], preferred_element_type=F32)


def _const_spec(shape):
    return pl.BlockSpec(shape, lambda i: (0,) * len(shape), pipeline_mode=pl.Buffered(1))


def _merge(yt, projt, oa, proj, dcol, bgcol, wgt, wbs, wba, *, sw, aw, tm):
    t = oa.shape[0]
    d = wbs.shape[1]
    assert (4 * aw) % d == 0
    n_gs = (4 * aw) // d
    return pl.pallas_call(
        _merge_kernel,
        grid=(t // tm,),
        in_specs=[
            pl.BlockSpec((sw, tm), lambda i: (0, i)),
            pl.BlockSpec((sw, tm), lambda i: (0, i)),
            pl.BlockSpec((sw, tm), lambda i: (1, i)),
            pl.BlockSpec((tm, aw), lambda i: (i, 0)),
            pl.BlockSpec((tm, d), lambda i: (i, n_gs)),
            pl.BlockSpec((tm, d), lambda i: (i, n_gs + 1)),
            _const_spec((sw, 1)), _const_spec((sw, 1)),
            _const_spec(wgt.shape), _const_spec(wbs.shape), _const_spec(wba.shape),
        ],
        out_specs=pl.BlockSpec((tm, d), lambda i: (i, 0)),
        out_shape=jax.ShapeDtypeStruct((t, d), BF16),
        compiler_params=_params("arbitrary"),
        name="merge",
    )(yt, projt, projt, oa, proj, proj, dcol, bgcol, wgt, wbs, wba)


def _outproj(m, x, wo, *, tm):
    t, d = x.shape
    return pl.pallas_call(
        _outproj_kernel,
        grid=(t // tm,),
        in_specs=[pl.BlockSpec((tm, d), lambda i: (i, 0)), pl.BlockSpec((tm, d), lambda i: (i, 0)),
                  _const_spec(wo.shape)],
        out_specs=pl.BlockSpec((tm, d), lambda i: (i, 0)),
        out_shape=jax.ShapeDtypeStruct((t, d), x.dtype),
        compiler_params=_params("arbitrary"),
        name="outproj",
    )(m, x, wo)


def _tile(n, pref):
    t = min(n, pref)
    while n % t:
        t //= 2
    return t


def _layer(xs, norm_g, w_in, a_re, a_im, log_dt, b_re, b_im, c_re, c_im, d_skip, w_glu, b_glu,
           q_g, k_g, rel_bias, w_bs, w_ba, w_out):
    d_model = w_in.shape[0]
    sw = d_skip.shape[0]
    groups, p_dim = a_re.shape[1], a_re.shape[2]
    h_dim = sw // groups
    hd = q_g.shape[0]
    aw = w_ba.shape[0]
    lc = CHUNK
    assert h_dim * lc % LANES == 0 and lc * 2 == LANES and h_dim % 2 == 0

    wt = w_in[:, :2 * sw].T.astype(BF16)
    wr = w_in[:, 2 * sw:].astype(BF16)
    tn = _tile(math.gcd(sw, aw), 1024)
    acts_t = ["none"] * (sw // tn) + ["silu"] * (sw // tn)
    acts_r = ["none"] * (3 * aw // tn) + ["silu"] * (aw // tn) + ["sigmoid"] * (2 * d_model // tn)
    pair = 2 * hd
    attn_cols = tuple(n * aw // pair for n in range(4))
    g2 = norm_g.reshape(1, d_model).astype(F32)
    dup = lambda v: jnp.concatenate([v, v], axis=-1)
    abase = jnp.stack([a_re, a_im, jnp.broadcast_to(log_dt[..., None], a_re.shape)], axis=1)
    abase = abase.transpose(2, 0, 1, 3).astype(F32)
    arow = dup(abase)[:, :, :, None, :]
    acol = abase[..., None]
    bt = dup(jnp.stack([b_re, b_im], axis=1).transpose(2, 0, 1, 4, 3).astype(F32))
    cc = jnp.stack([c_re, c_im], axis=1).transpose(2, 0, 1, 3, 4).astype(F32)
    c2 = dup(cc)
    ct = cc.transpose(0, 1, 2, 4, 3)
    qg2 = dup(q_g.astype(F32)).reshape(1, pair)
    kg2 = dup(k_g.astype(F32)).reshape(1, pair)
    dcol = d_skip.astype(F32).reshape(sw, 1)
    bgcol = b_glu.astype(F32).reshape(sw, 1)
    wgt = w_glu.T.astype(BF16)
    wbs = w_bs.astype(BF16)
    wba = w_ba.astype(BF16)
    wo = w_out.astype(BF16)

    w_t, f_t, e_t, dec = _ssm_ops(arow, acol, bt, c2, ct, groups=groups, h_dim=h_dim, p_dim=p_dim)
    bias = _attn_bias(rel_bias.astype(F32))

    projts, projs, u3s, ncs = [], [], [], []
    for x in xs:
        b, l, _ = x.shape
        t = b * l
        assert l % (GRID_W * ROW_UNROLL) == 0 and l % (WIN_ROWS * GRID_W) == 0
        assert (l // lc) & (l // lc - 1) == 0 and l // lc <= 2 ** SCAN_LEVELS
        projt, proj = _inproj(x.reshape(t, d_model), g2, wt, wr, acts_t=acts_t, acts_r=acts_r,
                              tm=_tile(t, 1024), tn=tn)
        projts.append(projt)
        projs.append(proj)
        u3s.append(_to_chunks(projt, groups=groups, h_dim=h_dim, tk=_tile(t // LANES, 8)))
        ncs.append(l // lc)

    y4s = _ssm_apply(u3s, w_t.reshape(groups, h_dim * lc, h_dim * lc), f_t.reshape(groups, h_dim * lc, 2 * LANES),
                     e_t, dec, n_chunks=ncs)

    outs = []
    for x, projt, proj, y4 in zip(xs, projts, projs, y4s):
        b, l, _ = x.shape
        t = b * l
        yt = _from_chunks(y4, groups=groups, h_dim=h_dim, tk=_tile(t // LANES, 8))
        oa = _attention(proj.reshape(b, l, -1), bias, qg2, kg2, cols=attn_cols, aw=aw, hd=hd).reshape(t, aw)
        m = _merge(yt, projt, oa, proj, dcol, bgcol, wgt, wbs, wba, sw=sw, aw=aw, tm=_tile(t, 512))
        out = _outproj(m, x.reshape(t, d_model), wo, tm=_tile(t, 512))
        outs.append(out.reshape(b, l, d_model))
    return outs


def kernel(x_prompt, x_sample, norm_g, w_in, ssm_a_re, ssm_a_im, ssm_log_dt, ssm_b_re, ssm_b_im, ssm_c_re, ssm_c_im, ssm_d, w_glu, b_glu, q_norm_g, k_norm_g, rel_bias, w_branch_ssm, w_branch_attn, w_out):
    xs = [x_prompt, x_sample]
    for layer in range(norm_g.shape[0]):
        xs = _layer(xs, norm_g[layer], w_in[layer], ssm_a_re[layer], ssm_a_im[layer], ssm_log_dt[layer],
                    ssm_b_re[layer], ssm_b_im[layer], ssm_c_re[layer], ssm_c_im[layer], ssm_d[layer],
                    w_glu[layer], b_glu[layer], q_norm_g[layer], k_norm_g[layer], rel_bias[layer],
                    w_branch_ssm[layer], w_branch_attn[layer], w_out[layer])
    return (xs[0], xs[1])
```

```python
import functools
import math

import numpy as np
import jax
import jax.numpy as jnp
from jax import lax
from jax.experimental import pallas as pl
from jax.experimental.pallas import tpu as pltpu

NORM_EPS = 1e-6
MASK_VALUE = -1e30
LOG2E = math.log2(math.e)
GRID_W = 64
WIN_ROWS = 8
WIN_COLS = 16
CHUNK = 64
LANES = 128
ROW_UNROLL = 8
MERGE_SPLIT = 2
SCAN_LEVELS = 7
VMEM_LIMIT_BYTES = 56 * 1024 * 1024

F32 = jnp.float32
BF16 = jnp.bfloat16
HIGHEST = lax.Precision.HIGHEST


def _params(*sem):
    return pltpu.CompilerParams(dimension_semantics=sem, vmem_limit_bytes=VMEM_LIMIT_BYTES)


def _sigmoid(x):
    return 0.5 * jnp.tanh(0.5 * x) + 0.5


def _activate(r, act):
    if act == "silu":
        return r * _sigmoid(r)
    if act == "sigmoid":
        return _sigmoid(r)
    return r


def _inproj_kernel(x_ref, g_ref, wt_ref, w_ref, outt_ref, out_ref, h_ref, *, acts_t, acts_r):
    j = pl.program_id(1)
    n_t = len(acts_t)

    @pl.when(j == 0)
    def _():
        x = x_ref[...]
        ms = jnp.mean(x * x, axis=-1, keepdims=True)
        h_ref[...] = (x * lax.rsqrt(ms + NORM_EPS) * g_ref[...]).astype(BF16)

    def tiles_with(acts, act, base):
        cond = None
        for idx, a in enumerate(acts):
            if a == act:
                c = j == base + idx
                cond = c if cond is None else cond | c
        return cond

    for act in ("none", "silu", "sigmoid"):
        cond = tiles_with(acts_t, act, 0)
        if cond is not None:
            @pl.when(cond)
            def _(act=act):
                r = lax.dot_general(wt_ref[...], h_ref[...], (((1,), (1,)), ((), ())), preferred_element_type=F32)
                outt_ref[...] = _activate(r, act).astype(BF16)
        cond = tiles_with(acts_r, act, n_t)
        if cond is not None:
            @pl.when(cond)
            def _(act=act):
                r = jnp.dot(h_ref[...], w_ref[...], preferred_element_type=F32)
                out_ref[...] = _activate(r, act).astype(BF16)


def _inproj(x, g, wt, w, *, acts_t, acts_r, tm, tn):
    t, d = x.shape
    n_t, n_r = len(acts_t), len(acts_r)
    assert wt.shape == (n_t * tn, d) and w.shape == (d, n_r * tn)
    kern = functools.partial(_inproj_kernel, acts_t=tuple(acts_t), acts_r=tuple(acts_r))
    return pl.pallas_call(
        kern,
        grid=(t // tm, n_t + n_r),
        in_specs=[
            pl.BlockSpec((tm, d), lambda i, j: (i, 0)),
            pl.BlockSpec((1, d), lambda i, j: (0, 0)),
            pl.BlockSpec((tn, d), lambda i, j: (jnp.minimum(j, n_t - 1), 0)),
            pl.BlockSpec((d, tn), lambda i, j: (0, jnp.maximum(j - n_t, 0))),
        ],
        out_specs=[
            pl.BlockSpec((tn, tm), lambda i, j: (jnp.minimum(j, n_t - 1), i)),
            pl.BlockSpec((tm, tn), lambda i, j: (i, jnp.maximum(j - n_t, 0))),
        ],
        out_shape=[jax.ShapeDtypeStruct((n_t * tn, t), BF16), jax.ShapeDtypeStruct((t, n_r * tn), BF16)],
        scratch_shapes=[pltpu.VMEM((tm, d), BF16)],
        compiler_params=_params("arbitrary", "arbitrary"),
        name="inproj",
    )(x, g, wt, w)


def _cmul(ar, ai, br, bi):
    return ar * br - ai * bi, ar * bi + ai * br


def _hi_lo(x):
    if x.dtype == BF16:
        return x, None
    hi = x.astype(BF16)
    return hi, (x - hi.astype(F32)).astype(BF16)


def _dot_hi_lo(a, b):
    ah, al = _hi_lo(a)
    bh, bl = _hi_lo(b)
    acc = jnp.dot(ah, bh, preferred_element_type=F32)
    if bl is not None:
        acc = acc + jnp.dot(ah, bl, preferred_element_type=F32)
    if al is not None:
        acc = acc + jnp.dot(al, bh, preferred_element_type=F32)
    return acc


def _discretize(a_re, a_im, log_dt):
    lam_re = jnp.minimum(a_re, -1e-4)
    lam_im = a_im
    dt = jnp.exp(log_dt)
    mag = jnp.exp(lam_re * dt)
    abar_re = mag * jnp.cos(lam_im * dt)
    abar_im = mag * jnp.sin(lam_im * dt)
    den = lam_re * lam_re + lam_im * lam_im
    f_re = ((abar_re - 1.0) * lam_re + abar_im * lam_im) / den
    f_im = (abar_im * lam_re - (abar_re - 1.0) * lam_im) / den
    return abar_re, abar_im, f_re, f_im


def _squarings(ar, ai, n):
    out = [(ar, ai)]
    for _ in range(n - 1):
        ar, ai = _cmul(ar, ai, ar, ai)
        out.append((ar, ai))
    return out


def _power_table(sq, exps, shape):
    pr = jnp.ones(shape, F32)
    pi = jnp.zeros(shape, F32)
    for k, (ar, ai) in enumerate(sq):
        bit = ((exps >> k) & 1) == 1
        fr = jnp.where(bit, ar, 1.0)
        fi = jnp.where(bit, ai, 0.0)
        pr, pi = _cmul(pr, pi, fr, fi)
    return pr, pi


def _ssm_ops_kernel(arow_ref, acol_ref, bt_ref, c_ref, ct_ref, w_ref, f_ref, e_ref, dec_ref, k_ref, *, h_dim, p_dim):
    lc = CHUNK
    nbits = lc.bit_length()
    lane = lax.broadcasted_iota(jnp.int32, (1, LANES), 1)
    left = lane < lc
    tloc = lane & (lc - 1)
    sub = lax.broadcasted_iota(jnp.int32, (lc, 1), 0)

    q2 = []
    rmat = []
    bb_rows = []
    sq_rows = []
    for d in range(2):
        abr, abi, fr, fi = _discretize(arow_ref[d, 0], arow_ref[d, 1], arow_ref[d, 2])
        bbr, bbi = _cmul(fr, fi, bt_ref[d, 0], bt_ref[d, 1])
        bb_rows.append((bbr, bbi))
        sq_row = _squarings(abr, abi, nbits + SCAN_LEVELS)
        sq_rows.append(sq_row)
        cr, ci = c_ref[d, 0], c_ref[d, 1]
        blocks = []
        for h in range(h_dim):
            qr, qi = _cmul(cr, ci, bbr[h:h + 1, :], bbi[h:h + 1, :])
            blocks.append(jnp.where(left, qr, -qi))
        q2.append(jnp.concatenate(blocks, axis=0))
        cabr, cabi, _, _ = _discretize(acol_ref[d, 0], acol_ref[d, 1], acol_ref[d, 2])
        sq_col = _squarings(cabr, cabi, nbits)
        lag = lane - lc
        if d == 0:
            exps, live = jnp.maximum(lag, 0), lag >= 0
        else:
            exps, live = jnp.maximum(-lag, 0), lag <= 0
        vr, vi = _power_table(sq_col, exps, (p_dim, LANES))
        rmat.append(jnp.concatenate([jnp.where(live, vr, 0.0), jnp.where(live, vi, 0.0)], axis=0))
        eexp = tloc + 1 if d == 0 else lc - tloc
        er, ei = _power_table(sq_col, eexp, (p_dim, LANES))
        n_rep = (h_dim * lc) // LANES
        er = jnp.concatenate([er] * n_rep, axis=1)
        ei = jnp.concatenate([ei] * n_rep, axis=1)
        rep_r = lax.broadcasted_iota(jnp.int32, (h_dim, h_dim * lc), 0)
        rep_c = lax.broadcasted_iota(jnp.int32, (h_dim, h_dim * lc), 1)
        rep = (_slot_channel(rep_c >> (lc.bit_length() - 1), h_dim) == rep_r).astype(BF16)
        cer = _dot_hi_lo(ct_ref[d, 0], rep)
        cei = _dot_hi_lo(ct_ref[d, 1], rep)
        gr, gi = _cmul(cer, cei, er, ei)
        e_ref[2 * d * p_dim:(2 * d + 1) * p_dim, :] = gr.astype(BF16)
        e_ref[(2 * d + 1) * p_dim:(2 * d + 2) * p_dim, :] = (-gi).astype(BF16)
        for k in range(SCAN_LEVELS):
            dr_, di_ = sq_row[nbits - 1 + k]
            dec_ref[(d * SCAN_LEVELS + k) * 2:(d * SCAN_LEVELS + k) * 2 + 1, :] = dr_
            dec_ref[(d * SCAN_LEVELS + k) * 2 + 1:(d * SCAN_LEVELS + k) * 2 + 2, :] = jnp.where(left, -di_, di_)

    kbi = _dot_hi_lo(q2[0], rmat[0]) + _dot_hi_lo(q2[1], rmat[1])
    k_ref[0] = kbi
    k_ref[1] = pltpu.roll(kbi, lc, 1)

    fexp = (lc - 1 - sub, sub)
    vrow = [_power_table(sq_rows[d][:nbits], fexp[d], (lc, LANES)) for d in range(2)]

    hh = h_dim // 2
    pre_roll_left = ((lane + sub) & (LANES - 1)) < lc

    def per_h(h, carry):
        slot = 2 * (h % hh) + h // hh
        for d in range(2):
            bbr, bbi = bb_rows[d]
            sel = (lax.broadcasted_iota(jnp.int32, (h_dim, 1), 0) == h).astype(F32)
            br = jnp.sum(bbr * sel, axis=0, keepdims=True)
            bi = jnp.sum(bbi * sel, axis=0, keepdims=True)
            zr, zi = _cmul(vrow[d][0], vrow[d][1], br, bi)
            f_ref[slot, :, d * LANES:(d + 1) * LANES] = jnp.where(left, zr, zi).astype(BF16)
        for m in range(hh):
            ra = k_ref[1, pl.ds(h * h_dim + m, 1), :]
            rb = k_ref[0, pl.ds(h * h_dim + m + hh, 1), :]
            src = jnp.where(pre_roll_left, jnp.broadcast_to(ra, (lc, LANES)), jnp.broadcast_to(rb, (lc, LANES)))
            tile = pltpu.roll(src, 0, 1, stride=1, stride_axis=0)
            w_ref[slot, :, m * LANES:(m + 1) * LANES] = tile.astype(BF16)
        return carry

    lax.fori_loop(0, h_dim, per_h, 0)


def _ssm_ops(arow, acol, bt, c2, ct, *, groups, h_dim, p_dim):
    lc = CHUNK
    kern = functools.partial(_ssm_ops_kernel, h_dim=h_dim, p_dim=p_dim)
    blk = lambda shape: pl.BlockSpec((None,) + shape, lambda g: (g,) + (0,) * len(shape))
    return pl.pallas_call(
        kern,
        grid=(groups,),
        in_specs=[blk((2, 3, 1, LANES)), blk((2, 3, p_dim, 1)), blk((2, 2, h_dim, LANES)),
                  blk((2, 2, h_dim, LANES)), blk((2, 2, p_dim, h_dim))],
        out_specs=[blk((h_dim, lc, h_dim * lc)), blk((h_dim, lc, 2 * LANES)),
                   blk((4 * p_dim, h_dim * lc)), blk((4 * SCAN_LEVELS, LANES))],
        out_shape=[jax.ShapeDtypeStruct((groups, h_dim, lc, h_dim * lc), BF16),
                   jax.ShapeDtypeStruct((groups, h_dim, lc, 2 * LANES), BF16),
                   jax.ShapeDtypeStruct((groups, 4 * p_dim, h_dim * lc), BF16),
                   jax.ShapeDtypeStruct((groups, 4 * SCAN_LEVELS, LANES), F32)],
        scratch_shapes=[pltpu.VMEM((2, h_dim * h_dim, LANES), F32)],
        compiler_params=_params("arbitrary"),
        name="ssm_ops",
    )(arow, acol, bt, c2, ct)


def _decay_mul(x, dec_ref, d, level):
    row = (d * SCAN_LEVELS + level) * 2
    return x * dec_ref[row:row + 1, :] + pltpu.roll(x, CHUNK, 1) * dec_ref[row + 1:row + 2, :]


def _pair_scan(z, dec_ref, d, n_pairs):
    m = z.shape[0]
    cidx = lax.broadcasted_iota(jnp.int32, (m, 1), 0) & (n_pairs - 1)

    def shifted(x, sh):
        if d == 0:
            return jnp.where(cidx >= sh, pltpu.roll(x, sh, 0), 0.0)
        return jnp.where(cidx < n_pairs - sh, pltpu.roll(x, m - sh, 0), 0.0)

    x = z
    for k in range(n_pairs.bit_length() - 1):
        x = x + _decay_mul(shifted(x, 1 << k), dec_ref, d, k + 1)
    return shifted(x, 1)


def _lane_halves(a, b, left):
    ax = a.ndim - 1
    return jnp.where(left, a, pltpu.roll(b, CHUNK, ax)), jnp.where(left, pltpu.roll(a, CHUNK, ax), b)


def _ssm_apply_kernel(*refs, n_seq, n_chunks):
    u_refs = refs[:n_seq]
    w_ref, f_ref, e_ref, dec_ref = refs[n_seq:n_seq + 4]
    y_refs = refs[n_seq + 4:]
    for u_ref, y_ref, nc in zip(u_refs, y_refs, n_chunks):
        half = u_ref.shape[1]
        u = u_ref[...].reshape(2 * half, u_ref.shape[2])
        z = jnp.dot(u, f_ref[...], preferred_element_type=F32)
        ze, zo = z[:half], z[half:]
        pf = _pair_scan(_decay_mul(ze[:, :LANES], dec_ref, 0, 0) + zo[:, :LANES], dec_ref, 0, nc // 2)
        sf = jnp.concatenate([pf, _decay_mul(pf, dec_ref, 0, 0) + ze[:, :LANES]], axis=0)
        pb = _pair_scan(ze[:, LANES:] + _decay_mul(zo[:, LANES:], dec_ref, 1, 0), dec_ref, 1, nc // 2)
        sb = jnp.concatenate([_decay_mul(pb, dec_ref, 1, 0) + zo[:, LANES:], pb], axis=0)
        s = jnp.concatenate([sf, sb], axis=1)
        acc = jnp.dot(u, w_ref[...], preferred_element_type=F32)
        acc = acc + jnp.dot(s.astype(BF16), e_ref[...], preferred_element_type=F32)
        y_ref[...] = acc.astype(BF16).reshape(y_ref.shape)


def _slot_channel(slot, h_dim):
    return (slot >> 1) + (h_dim // 2) * (slot & 1)


def _to_chunks_kernel(x_ref, o_ref, *, groups, h_dim):
    tk = o_ref.shape[2]
    hh = h_dim // 2
    left = lax.broadcasted_iota(jnp.int32, (1, 1, LANES), 2) < CHUNK
    x = x_ref[...].astype(F32).reshape(groups, 2, hh, tk * LANES)
    ev, od = [], []
    for k in range(tk):
        e, o = _lane_halves(x[:, 0, :, k * LANES:(k + 1) * LANES], x[:, 1, :, k * LANES:(k + 1) * LANES], left)
        ev.append(e)
        od.append(o)
    for p, parts in enumerate((ev, od)):
        o_ref[:, p] = jnp.stack(parts, axis=1).reshape(groups, tk, hh * LANES).astype(o_ref.dtype)


def _from_chunks_kernel(x_ref, o_ref, *, groups, h_dim):
    tk = x_ref.shape[2]
    hh = h_dim // 2
    left = lax.broadcasted_iota(jnp.int32, (1, 1, LANES), 2) < CHUNK
    ye = x_ref[:, 0].astype(F32).reshape(groups, tk, hh, LANES)
    yo = x_ref[:, 1].astype(F32).reshape(groups, tk, hh, LANES)
    lo, hi = zip(*[_lane_halves(ye[:, k], yo[:, k], left) for k in range(tk)])
    y = jnp.stack([jnp.concatenate(lo, axis=-1), jnp.concatenate(hi, axis=-1)], axis=1)
    o_ref[...] = y.reshape(groups * h_dim, tk * LANES).astype(o_ref.dtype)


def _to_chunks(xt, *, groups, h_dim, tk):
    t = xt.shape[1]
    return pl.pallas_call(
        functools.partial(_to_chunks_kernel, groups=groups, h_dim=h_dim),
        grid=(t // (tk * LANES),),
        in_specs=[pl.BlockSpec((groups * h_dim, tk * LANES), lambda i: (0, i))],
        out_specs=pl.BlockSpec((groups, 2, tk, h_dim * CHUNK), lambda i: (0, 0, i, 0)),
        out_shape=jax.ShapeDtypeStruct((groups, 2, t // LANES, h_dim * CHUNK), BF16),
        compiler_params=_params("arbitrary"),
        name="to_chunks",
    )(xt)


def _from_chunks(y4, *, groups, h_dim, tk):
    t = y4.shape[2] * LANES
    return pl.pallas_call(
        functools.partial(_from_chunks_kernel, groups=groups, h_dim=h_dim),
        grid=(t // (tk * LANES),),
        in_specs=[pl.BlockSpec((groups, 2, tk, h_dim * CHUNK), lambda i: (0, 0, i, 0))],
        out_specs=pl.BlockSpec((groups * h_dim, tk * LANES), lambda i: (0, i)),
        out_shape=jax.ShapeDtypeStruct((groups * h_dim, t), BF16),
        compiler_params=_params("arbitrary"),
        name="from_chunks",
    )(y4)


def _ssm_apply(u4s, w, f, e, dec, *, n_chunks):
    groups = w.shape[0]
    blk = lambda shape: pl.BlockSpec((None,) + shape, lambda g: (g,) + (0,) * len(shape))
    kern = functools.partial(_ssm_apply_kernel, n_seq=len(u4s), n_chunks=tuple(n_chunks))
    return pl.pallas_call(
        kern,
        grid=(groups,),
        in_specs=[blk(u.shape[1:]) for u in u4s]
        + [blk(w.shape[1:]), blk(f.shape[1:]), blk(e.shape[1:]), blk(dec.shape[1:])],
        out_specs=[blk(u.shape[1:]) for u in u4s],
        out_shape=[jax.ShapeDtypeStruct(u.shape, BF16) for u in u4s],
        compiler_params=_params("arbitrary"),
        name="ssm_apply",
    )(*u4s, w, f, e, dec)


def _attn_bias_kernel(rb_ref, out_ref, pair_ref, *, hd):
    hp = pl.program_id(0)
    gw = GRID_W
    cq = lax.broadcasted_iota(jnp.int32, (gw, LANES), 0)
    lane = lax.broadcasted_iota(jnp.int32, (gw, LANES), 1)
    left = lane < gw
    ck = lane & (gw - 1)
    cstart = jnp.clip(cq - WIN_COLS // 2, 0, gw - WIN_COLS)
    valid = (ck >= cstart) & (ck < cstart + WIN_COLS)
    dc = ck - cq + (WIN_COLS - 1)
    n_dr = 2 * WIN_ROWS - 1
    for head in range(2):
        hidx = hp * 2 + head
        for dr in range(n_dr - 1):
            acc = jnp.zeros((gw, LANES), F32)
            for i in range(2 * WIN_COLS - 1):
                coef = jnp.where(left, rb_ref[hidx, dr, i], rb_ref[hidx, dr + 1, i])
                acc = jnp.where(dc == i, coef, acc)
            pair_ref[head, dr] = jnp.where(valid, acc * LOG2E, MASK_VALUE)
    for dr0 in range(WIN_ROWS):
        for head in range(2):
            for wq in range(WIN_ROWS // 2):
                out_ref[dr0, head * gw:(head + 1) * gw, wq * LANES:(wq + 1) * LANES] = pair_ref[head, dr0 + 2 * wq]


def _attn_bias(rel_bias):
    nh = rel_bias.shape[0]
    gw = GRID_W
    return pl.pallas_call(
        functools.partial(_attn_bias_kernel, hd=gw),
        grid=(nh // 2,),
        in_specs=[pl.BlockSpec(memory_space=pltpu.SMEM)],
        out_specs=pl.BlockSpec((None, WIN_ROWS, 2 * gw, WIN_ROWS * gw), lambda h: (h, 0, 0, 0)),
        out_shape=jax.ShapeDtypeStruct((nh // 2, WIN_ROWS, 2 * gw, WIN_ROWS * gw), F32),
        scratch_shapes=[pltpu.VMEM((2, 2 * WIN_ROWS - 2, gw, LANES), F32)],
        compiler_params=_params("arbitrary"),
        name="attn_bias",
    )(rel_bias)


def _attn_kernel(q_ref, k_ref, v_ref, za_ref, bias_ref, qg_ref, kg_ref, o_ref, qn_ref, kn_ref, *, rows, hd):
    gw = GRID_W
    win = WIN_ROWS * gw
    blk = ROW_UNROLL * gw
    n_blk = rows // ROW_UNROLL
    lane = lax.broadcasted_iota(jnp.int32, (1, 2 * hd), 1)
    left = lane < hd

    def norm_block(src_ref, dst_ref, g, src_b, dst_b, scale):
        x = src_ref[pl.ds(pl.multiple_of(src_b * blk, blk), blk), :].astype(F32)
        x2 = x * x
        sa = jnp.sum(jnp.where(left, x2, 0.0), axis=-1, keepdims=True)
        sb = jnp.sum(jnp.where(left, 0.0, x2), axis=-1, keepdims=True)
        ms = jnp.where(left, sa, sb) * (1.0 / hd)
        y = x * lax.rsqrt(ms + NORM_EPS) * g
        dst_ref[pl.ds(pl.multiple_of(dst_b * blk, blk), blk), :] = (y * scale).astype(BF16)

    qscale = hd ** -0.5 * LOG2E
    norm_block(q_ref, qn_ref, qg_ref[...], 0, 0, qscale)
    norm_block(k_ref, kn_ref, kg_ref[...], 0, 0, 1.0)
    norm_block(k_ref, kn_ref, kg_ref[...], min(1, n_blk - 1), 1, 1.0)

    def rows_body(gi, carry):
        slices, scores, probs = [], [], []
        for i in range(ROW_UNROLL):
            r = gi * ROW_UNROLL + i
            rstart = jnp.clip(r - WIN_ROWS // 2, 0, rows - WIN_ROWS)
            dr0 = rstart - r + (WIN_ROWS - 1)
            qs = pl.ds(pl.multiple_of(r * gw, gw), gw)
            ks = pl.ds(pl.multiple_of(rstart * gw, gw), win)
            slices.append((qs, ks))
            q = qn_ref[qs, :]
            zero = jnp.zeros_like(q)
            q2 = jnp.concatenate([jnp.where(left, q, zero), jnp.where(left, zero, q)], axis=0)
            s = lax.dot_general(q2, kn_ref[ks, :], (((1,), (1,)), ((), ())), preferred_element_type=F32)
            scores.append(s + bias_ref[dr0])
        for s in scores:
            m = jnp.max(s, axis=-1, keepdims=True)
            p = jnp.exp2(s - m)
            probs.append((p.astype(BF16), jnp.sum(p, axis=-1, keepdims=True)))
        for (qs, ks), (p, l) in zip(slices, probs):
            o2 = jnp.dot(p, v_ref[ks, :], preferred_element_type=F32) / l
            o = jnp.where(left, o2[:gw], o2[gw:]) * za_ref[qs, :].astype(F32)
            o_ref[qs, :] = o.astype(BF16)
        norm_block(q_ref, qn_ref, qg_ref[...], jnp.minimum(gi + 1, n_blk - 1), gi + 1, qscale)
        norm_block(k_ref, kn_ref, kg_ref[...], jnp.minimum(gi + 2, n_blk - 1), gi + 2, 1.0)
        return carry

    lax.fori_loop(0, n_blk, rows_body, 0)


def _attention(proj3, bias, qg2, kg2, *, cols, aw, hd):
    b, l, _ = proj3.shape
    rows = l // GRID_W
    nhp = aw // (2 * hd)
    assert ROW_UNROLL >= WIN_ROWS // 2 and rows % ROW_UNROLL == 0
    col = lambda off: pl.BlockSpec((None, l, 2 * hd), lambda h, i, off=off: (i, 0, off + h))
    kern = functools.partial(_attn_kernel, rows=rows, hd=hd)
    blk = ROW_UNROLL * GRID_W
    return pl.pallas_call(
        kern,
        grid=(nhp, b),
        in_specs=[col(cols[0]), col(cols[1]), col(cols[2]), col(cols[3]),
                  pl.BlockSpec((None,) + bias.shape[1:], lambda h, i: (h, 0, 0, 0)),
                  pl.BlockSpec((1, 2 * hd), lambda h, i: (0, 0)),
                  pl.BlockSpec((1, 2 * hd), lambda h, i: (0, 0))],
        out_specs=pl.BlockSpec((None, l, 2 * hd), lambda h, i: (i, 0, h)),
        out_shape=jax.ShapeDtypeStruct((b, l, aw), BF16),
        scratch_shapes=[pltpu.VMEM((l + blk, 2 * hd), BF16), pltpu.VMEM((l + 2 * blk, 2 * hd), BF16)],
        compiler_params=_params("arbitrary", "arbitrary"),
        name="attention",
    )(proj3, proj3, proj3, proj3, bias, qg2, kg2)


def _gelu_tanh(x):
    c = math.sqrt(2.0 / math.pi)
    return 0.5 * x * (1.0 + jnp.tanh(c * (x + 0.044715 * (x * x * x))))


def _merge_kernel(yt_ref, ut_ref, zt_ref, oa_ref, gs_ref, ga_ref, d_ref, bg_ref, wg_ref, wbs_ref, wba_ref, m_ref):
    tm = m_ref.shape[0]
    sub = tm // MERGE_SPLIT
    parts = [slice(i * sub, (i + 1) * sub) for i in range(MERGE_SPLIT)]
    acts = [_gelu_tanh(yt_ref[:, p].astype(F32) + d_ref[...] * ut_ref[:, p].astype(F32)) for p in parts]
    mas = [jnp.dot(oa_ref[p, :], wba_ref[...], preferred_element_type=F32) for p in parts]
    gates = [jnp.dot(wg_ref[...], a.astype(BF16), preferred_element_type=F32) + bg_ref[...] for a in acts]
    osts = [(a * _sigmoid(g) * zt_ref[:, p].astype(F32)).astype(BF16) for a, g, p in zip(acts, gates, parts)]
    mss = [lax.dot_general(o, wbs_ref[...], (((0,), (0,)), ((), ())), preferred_element_type=F32) for o in osts]
    for p, ms, ma in zip(parts, mss, mas):
        m_ref[p, :] = (gs_ref[p, :].astype(F32) * ms + ga_ref[p, :].astype(F32) * ma).astype(BF16)


def _outproj_kernel(m_ref, x_ref, wo_ref, out_ref):
    out_ref[...] = x_ref[...] + jnp.dot(m_ref[...], wo_ref[...], preferred_element_type=F32)


def _const_spec(shape):
    return pl.BlockSpec(shape, lambda i: (0,) * len(shape), pipeline_mode=pl.Buffered(1))


def _merge(yt, projt, oa, proj, dcol, bgcol, wgt, wbs, wba, *, sw, aw, tm):
    t = oa.shape[0]
    d = wbs.shape[1]
    assert (4 * aw) % d == 0
    n_gs = (4 * aw) // d
    return pl.pallas_call(
        _merge_kernel,
        grid=(t // tm,),
        in_specs=[
            pl.BlockSpec((sw, tm), lambda i: (0, i)),
            pl.BlockSpec((sw, tm), lambda i: (0, i)),
            pl.BlockSpec((sw, tm), lambda i: (1, i)),
            pl.BlockSpec((tm, aw), lambda i: (i, 0)),
            pl.BlockSpec((tm, d), lambda i: (i, n_gs)),
            pl.BlockSpec((tm, d), lambda i: (i, n_gs + 1)),
            _const_spec((sw, 1)), _const_spec((sw, 1)),
            _const_spec(wgt.shape), _const_spec(wbs.shape), _const_spec(wba.shape),
        ],
        out_specs=pl.BlockSpec((tm, d), lambda i: (i, 0)),
        out_shape=jax.ShapeDtypeStruct((t, d), BF16),
        compiler_params=_params("arbitrary"),
        name="merge",
    )(yt, projt, projt, oa, proj, proj, dcol, bgcol, wgt, wbs, wba)


def _outproj(m, x, wo, *, tm):
    t, d = x.shape
    return pl.pallas_call(
        _outproj_kernel,
        grid=(t // tm,),
        in_specs=[pl.BlockSpec((tm, d), lambda i: (i, 0)), pl.BlockSpec((tm, d), lambda i: (i, 0)),
                  _const_spec(wo.shape)],
        out_specs=pl.BlockSpec((tm, d), lambda i: (i, 0)),
        out_shape=jax.ShapeDtypeStruct((t, d), x.dtype),
        compiler_params=_params("arbitrary"),
        name="outproj",
    )(m, x, wo)


def _tile(n, pref):
    t = min(n, pref)
    while n % t:
        t //= 2
    return t


def _layer(xs, norm_g, w_in, a_re, a_im, log_dt, b_re, b_im, c_re, c_im, d_skip, w_glu, b_glu,
           q_g, k_g, rel_bias, w_bs, w_ba, w_out):
    d_model = w_in.shape[0]
    sw = d_skip.shape[0]
    groups, p_dim = a_re.shape[1], a_re.shape[2]
    h_dim = sw // groups
    hd = q_g.shape[0]
    aw = w_ba.shape[0]
    lc = CHUNK
    assert h_dim * lc % LANES == 0 and lc * 2 == LANES and h_dim % 2 == 0

    wt = w_in[:, :2 * sw].T.astype(BF16)
    wr = w_in[:, 2 * sw:].astype(BF16)
    tn = _tile(math.gcd(sw, aw), 1024)
    acts_t = ["none"] * (sw // tn) + ["silu"] * (sw // tn)
    acts_r = ["none"] * (3 * aw // tn) + ["silu"] * (aw // tn) + ["sigmoid"] * (2 * d_model // tn)
    pair = 2 * hd
    attn_cols = tuple(n * aw // pair for n in range(4))
    g2 = norm_g.reshape(1, d_model).astype(F32)
    dup = lambda v: jnp.concatenate([v, v], axis=-1)
    abase = jnp.stack([a_re, a_im, jnp.broadcast_to(log_dt[..., None], a_re.shape)], axis=1)
    abase = abase.transpose(2, 0, 1, 3).astype(F32)
    arow = dup(abase)[:, :, :, None, :]
    acol = abase[..., None]
    bt = dup(jnp.stack([b_re, b_im], axis=1).transpose(2, 0, 1, 4, 3).astype(F32))
    cc = jnp.stack([c_re, c_im], axis=1).transpose(2, 0, 1, 3, 4).astype(F32)
    c2 = dup(cc)
    ct = cc.transpose(0, 1, 2, 4, 3)
    qg2 = dup(q_g.astype(F32)).reshape(1, pair)
    kg2 = dup(k_g.astype(F32)).reshape(1, pair)
    dcol = d_skip.astype(F32).reshape(sw, 1)
    bgcol = b_glu.astype(F32).reshape(sw, 1)
    wgt = w_glu.T.astype(BF16)
    wbs = w_bs.astype(BF16)
    wba = w_ba.astype(BF16)
    wo = w_out.astype(BF16)

    w_t, f_t, e_t, dec = _ssm_ops(arow, acol, bt, c2, ct, groups=groups, h_dim=h_dim, p_dim=p_dim)
    bias = _attn_bias(rel_bias.astype(F32))

    projts, projs, u3s, ncs = [], [], [], []
    for x in xs:
        b, l, _ = x.shape
        t = b * l
        assert l % (GRID_W * ROW_UNROLL) == 0 and l % (WIN_ROWS * GRID_W) == 0
        assert (l // lc) & (l // lc - 1) == 0 and l // lc <= 2 ** SCAN_LEVELS
        projt, proj = _inproj(x.reshape(t, d_model), g2, wt, wr, acts_t=acts_t, acts_r=acts_r,
                              tm=_tile(t, 1024), tn=tn)
        projts.append(projt)
        projs.append(proj)
        u3s.append(_to_chunks(projt, groups=groups, h_dim=h_dim, tk=_tile(t // LANES, 8)))
        ncs.append(l // lc)

    y4s = _ssm_apply(u3s, w_t.reshape(groups, h_dim * lc, h_dim * lc), f_t.reshape(groups, h_dim * lc, 2 * LANES),
                     e_t, dec, n_chunks=ncs)

    outs = []
    for x, projt, proj, y4 in zip(xs, projts, projs, y4s):
        b, l, _ = x.shape
        t = b * l
        yt = _from_chunks(y4, groups=groups, h_dim=h_dim, tk=_tile(t // LANES, 8))
        oa = _attention(proj.reshape(b, l, -1), bias, qg2, kg2, cols=attn_cols, aw=aw, hd=hd).reshape(t, aw)
        m = _merge(yt, projt, oa, proj, dcol, bgcol, wgt, wbs, wba, sw=sw, aw=aw, tm=_tile(t, 512))
        out = _outproj(m, x.reshape(t, d_model), wo, tm=_tile(t, 512))
        outs.append(out.reshape(b, l, d_model))
    return outs


def kernel(x_prompt, x_sample, norm_g, w_in, ssm_a_re, ssm_a_im, ssm_log_dt, ssm_b_re, ssm_b_im, ssm_c_re, ssm_c_im, ssm_d, w_glu, b_glu, q_norm_g, k_norm_g, rel_bias, w_branch_ssm, w_branch_attn, w_out):
    xs = [x_prompt, x_sample]
    for layer in range(norm_g.shape[0]):
        xs = _layer(xs, norm_g[layer], w_in[layer], ssm_a_re[layer], ssm_a_im[layer], ssm_log_dt[layer],
                    ssm_b_re[layer], ssm_b_im[layer], ssm_c_re[layer], ssm_c_im[layer], ssm_d[layer],
                    w_glu[layer], b_glu[layer], q_norm_g[layer], k_norm_g[layer], rel_bias[layer],
                    w_branch_ssm[layer], w_branch_attn[layer], w_out[layer])
    return (xs[0], xs[1])
```

```python
import functools
import math

import numpy as np
import jax
import jax.numpy as jnp
from jax import lax
from jax.experimental import pallas as pl
from jax.experimental.pallas import tpu as pltpu

NORM_EPS = 1e-6
MASK_VALUE = -1e30
LOG2E = math.log2(math.e)
GRID_W = 64
WIN_ROWS = 8
WIN_COLS = 16
CHUNK = 64
LANES = 128
ROW_UNROLL = 8
MERGE_SPLIT = 2
SCAN_LEVELS = 7
VMEM_LIMIT_BYTES = 56 * 1024 * 1024

F32 = jnp.float32
BF16 = jnp.bfloat16
HIGHEST = lax.Precision.HIGHEST


def _params(*sem):
    return pltpu.CompilerParams(dimension_semantics=sem, vmem_limit_bytes=VMEM_LIMIT_BYTES)


def _sigmoid(x):
    return 0.5 * jnp.tanh(0.5 * x) + 0.5


def _head_norm(r, gain, hd):
    lane = lax.broadcasted_iota(jnp.int32, (1, 2 * hd), 1)
    left = lane < hd
    out = []
    for c in range(r.shape[1] // (2 * hd)):
        x = r[:, c * 2 * hd:(c + 1) * 2 * hd]
        x2 = x * x
        sa = jnp.sum(jnp.where(left, x2, 0.0), axis=-1, keepdims=True)
        sb = jnp.sum(jnp.where(left, 0.0, x2), axis=-1, keepdims=True)
        ms = jnp.where(left, sa, sb) * (1.0 / hd)
        out.append(x * lax.rsqrt(ms + NORM_EPS) * gain[:, c * 2 * hd:(c + 1) * 2 * hd])
    return jnp.concatenate(out, axis=1)


def _activate(r, act, gains, hd):
    if act == "silu":
        return r * _sigmoid(r)
    if act == "sigmoid":
        return _sigmoid(r)
    if act in gains:
        return _head_norm(r, gains[act][...], hd)
    return r


ACTS = ("none", "silu", "sigmoid", "qnorm", "knorm")


def _inproj_kernel(x_ref, g_ref, wt_ref, w_ref, qg_ref, kg_ref, outt_ref, out_ref, h_ref, *, acts_t, acts_r, hd):
    j = pl.program_id(1)
    n_t = len(acts_t)
    gains = {"qnorm": qg_ref, "knorm": kg_ref}

    @pl.when(j == 0)
    def _():
        x = x_ref[...]
        ms = jnp.mean(x * x, axis=-1, keepdims=True)
        h_ref[...] = (x * lax.rsqrt(ms + NORM_EPS) * g_ref[...]).astype(BF16)

    def tiles_with(acts, act, base):
        cond = None
        for idx, a in enumerate(acts):
            if a == act:
                c = j == base + idx
                cond = c if cond is None else cond | c
        return cond

    for act in ACTS:
        cond = tiles_with(acts_t, act, 0)
        if cond is not None:
            assert act not in gains
            @pl.when(cond)
            def _(act=act):
                r = lax.dot_general(wt_ref[...], h_ref[...], (((1,), (1,)), ((), ())), preferred_element_type=F32)
                outt_ref[...] = _activate(r, act, gains, hd).astype(BF16)
        cond = tiles_with(acts_r, act, n_t)
        if cond is not None:
            @pl.when(cond)
            def _(act=act):
                r = jnp.dot(h_ref[...], w_ref[...], preferred_element_type=F32)
                out_ref[...] = _activate(r, act, gains, hd).astype(BF16)


def _inproj(x, g, wt, w, qg, kg, *, acts_t, acts_r, tm, tn, hd):
    t, d = x.shape
    n_t, n_r = len(acts_t), len(acts_r)
    assert wt.shape == (n_t * tn, d) and w.shape == (d, n_r * tn) and set(acts_t) | set(acts_r) <= set(ACTS)
    kern = functools.partial(_inproj_kernel, acts_t=tuple(acts_t), acts_r=tuple(acts_r), hd=hd)
    return pl.pallas_call(
        kern,
        grid=(t // tm, n_t + n_r),
        in_specs=[
            pl.BlockSpec((tm, d), lambda i, j: (i, 0)),
            pl.BlockSpec((1, d), lambda i, j: (0, 0)),
            pl.BlockSpec((tn, d), lambda i, j: (jnp.minimum(j, n_t - 1), 0)),
            pl.BlockSpec((d, tn), lambda i, j: (0, jnp.maximum(j - n_t, 0))),
            pl.BlockSpec((1, tn), lambda i, j: (0, 0)),
            pl.BlockSpec((1, tn), lambda i, j: (0, 0)),
        ],
        out_specs=[
            pl.BlockSpec((tn, tm), lambda i, j: (jnp.minimum(j, n_t - 1), i)),
            pl.BlockSpec((tm, tn), lambda i, j: (i, jnp.maximum(j - n_t, 0))),
        ],
        out_shape=[jax.ShapeDtypeStruct((n_t * tn, t), BF16), jax.ShapeDtypeStruct((t, n_r * tn), BF16)],
        scratch_shapes=[pltpu.VMEM((tm, d), BF16)],
        compiler_params=_params("arbitrary", "arbitrary"),
        name="inproj",
    )(x, g, wt, w, qg, kg)


def _cmul(ar, ai, br, bi):
    return ar * br - ai * bi, ar * bi + ai * br


def _hi_lo(x):
    if x.dtype == BF16:
        return x, None
    hi = x.astype(BF16)
    return hi, (x - hi.astype(F32)).astype(BF16)


def _dot_hi_lo(a, b):
    ah, al = _hi_lo(a)
    bh, bl = _hi_lo(b)
    acc = jnp.dot(ah, bh, preferred_element_type=F32)
    if bl is not None:
        acc = acc + jnp.dot(ah, bl, preferred_element_type=F32)
    if al is not None:
        acc = acc + jnp.dot(al, bh, preferred_element_type=F32)
    return acc


def _discretize(a_re, a_im, log_dt):
    lam_re = jnp.minimum(a_re, -1e-4)
    lam_im = a_im
    dt = jnp.exp(log_dt)
    mag = jnp.exp(lam_re * dt)
    abar_re = mag * jnp.cos(lam_im * dt)
    abar_im = mag * jnp.sin(lam_im * dt)
    den = lam_re * lam_re + lam_im * lam_im
    f_re = ((abar_re - 1.0) * lam_re + abar_im * lam_im) / den
    f_im = (abar_im * lam_re - (abar_re - 1.0) * lam_im) / den
    return abar_re, abar_im, f_re, f_im


def _squarings(ar, ai, n):
    out = [(ar, ai)]
    for _ in range(n - 1):
        ar, ai = _cmul(ar, ai, ar, ai)
        out.append((ar, ai))
    return out


def _power_table(sq, exps, shape):
    pr = jnp.ones(shape, F32)
    pi = jnp.zeros(shape, F32)
    for k, (ar, ai) in enumerate(sq):
        bit = ((exps >> k) & 1) == 1
        fr = jnp.where(bit, ar, 1.0)
        fi = jnp.where(bit, ai, 0.0)
        pr, pi = _cmul(pr, pi, fr, fi)
    return pr, pi


def _ssm_ops_kernel(arow_ref, acol_ref, bt_ref, c_ref, ct_ref, w_ref, f_ref, e_ref, dec_ref, k_ref, *, h_dim, p_dim):
    lc = CHUNK
    nbits = lc.bit_length()
    lane = lax.broadcasted_iota(jnp.int32, (1, LANES), 1)
    left = lane < lc
    tloc = lane & (lc - 1)
    sub = lax.broadcasted_iota(jnp.int32, (lc, 1), 0)

    q2 = []
    rmat = []
    bb_rows = []
    sq_rows = []
    for d in range(2):
        abr, abi, fr, fi = _discretize(arow_ref[d, 0], arow_ref[d, 1], arow_ref[d, 2])
        bbr, bbi = _cmul(fr, fi, bt_ref[d, 0], bt_ref[d, 1])
        bb_rows.append((bbr, bbi))
        sq_row = _squarings(abr, abi, nbits + SCAN_LEVELS)
        sq_rows.append(sq_row)
        cr, ci = c_ref[d, 0], c_ref[d, 1]
        blocks = []
        for h in range(h_dim):
            qr, qi = _cmul(cr, ci, bbr[h:h + 1, :], bbi[h:h + 1, :])
            blocks.append(jnp.where(left, qr, -qi))
        q2.append(jnp.concatenate(blocks, axis=0))
        cabr, cabi, _, _ = _discretize(acol_ref[d, 0], acol_ref[d, 1], acol_ref[d, 2])
        sq_col = _squarings(cabr, cabi, nbits)
        lag = lane - lc
        if d == 0:
            exps, live = jnp.maximum(lag, 0), lag >= 0
        else:
            exps, live = jnp.maximum(-lag, 0), lag <= 0
        vr, vi = _power_table(sq_col, exps, (p_dim, LANES))
        rmat.append(jnp.concatenate([jnp.where(live, vr, 0.0), jnp.where(live, vi, 0.0)], axis=0))
        eexp = tloc + 1 if d == 0 else lc - tloc
        er, ei = _power_table(sq_col, eexp, (p_dim, LANES))
        n_rep = (h_dim * lc) // LANES
        er = jnp.concatenate([er] * n_rep, axis=1)
        ei = jnp.concatenate([ei] * n_rep, axis=1)
        rep_r = lax.broadcasted_iota(jnp.int32, (h_dim, h_dim * lc), 0)
        rep_c = lax.broadcasted_iota(jnp.int32, (h_dim, h_dim * lc), 1)
        rep = (_slot_channel(rep_c >> (lc.bit_length() - 1), h_dim) == rep_r).astype(BF16)
        cer = _dot_hi_lo(ct_ref[d, 0], rep)
        cei = _dot_hi_lo(ct_ref[d, 1], rep)
        gr, gi = _cmul(cer, cei, er, ei)
        e_ref[2 * d * p_dim:(2 * d + 1) * p_dim, :] = gr.astype(BF16)
        e_ref[(2 * d + 1) * p_dim:(2 * d + 2) * p_dim, :] = (-gi).astype(BF16)
        for k in range(SCAN_LEVELS):
            dr_, di_ = sq_row[nbits - 1 + k]
            dec_ref[(d * SCAN_LEVELS + k) * 2:(d * SCAN_LEVELS + k) * 2 + 1, :] = dr_
            dec_ref[(d * SCAN_LEVELS + k) * 2 + 1:(d * SCAN_LEVELS + k) * 2 + 2, :] = jnp.where(left, -di_, di_)

    kbi = _dot_hi_lo(q2[0], rmat[0]) + _dot_hi_lo(q2[1], rmat[1])
    k_ref[0] = kbi
    k_ref[1] = pltpu.roll(kbi, lc, 1)

    fexp = (lc - 1 - sub, sub)
    vrow = [_power_table(sq_rows[d][:nbits], fexp[d], (lc, LANES)) for d in range(2)]

    hh = h_dim // 2
    pre_roll_left = ((lane + sub) & (LANES - 1)) < lc

    def per_h(h, carry):
        slot = 2 * (h % hh) + h // hh
        for d in range(2):
            bbr, bbi = bb_rows[d]
            sel = (lax.broadcasted_iota(jnp.int32, (h_dim, 1), 0) == h).astype(F32)
            br = jnp.sum(bbr * sel, axis=0, keepdims=True)
            bi = jnp.sum(bbi * sel, axis=0, keepdims=True)
            zr, zi = _cmul(vrow[d][0], vrow[d][1], br, bi)
            f_ref[slot, :, d * LANES:(d + 1) * LANES] = jnp.where(left, zr, zi).astype(BF16)
        for m in range(hh):
            ra = k_ref[1, pl.ds(h * h_dim + m, 1), :]
            rb = k_ref[0, pl.ds(h * h_dim + m + hh, 1), :]
            src = jnp.where(pre_roll_left, jnp.broadcast_to(ra, (lc, LANES)), jnp.broadcast_to(rb, (lc, LANES)))
            tile = pltpu.roll(src, 0, 1, stride=1, stride_axis=0)
            w_ref[slot, :, m * LANES:(m + 1) * LANES] = tile.astype(BF16)
        return carry

    lax.fori_loop(0, h_dim, per_h, 0)


def _ssm_ops(arow, acol, bt, c2, ct, *, groups, h_dim, p_dim):
    lc = CHUNK
    kern = functools.partial(_ssm_ops_kernel, h_dim=h_dim, p_dim=p_dim)
    blk = lambda shape: pl.BlockSpec((None,) + shape, lambda g: (g,) + (0,) * len(shape))
    return pl.pallas_call(
        kern,
        grid=(groups,),
        in_specs=[blk((2, 3, 1, LANES)), blk((2, 3, p_dim, 1)), blk((2, 2, h_dim, LANES)),
                  blk((2, 2, h_dim, LANES)), blk((2, 2, p_dim, h_dim))],
        out_specs=[blk((h_dim, lc, h_dim * lc)), blk((h_dim, lc, 2 * LANES)),
                   blk((4 * p_dim, h_dim * lc)), blk((4 * SCAN_LEVELS, LANES))],
        out_shape=[jax.ShapeDtypeStruct((groups, h_dim, lc, h_dim * lc), BF16),
                   jax.ShapeDtypeStruct((groups, h_dim, lc, 2 * LANES), BF16),
                   jax.ShapeDtypeStruct((groups, 4 * p_dim, h_dim * lc), BF16),
                   jax.ShapeDtypeStruct((groups, 4 * SCAN_LEVELS, LANES), F32)],
        scratch_shapes=[pltpu.VMEM((2, h_dim * h_dim, LANES), F32)],
        compiler_params=_params("arbitrary"),
        name="ssm_ops",
    )(arow, acol, bt, c2, ct)


def _decay_mul(x, dec_ref, d, level):
    row = (d * SCAN_LEVELS + level) * 2
    return x * dec_ref[row:row + 1, :] + pltpu.roll(x, CHUNK, 1) * dec_ref[row + 1:row + 2, :]


def _pair_scan(z, dec_ref, d, n_pairs):
    m = z.shape[0]
    cidx = lax.broadcasted_iota(jnp.int32, (m, 1), 0) & (n_pairs - 1)

    def shifted(x, sh):
        if d == 0:
            return jnp.where(cidx >= sh, pltpu.roll(x, sh, 0), 0.0)
        return jnp.where(cidx < n_pairs - sh, pltpu.roll(x, m - sh, 0), 0.0)

    x = z
    for k in range(n_pairs.bit_length() - 1):
        x = x + _decay_mul(shifted(x, 1 << k), dec_ref, d, k + 1)
    return shifted(x, 1)


def _lane_halves(a, b, left):
    ax = a.ndim - 1
    return jnp.where(left, a, pltpu.roll(b, CHUNK, ax)), jnp.where(left, pltpu.roll(a, CHUNK, ax), b)


def _ssm_apply_kernel(*refs, n_seq, n_chunks):
    u_refs = refs[:n_seq]
    w_ref, f_ref, e_ref, dec_ref = refs[n_seq:n_seq + 4]
    y_refs = refs[n_seq + 4:]
    for u_ref, y_ref, nc in zip(u_refs, y_refs, n_chunks):
        half = u_ref.shape[1]
        u = u_ref[...].reshape(2 * half, u_ref.shape[2])
        z = jnp.dot(u, f_ref[...], preferred_element_type=F32)
        ze, zo = z[:half], z[half:]
        pf = _pair_scan(_decay_mul(ze[:, :LANES], dec_ref, 0, 0) + zo[:, :LANES], dec_ref, 0, nc // 2)
        sf = jnp.concatenate([pf, _decay_mul(pf, dec_ref, 0, 0) + ze[:, :LANES]], axis=0)
        pb = _pair_scan(ze[:, LANES:] + _decay_mul(zo[:, LANES:], dec_ref, 1, 0), dec_ref, 1, nc // 2)
        sb = jnp.concatenate([_decay_mul(pb, dec_ref, 1, 0) + zo[:, LANES:], pb], axis=0)
        s = jnp.concatenate([sf, sb], axis=1)
        acc = jnp.dot(u, w_ref[...], preferred_element_type=F32)
        acc = acc + jnp.dot(s.astype(BF16), e_ref[...], preferred_element_type=F32)
        y_ref[...] = acc.astype(BF16).reshape(y_ref.shape)


def _slot_channel(slot, h_dim):
    return (slot >> 1) + (h_dim // 2) * (slot & 1)


def _to_chunks_kernel(x_ref, o_ref, *, groups, h_dim):
    tk = o_ref.shape[2]
    hh = h_dim // 2
    left = lax.broadcasted_iota(jnp.int32, (1, 1, LANES), 2) < CHUNK
    x = x_ref[...].astype(F32).reshape(groups, 2, hh, tk * LANES)
    ev, od = [], []
    for k in range(tk):
        e, o = _lane_halves(x[:, 0, :, k * LANES:(k + 1) * LANES], x[:, 1, :, k * LANES:(k + 1) * LANES], left)
        ev.append(e)
        od.append(o)
    for p, parts in enumerate((ev, od)):
        o_ref[:, p] = jnp.stack(parts, axis=1).reshape(groups, tk, hh * LANES).astype(o_ref.dtype)


def _from_chunks_kernel(x_ref, o_ref, *, groups, h_dim):
    tk = x_ref.shape[2]
    hh = h_dim // 2
    left = lax.broadcasted_iota(jnp.int32, (1, 1, LANES), 2) < CHUNK
    ye = x_ref[:, 0].astype(F32).reshape(groups, tk, hh, LANES)
    yo = x_ref[:, 1].astype(F32).reshape(groups, tk, hh, LANES)
    lo, hi = zip(*[_lane_halves(ye[:, k], yo[:, k], left) for k in range(tk)])
    y = jnp.stack([jnp.concatenate(lo, axis=-1), jnp.concatenate(hi, axis=-1)], axis=1)
    o_ref[...] = y.reshape(groups * h_dim, tk * LANES).astype(o_ref.dtype)


def _to_chunks(xt, *, groups, h_dim, tk):
    t = xt.shape[1]
    return pl.pallas_call(
        functools.partial(_to_chunks_kernel, groups=groups, h_dim=h_dim),
        grid=(t // (tk * LANES),),
        in_specs=[pl.BlockSpec((groups * h_dim, tk * LANES), lambda i: (0, i))],
        out_specs=pl.BlockSpec((groups, 2, tk, h_dim * CHUNK), lambda i: (0, 0, i, 0)),
        out_shape=jax.ShapeDtypeStruct((groups, 2, t // LANES, h_dim * CHUNK), BF16),
        compiler_params=_params("arbitrary"),
        name="to_chunks",
    )(xt)


def _from_chunks(y4, *, groups, h_dim, tk):
    t = y4.shape[2] * LANES
    return pl.pallas_call(
        functools.partial(_from_chunks_kernel, groups=groups, h_dim=h_dim),
        grid=(t // (tk * LANES),),
        in_specs=[pl.BlockSpec((groups, 2, tk, h_dim * CHUNK), lambda i: (0, 0, i, 0))],
        out_specs=pl.BlockSpec((groups * h_dim, tk * LANES), lambda i: (0, i)),
        out_shape=jax.ShapeDtypeStruct((groups * h_dim, t), BF16),
        compiler_params=_params("arbitrary"),
        name="from_chunks",
    )(y4)


def _ssm_apply(u4s, w, f, e, dec, *, n_chunks):
    groups = w.shape[0]
    blk = lambda shape: pl.BlockSpec((None,) + shape, lambda g: (g,) + (0,) * len(shape))
    kern = functools.partial(_ssm_apply_kernel, n_seq=len(u4s), n_chunks=tuple(n_chunks))
    return pl.pallas_call(
        kern,
        grid=(groups,),
        in_specs=[blk(u.shape[1:]) for u in u4s]
        + [blk(w.shape[1:]), blk(f.shape[1:]), blk(e.shape[1:]), blk(dec.shape[1:])],
        out_specs=[blk(u.shape[1:]) for u in u4s],
        out_shape=[jax.ShapeDtypeStruct(u.shape, BF16) for u in u4s],
        compiler_params=_params("arbitrary"),
        name="ssm_apply",
    )(*u4s, w, f, e, dec)


def _attn_bias_kernel(rb_ref, out_ref, pair_ref, *, hd):
    hp = pl.program_id(0)
    gw = GRID_W
    cq = lax.broadcasted_iota(jnp.int32, (gw, LANES), 0)
    lane = lax.broadcasted_iota(jnp.int32, (gw, LANES), 1)
    left = lane < gw
    ck = lane & (gw - 1)
    cstart = jnp.clip(cq - WIN_COLS // 2, 0, gw - WIN_COLS)
    valid = (ck >= cstart) & (ck < cstart + WIN_COLS)
    dc = ck - cq + (WIN_COLS - 1)
    n_dr = 2 * WIN_ROWS - 1
    for head in range(2):
        hidx = hp * 2 + head
        for dr in range(n_dr - 1):
            acc = jnp.zeros((gw, LANES), F32)
            for i in range(2 * WIN_COLS - 1):
                coef = jnp.where(left, rb_ref[hidx, dr, i], rb_ref[hidx, dr + 1, i])
                acc = jnp.where(dc == i, coef, acc)
            pair_ref[head, dr] = jnp.where(valid, acc * LOG2E, MASK_VALUE)
    for dr0 in range(WIN_ROWS):
        for head in range(2):
            for wq in range(WIN_ROWS // 2):
                out_ref[dr0, head * gw:(head + 1) * gw, wq * LANES:(wq + 1) * LANES] = pair_ref[head, dr0 + 2 * wq]


def _attn_bias(rel_bias):
    nh = rel_bias.shape[0]
    gw = GRID_W
    return pl.pallas_call(
        functools.partial(_attn_bias_kernel, hd=gw),
        grid=(nh // 2,),
        in_specs=[pl.BlockSpec(memory_space=pltpu.SMEM)],
        out_specs=pl.BlockSpec((None, WIN_ROWS, 2 * gw, WIN_ROWS * gw), lambda h: (h, 0, 0, 0)),
        out_shape=jax.ShapeDtypeStruct((nh // 2, WIN_ROWS, 2 * gw, WIN_ROWS * gw), F32),
        scratch_shapes=[pltpu.VMEM((2, 2 * WIN_ROWS - 2, gw, LANES), F32)],
        compiler_params=_params("arbitrary"),
        name="attn_bias",
    )(rel_bias)


def _attn_kernel(q_ref, k_ref, v_ref, za_ref, bias_ref, o_ref, *, rows, hd):
    gw = GRID_W
    win = WIN_ROWS * gw
    lane = lax.broadcasted_iota(jnp.int32, (1, 2 * hd), 1)
    left = lane < hd

    def rows_body(gi, carry):
        slices, scores, probs = [], [], []
        for i in range(ROW_UNROLL):
            r = gi * ROW_UNROLL + i
            rstart = jnp.clip(r - WIN_ROWS // 2, 0, rows - WIN_ROWS)
            dr0 = rstart - r + (WIN_ROWS - 1)
            qs = pl.ds(pl.multiple_of(r * gw, gw), gw)
            ks = pl.ds(pl.multiple_of(rstart * gw, gw), win)
            slices.append((qs, ks))
            q = q_ref[qs, :]
            zero = jnp.zeros_like(q)
            q2 = jnp.concatenate([jnp.where(left, q, zero), jnp.where(left, zero, q)], axis=0)
            s = lax.dot_general(q2, k_ref[ks, :], (((1,), (1,)), ((), ())), preferred_element_type=F32)
            scores.append(s + bias_ref[dr0])
        for s in scores:
            m = jnp.max(s, axis=-1, keepdims=True)
            p = jnp.exp2(s - m)
            probs.append((p.astype(BF16), jnp.sum(p, axis=-1, keepdims=True)))
        for (qs, ks), (p, l) in zip(slices, probs):
            o2 = jnp.dot(p, v_ref[ks, :], preferred_element_type=F32) / l
            o = jnp.where(left, o2[:gw], o2[gw:]) * za_ref[qs, :].astype(F32)
            o_ref[qs, :] = o.astype(BF16)
        return carry

    lax.fori_loop(0, rows // ROW_UNROLL, rows_body, 0)


def _attention(proj3, bias, *, cols, aw, hd):
    b, l, _ = proj3.shape
    rows = l // GRID_W
    nhp = aw // (2 * hd)
    assert rows % ROW_UNROLL == 0
    col = lambda off: pl.BlockSpec((None, l, 2 * hd), lambda h, i, off=off: (i, 0, off + h))
    kern = functools.partial(_attn_kernel, rows=rows, hd=hd)
    return pl.pallas_call(
        kern,
        grid=(nhp, b),
        in_specs=[col(cols[0]), col(cols[1]), col(cols[2]), col(cols[3]),
                  pl.BlockSpec((None,) + bias.shape[1:], lambda h, i: (h, 0, 0, 0))],
        out_specs=pl.BlockSpec((None, l, 2 * hd), lambda h, i: (i, 0, h)),
        out_shape=jax.ShapeDtypeStruct((b, l, aw), BF16),
        compiler_params=_params("arbitrary", "arbitrary"),
        name="attention",
    )(proj3, proj3, proj3, proj3, bias)


def _gelu_tanh(x):
    c = math.sqrt(2.0 / math.pi)
    return 0.5 * x * (1.0 + jnp.tanh(c * (x + 0.044715 * (x * x * x))))


def _merge_kernel(yt_ref, ut_ref, zt_ref, oa_ref, gs_ref, ga_ref, d_ref, bg_ref, wg_ref, wbs_ref, wba_ref, m_ref):
    tm = m_ref.shape[0]
    sub = tm // MERGE_SPLIT
    parts = [slice(i * sub, (i + 1) * sub) for i in range(MERGE_SPLIT)]
    acts = [_gelu_tanh(yt_ref[:, p].astype(F32) + d_ref[...] * ut_ref[:, p].astype(F32)) for p in parts]
    mas = [jnp.dot(oa_ref[p, :], wba_ref[...], preferred_element_type=F32) for p in parts]
    gates = [jnp.dot(wg_ref[...], a.astype(BF16), preferred_element_type=F32) + bg_ref[...] for a in acts]
    osts = [(a * _sigmoid(g) * zt_ref[:, p].astype(F32)).astype(BF16) for a, g, p in zip(acts, gates, parts)]
    mss = [lax.dot_general(o, wbs_ref[...], (((0,), (0,)), ((), ())), preferred_element_type=F32) for o in osts]
    for p, ms, ma in zip(parts, mss, mas):
        m_ref[p, :] = (gs_ref[p, :].astype(F32) * ms + ga_ref[p, :].astype(F32) * ma).astype(BF16)


def _outproj_kernel(m_ref, x_ref, wo_ref, out_ref):
    out_ref[...] = x_ref[...] + jnp.dot(m_ref[...], wo_ref[...], preferred_element_type=F32)


def _const_spec(shape):
    return pl.BlockSpec(shape, lambda i: (0,) * len(shape), pipeline_mode=pl.Buffered(1))


def _merge(yt, projt, oa, proj, dcol, bgcol, wgt, wbs, wba, *, sw, aw, tm):
    t = oa.shape[0]
    d = wbs.shape[1]
    assert (4 * aw) % d == 0
    n_gs = (4 * aw) // d
    return pl.pallas_call(
        _merge_kernel,
        grid=(t // tm,),
        in_specs=[
            pl.BlockSpec((sw, tm), lambda i: (0, i)),
            pl.BlockSpec((sw, tm), lambda i: (0, i)),
            pl.BlockSpec((sw, tm), lambda i: (1, i)),
            pl.BlockSpec((tm, aw), lambda i: (i, 0)),
            pl.BlockSpec((tm, d), lambda i: (i, n_gs)),
            pl.BlockSpec((tm, d), lambda i: (i, n_gs + 1)),
            _const_spec((sw, 1)), _const_spec((sw, 1)),
            _const_spec(wgt.shape), _const_spec(wbs.shape), _const_spec(wba.shape),
        ],
        out_specs=pl.BlockSpec((tm, d), lambda i: (i, 0)),
        out_shape=jax.ShapeDtypeStruct((t, d), BF16),
        compiler_params=_params("arbitrary"),
        name="merge",
    )(yt, projt, projt, oa, proj, proj, dcol, bgcol, wgt, wbs, wba)


def _outproj(m, x, wo, *, tm):
    t, d = x.shape
    return pl.pallas_call(
        _outproj_kernel,
        grid=(t // tm,),
        in_specs=[pl.BlockSpec((tm, d), lambda i: (i, 0)), pl.BlockSpec((tm, d), lambda i: (i, 0)),
                  _const_spec(wo.shape)],
        out_specs=pl.BlockSpec((tm, d), lambda i: (i, 0)),
        out_shape=jax.ShapeDtypeStruct((t, d), x.dtype),
        compiler_params=_params("arbitrary"),
        name="outproj",
    )(m, x, wo)


def _tile(n, pref):
    t = min(n, pref)
    while n % t:
        t //= 2
    return t


def _layer(xs, norm_g, w_in, a_re, a_im, log_dt, b_re, b_im, c_re, c_im, d_skip, w_glu, b_glu,
           q_g, k_g, rel_bias, w_bs, w_ba, w_out):
    d_model = w_in.shape[0]
    sw = d_skip.shape[0]
    groups, p_dim = a_re.shape[1], a_re.shape[2]
    h_dim = sw // groups
    hd = q_g.shape[0]
    aw = w_ba.shape[0]
    lc = CHUNK
    assert h_dim * lc % LANES == 0 and lc * 2 == LANES and h_dim % 2 == 0

    wt = w_in[:, :2 * sw].T.astype(BF16)
    wr = w_in[:, 2 * sw:].astype(BF16)
    tn = _tile(math.gcd(sw, aw), 1024)
    acts_t = ["none"] * (sw // tn) + ["silu"] * (sw // tn)
    acts_r = (["qnorm"] * (aw // tn) + ["knorm"] * (aw // tn) + ["none"] * (aw // tn) + ["silu"] * (aw // tn)
              + ["sigmoid"] * (2 * d_model // tn))
    pair = 2 * hd
    attn_cols = tuple(n * aw // pair for n in range(4))
    g2 = norm_g.reshape(1, d_model).astype(F32)
    dup = lambda v: jnp.concatenate([v, v], axis=-1)
    abase = jnp.stack([a_re, a_im, jnp.broadcast_to(log_dt[..., None], a_re.shape)], axis=1)
    abase = abase.transpose(2, 0, 1, 3).astype(F32)
    arow = dup(abase)[:, :, :, None, :]
    acol = abase[..., None]
    bt = dup(jnp.stack([b_re, b_im], axis=1).transpose(2, 0, 1, 4, 3).astype(F32))
    cc = jnp.stack([c_re, c_im], axis=1).transpose(2, 0, 1, 3, 4).astype(F32)
    c2 = dup(cc)
    ct = cc.transpose(0, 1, 2, 4, 3)
    assert tn % hd == 0
    qg_row = jnp.tile(q_g.astype(F32) * (hd ** -0.5 * LOG2E), tn // hd).reshape(1, tn)
    kg_row = jnp.tile(k_g.astype(F32), tn // hd).reshape(1, tn)
    dcol = d_skip.astype(F32).reshape(sw, 1)
    bgcol = b_glu.astype(F32).reshape(sw, 1)
    wgt = w_glu.T.astype(BF16)
    wbs = w_bs.astype(BF16)
    wba = w_ba.astype(BF16)
    wo = w_out.astype(BF16)

    w_t, f_t, e_t, dec = _ssm_ops(arow, acol, bt, c2, ct, groups=groups, h_dim=h_dim, p_dim=p_dim)
    bias = _attn_bias(rel_bias.astype(F32))

    projts, projs, u3s, ncs = [], [], [], []
    for x in xs:
        b, l, _ = x.shape
        t = b * l
        assert l % (GRID_W * ROW_UNROLL) == 0 and l % (WIN_ROWS * GRID_W) == 0
        assert (l // lc) & (l // lc - 1) == 0 and l // lc <= 2 ** SCAN_LEVELS
        projt, proj = _inproj(x.reshape(t, d_model), g2, wt, wr, qg_row, kg_row, acts_t=acts_t, acts_r=acts_r,
                              tm=_tile(t, 1024), tn=tn, hd=hd)
        projts.append(projt)
        projs.append(proj)
        u3s.append(_to_chunks(projt, groups=groups, h_dim=h_dim, tk=_tile(t // LANES, 8)))
        ncs.append(l // lc)

    y4s = _ssm_apply(u3s, w_t.reshape(groups, h_dim * lc, h_dim * lc), f_t.reshape(groups, h_dim * lc, 2 * LANES),
                     e_t, dec, n_chunks=ncs)

    outs = []
    for x, projt, proj, y4 in zip(xs, projts, projs, y4s):
        b, l, _ = x.shape
        t = b * l
        yt = _from_chunks(y4, groups=groups, h_dim=h_dim, tk=_tile(t // LANES, 8))
        oa = _attention(proj.reshape(b, l, -1), bias, cols=attn_cols, aw=aw, hd=hd).reshape(t, aw)
        m = _merge(yt, projt, oa, proj, dcol, bgcol, wgt, wbs, wba, sw=sw, aw=aw, tm=_tile(t, 512))
        out = _outproj(m, x.reshape(t, d_model), wo, tm=_tile(t, 512))
        outs.append(out.reshape(b, l, d_model))
    return outs


def kernel(x_prompt, x_sample, norm_g, w_in, ssm_a_re, ssm_a_im, ssm_log_dt, ssm_b_re, ssm_b_im, ssm_c_re, ssm_c_im, ssm_d, w_glu, b_glu, q_norm_g, k_norm_g, rel_bias, w_branch_ssm, w_branch_attn, w_out):
    xs = [x_prompt, x_sample]
    for layer in range(norm_g.shape[0]):
        xs = _layer(xs, norm_g[layer], w_in[layer], ssm_a_re[layer], ssm_a_im[layer], ssm_log_dt[layer],
                    ssm_b_re[layer], ssm_b_im[layer], ssm_c_re[layer], ssm_c_im[layer], ssm_d[layer],
                    w_glu[layer], b_glu[layer], q_norm_g[layer], k_norm_g[layer], rel_bias[layer],
                    w_branch_ssm[layer], w_branch_attn[layer], w_out[layer])
    return (xs[0], xs[1])
```

```python
import functools
import math

import numpy as np
import jax
import jax.numpy as jnp
from jax import lax
from jax.experimental import pallas as pl
from jax.experimental.pallas import tpu as pltpu

NORM_EPS = 1e-6
MASK_VALUE = -1e30
LOG2E = math.log2(math.e)
GRID_W = 64
WIN_ROWS = 8
WIN_COLS = 16
CHUNK = 64
LANES = 128
ROW_UNROLL = 8
NORM_SPLIT = 4
MERGE_SPLIT = 2
SCAN_LEVELS = 7
VMEM_LIMIT_BYTES = 56 * 1024 * 1024
INPROJ_TM = 1024
INPROJ_TN = 1024
MERGE_TM = 512
OUTPROJ_TM = 512
RELAYOUT_TK = 8

F32 = jnp.float32
BF16 = jnp.bfloat16


def _params(*sem):
    return pltpu.CompilerParams(dimension_semantics=sem, vmem_limit_bytes=VMEM_LIMIT_BYTES)


def _sigmoid(x):
    return 0.5 * jnp.tanh(0.5 * x) + 0.5


def _head_norm(r, gain, hd):
    lane = lax.broadcasted_iota(jnp.int32, (1, 2 * hd), 1)
    left = lane < hd
    out = []
    for c in range(r.shape[1] // (2 * hd)):
        x = r[:, c * 2 * hd:(c + 1) * 2 * hd]
        x2 = x * x
        sa = jnp.sum(jnp.where(left, x2, 0.0), axis=-1, keepdims=True)
        sb = jnp.sum(jnp.where(left, 0.0, x2), axis=-1, keepdims=True)
        ms = jnp.where(left, sa, sb) * (1.0 / hd)
        out.append(x * lax.rsqrt(ms + NORM_EPS) * gain[:, c * 2 * hd:(c + 1) * 2 * hd])
    return jnp.concatenate(out, axis=1)


def _activate(r, act, gains, hd):
    if act == "silu":
        return r * _sigmoid(r)
    if act == "sigmoid":
        return _sigmoid(r)
    if act in gains:
        return _head_norm(r, gains[act][...], hd)
    return r


ACTS = ("none", "silu", "sigmoid", "qnorm", "knorm")


def _inproj_kernel(x_ref, g_ref, wt_ref, w_ref, qg_ref, kg_ref, outt_ref, out_ref, h_ref, *, acts_t, acts_r, hd):
    j = pl.program_id(1)
    n_t = len(acts_t)
    gains = {"qnorm": qg_ref, "knorm": kg_ref}

    @pl.when(j == 0)
    def _():
        x = x_ref[...]
        ms = jnp.mean(x * x, axis=-1, keepdims=True)
        h_ref[...] = (x * lax.rsqrt(ms + NORM_EPS) * g_ref[...]).astype(BF16)

    def tiles_with(acts, act, base):
        cond = None
        for idx, a in enumerate(acts):
            if a == act:
                c = j == base + idx
                cond = c if cond is None else cond | c
        return cond

    for act in ACTS:
        cond = tiles_with(acts_t, act, 0)
        if cond is not None:
            assert act not in gains
            @pl.when(cond)
            def _(act=act):
                r = lax.dot_general(wt_ref[...], h_ref[...], (((1,), (1,)), ((), ())), preferred_element_type=F32)
                outt_ref[...] = _activate(r, act, gains, hd).astype(BF16)
        cond = tiles_with(acts_r, act, n_t)
        if cond is not None:
            @pl.when(cond)
            def _(act=act):
                n_sub = NORM_SPLIT if act in gains else 1
                sub = h_ref.shape[0] // n_sub
                for p in range(n_sub):
                    rows = slice(p * sub, (p + 1) * sub)
                    r = jnp.dot(h_ref[rows, :], w_ref[...], preferred_element_type=F32)
                    out_ref[rows, :] = _activate(r, act, gains, hd).astype(BF16)


def _inproj(x, g, wt, w, qg, kg, *, acts_t, acts_r, tm, tn, hd):
    t, d = x.shape
    n_t, n_r = len(acts_t), len(acts_r)
    assert wt.shape == (n_t * tn, d) and w.shape == (d, n_r * tn) and set(acts_t) | set(acts_r) <= set(ACTS)
    kern = functools.partial(_inproj_kernel, acts_t=tuple(acts_t), acts_r=tuple(acts_r), hd=hd)
    return pl.pallas_call(
        kern,
        grid=(t // tm, n_t + n_r),
        in_specs=[
            pl.BlockSpec((tm, d), lambda i, j: (i, 0)),
            pl.BlockSpec((1, d), lambda i, j: (0, 0)),
            pl.BlockSpec((tn, d), lambda i, j: (jnp.minimum(j, n_t - 1), 0)),
            pl.BlockSpec((d, tn), lambda i, j: (0, jnp.maximum(j - n_t, 0))),
            pl.BlockSpec((1, tn), lambda i, j: (0, 0)),
            pl.BlockSpec((1, tn), lambda i, j: (0, 0)),
        ],
        out_specs=[
            pl.BlockSpec((tn, tm), lambda i, j: (jnp.minimum(j, n_t - 1), i)),
            pl.BlockSpec((tm, tn), lambda i, j: (i, jnp.maximum(j - n_t, 0))),
        ],
        out_shape=[jax.ShapeDtypeStruct((n_t * tn, t), BF16), jax.ShapeDtypeStruct((t, n_r * tn), BF16)],
        scratch_shapes=[pltpu.VMEM((tm, d), BF16)],
        compiler_params=_params("arbitrary", "arbitrary"),
        name="inproj",
    )(x, g, wt, w, qg, kg)


def _cmul(ar, ai, br, bi):
    return ar * br - ai * bi, ar * bi + ai * br


def _hi_lo(x):
    if x.dtype == BF16:
        return x, None
    hi = x.astype(BF16)
    return hi, (x - hi.astype(F32)).astype(BF16)


def _dot_hi_lo(a, b):
    ah, al = _hi_lo(a)
    bh, bl = _hi_lo(b)
    acc = jnp.dot(ah, bh, preferred_element_type=F32)
    if bl is not None:
        acc = acc + jnp.dot(ah, bl, preferred_element_type=F32)
    if al is not None:
        acc = acc + jnp.dot(al, bh, preferred_element_type=F32)
    return acc


def _discretize(a_re, a_im, log_dt):
    lam_re = jnp.minimum(a_re, -1e-4)
    lam_im = a_im
    dt = jnp.exp(log_dt)
    mag = jnp.exp(lam_re * dt)
    abar_re = mag * jnp.cos(lam_im * dt)
    abar_im = mag * jnp.sin(lam_im * dt)
    den = lam_re * lam_re + lam_im * lam_im
    f_re = ((abar_re - 1.0) * lam_re + abar_im * lam_im) / den
    f_im = (abar_im * lam_re - (abar_re - 1.0) * lam_im) / den
    return abar_re, abar_im, f_re, f_im


def _squarings(ar, ai, n):
    out = [(ar, ai)]
    for _ in range(n - 1):
        ar, ai = _cmul(ar, ai, ar, ai)
        out.append((ar, ai))
    return out


def _power_table(sq, exps, shape):
    pr = jnp.ones(shape, F32)
    pi = jnp.zeros(shape, F32)
    for k, (ar, ai) in enumerate(sq):
        bit = ((exps >> k) & 1) == 1
        fr = jnp.where(bit, ar, 1.0)
        fi = jnp.where(bit, ai, 0.0)
        pr, pi = _cmul(pr, pi, fr, fi)
    return pr, pi


def _ssm_ops_kernel(arow_ref, bt_ref, c_ref, ct_ref, w_ref, f_ref, e_ref, dec_ref, k_ref, *, h_dim, p_dim):
    lc = CHUNK
    nbits = lc.bit_length()
    lane = lax.broadcasted_iota(jnp.int32, (1, LANES), 1)
    left = lane < lc
    sub = lax.broadcasted_iota(jnp.int32, (lc, 1), 0)
    eye = lax.broadcasted_iota(jnp.int32, (p_dim, 1), 0) == lane

    q2 = []
    rmat = []
    bb_rows = []
    sq_rows = []
    for d in range(2):
        abr, abi, fr, fi = _discretize(arow_ref[d, 0], arow_ref[d, 1], arow_ref[d, 2])
        bbr, bbi = _cmul(fr, fi, bt_ref[d, 0], bt_ref[d, 1])
        bb_rows.append((bbr, bbi))
        sq_row = _squarings(abr, abi, nbits + SCAN_LEVELS)
        sq_rows.append(sq_row)
        cr, ci = c_ref[d, 0], c_ref[d, 1]
        blocks = []
        for h in range(h_dim):
            qr, qi = _cmul(cr, ci, bbr[h:h + 1, :], bbi[h:h + 1, :])
            blocks.append(jnp.where(left, qr, -qi))
        q2.append(jnp.concatenate(blocks, axis=0))
        cabr = jnp.sum(jnp.where(eye, abr, 0.0), axis=-1, keepdims=True)
        cabi = jnp.sum(jnp.where(eye, abi, 0.0), axis=-1, keepdims=True)
        sq_col = _squarings(cabr, cabi, nbits)
        lag = lane - lc
        if d == 0:
            exps, live = jnp.maximum(lag, 0), lag >= 0
        else:
            exps, live = jnp.maximum(-lag, 0), lag <= 0
        vr, vi = _power_table(sq_col, exps, (p_dim, LANES))
        rmat.append(jnp.concatenate([jnp.where(live, vr, 0.0), jnp.where(live, vi, 0.0)], axis=0))
        if d == 0:
            er, ei = _cmul(jnp.where(left, pltpu.roll(vr, lc, 1), vr), jnp.where(left, pltpu.roll(vi, lc, 1), vi),
                           cabr, cabi)
        else:
            er, ei = jnp.where(left, vr, pltpu.roll(vr, lc, 1)), jnp.where(left, vi, pltpu.roll(vi, lc, 1))
        n_rep = (h_dim * lc) // LANES
        er = jnp.concatenate([er] * n_rep, axis=1)
        ei = jnp.concatenate([ei] * n_rep, axis=1)
        rep_r = lax.broadcasted_iota(jnp.int32, (h_dim, h_dim * lc), 0)
        rep_c = lax.broadcasted_iota(jnp.int32, (h_dim, h_dim * lc), 1)
        rep = (_slot_channel(rep_c >> (lc.bit_length() - 1), h_dim) == rep_r).astype(BF16)
        cer = _dot_hi_lo(ct_ref[d, 0], rep)
        cei = _dot_hi_lo(ct_ref[d, 1], rep)
        gr, gi = _cmul(cer, cei, er, ei)
        e_ref[2 * d * p_dim:(2 * d + 1) * p_dim, :] = gr.astype(BF16)
        e_ref[(2 * d + 1) * p_dim:(2 * d + 2) * p_dim, :] = (-gi).astype(BF16)
        for k in range(SCAN_LEVELS):
            dr_, di_ = sq_row[nbits - 1 + k]
            dec_ref[(d * SCAN_LEVELS + k) * 2:(d * SCAN_LEVELS + k) * 2 + 1, :] = dr_
            dec_ref[(d * SCAN_LEVELS + k) * 2 + 1:(d * SCAN_LEVELS + k) * 2 + 2, :] = jnp.where(left, -di_, di_)

    kbi = _dot_hi_lo(q2[0], rmat[0]) + _dot_hi_lo(q2[1], rmat[1])
    k_ref[0] = kbi
    k_ref[1] = pltpu.roll(kbi, lc, 1)

    fexp = (lc - 1 - sub, sub)
    vrow = [_power_table(sq_rows[d][:nbits], fexp[d], (lc, LANES)) for d in range(2)]

    hh = h_dim // 2
    pre_roll_left = ((lane + sub) & (LANES - 1)) < lc

    def per_h(h, carry):
        slot = 2 * (h % hh) + h // hh
        for d in range(2):
            bbr, bbi = bb_rows[d]
            sel = (lax.broadcasted_iota(jnp.int32, (h_dim, 1), 0) == h).astype(F32)
            br = jnp.sum(bbr * sel, axis=0, keepdims=True)
            bi = jnp.sum(bbi * sel, axis=0, keepdims=True)
            zr, zi = _cmul(vrow[d][0], vrow[d][1], br, bi)
            f_ref[slot, :, d * LANES:(d + 1) * LANES] = jnp.where(left, zr, zi).astype(BF16)
        for m in range(hh):
            ra = k_ref[1, pl.ds(h * h_dim + m, 1), :]
            rb = k_ref[0, pl.ds(h * h_dim + m + hh, 1), :]
            src = jnp.where(pre_roll_left, jnp.broadcast_to(ra, (lc, LANES)), jnp.broadcast_to(rb, (lc, LANES)))
            tile = pltpu.roll(src, 0, 1, stride=1, stride_axis=0)
            w_ref[slot, :, m * LANES:(m + 1) * LANES] = tile.astype(BF16)
        return carry

    lax.fori_loop(0, h_dim, per_h, 0, unroll=4)


def _ssm_ops(arow, bt, c2, ct, *, groups, h_dim, p_dim):
    lc = CHUNK
    kern = functools.partial(_ssm_ops_kernel, h_dim=h_dim, p_dim=p_dim)
    blk = lambda shape: pl.BlockSpec((None,) + shape, lambda g: (g,) + (0,) * len(shape))
    return pl.pallas_call(
        kern,
        grid=(groups,),
        in_specs=[blk((2, 3, 1, LANES)), blk((2, 2, h_dim, LANES)),
                  blk((2, 2, h_dim, LANES)), blk((2, 2, p_dim, h_dim))],
        out_specs=[blk((h_dim, lc, h_dim * lc)), blk((h_dim, lc, 2 * LANES)),
                   blk((4 * p_dim, h_dim * lc)), blk((4 * SCAN_LEVELS, LANES))],
        out_shape=[jax.ShapeDtypeStruct((groups, h_dim, lc, h_dim * lc), BF16),
                   jax.ShapeDtypeStruct((groups, h_dim, lc, 2 * LANES), BF16),
                   jax.ShapeDtypeStruct((groups, 4 * p_dim, h_dim * lc), BF16),
                   jax.ShapeDtypeStruct((groups, 4 * SCAN_LEVELS, LANES), F32)],
        scratch_shapes=[pltpu.VMEM((2, h_dim * h_dim, LANES), F32)],
        compiler_params=_params("arbitrary"),
        name="ssm_ops",
    )(arow, bt, c2, ct)


def _decay_mul(x, dec_ref, d, level):
    row = (d * SCAN_LEVELS + level) * 2
    return x * dec_ref[row:row + 1, :] + pltpu.roll(x, CHUNK, 1) * dec_ref[row + 1:row + 2, :]


def _pair_scan(z, dec_ref, d, n_pairs):
    m = z.shape[0]
    cidx = lax.broadcasted_iota(jnp.int32, (m, 1), 0) & (n_pairs - 1)

    def shifted(x, sh):
        if d == 0:
            return jnp.where(cidx >= sh, pltpu.roll(x, sh, 0), 0.0)
        return jnp.where(cidx < n_pairs - sh, pltpu.roll(x, m - sh, 0), 0.0)

    x = z
    for k in range(n_pairs.bit_length() - 1):
        x = x + _decay_mul(shifted(x, 1 << k), dec_ref, d, k + 1)
    return shifted(x, 1)


def _lane_halves(a, b, left):
    ax = a.ndim - 1
    return jnp.where(left, a, pltpu.roll(b, CHUNK, ax)), jnp.where(left, pltpu.roll(a, CHUNK, ax), b)


def _ssm_apply_kernel(*refs, n_seq, n_chunks):
    u_refs = refs[:n_seq]
    w_ref, f_ref, e_ref, dec_ref = refs[n_seq:n_seq + 4]
    y_refs = refs[n_seq + 4:]
    for u_ref, y_ref, nc in zip(u_refs, y_refs, n_chunks):
        half = u_ref.shape[1]
        u = u_ref[...].reshape(2 * half, u_ref.shape[2])
        z = jnp.dot(u, f_ref[...], preferred_element_type=F32)
        ze, zo = z[:half], z[half:]
        pf = _pair_scan(_decay_mul(ze[:, :LANES], dec_ref, 0, 0) + zo[:, :LANES], dec_ref, 0, nc // 2)
        sf = jnp.concatenate([pf, _decay_mul(pf, dec_ref, 0, 0) + ze[:, :LANES]], axis=0)
        pb = _pair_scan(ze[:, LANES:] + _decay_mul(zo[:, LANES:], dec_ref, 1, 0), dec_ref, 1, nc // 2)
        sb = jnp.concatenate([_decay_mul(pb, dec_ref, 1, 0) + zo[:, LANES:], pb], axis=0)
        s = jnp.concatenate([sf, sb], axis=1)
        acc = jnp.dot(u, w_ref[...], preferred_element_type=F32)
        acc = acc + jnp.dot(s.astype(BF16), e_ref[...], preferred_element_type=F32)
        y_ref[...] = acc.astype(BF16).reshape(y_ref.shape)


def _slot_channel(slot, h_dim):
    return (slot >> 1) + (h_dim // 2) * (slot & 1)


def _to_chunks_kernel(x_ref, o_ref, *, groups, h_dim):
    tk = o_ref.shape[2]
    hh = h_dim // 2
    left = lax.broadcasted_iota(jnp.int32, (1, 1, LANES), 2) < CHUNK
    x = x_ref[...].astype(F32).reshape(groups, 2, hh, tk * LANES)
    ev, od = [], []
    for k in range(tk):
        e, o = _lane_halves(x[:, 0, :, k * LANES:(k + 1) * LANES], x[:, 1, :, k * LANES:(k + 1) * LANES], left)
        ev.append(e)
        od.append(o)
    for p, parts in enumerate((ev, od)):
        o_ref[:, p] = jnp.stack(parts, axis=1).reshape(groups, tk, hh * LANES).astype(o_ref.dtype)


def _from_chunks_kernel(x_ref, o_ref, *, groups, h_dim):
    tk = x_ref.shape[2]
    hh = h_dim // 2
    left = lax.broadcasted_iota(jnp.int32, (1, 1, LANES), 2) < CHUNK
    ye = x_ref[:, 0].astype(F32).reshape(groups, tk, hh, LANES)
    yo = x_ref[:, 1].astype(F32).reshape(groups, tk, hh, LANES)
    lo, hi = zip(*[_lane_halves(ye[:, k], yo[:, k], left) for k in range(tk)])
    y = jnp.stack([jnp.concatenate(lo, axis=-1), jnp.concatenate(hi, axis=-1)], axis=1)
    o_ref[...] = y.reshape(groups * h_dim, tk * LANES).astype(o_ref.dtype)


def _to_chunks(xt, *, groups, h_dim, tk):
    t = xt.shape[1]
    return pl.pallas_call(
        functools.partial(_to_chunks_kernel, groups=groups, h_dim=h_dim),
        grid=(t // (tk * LANES),),
        in_specs=[pl.BlockSpec((groups * h_dim, tk * LANES), lambda i: (0, i))],
        out_specs=pl.BlockSpec((groups, 2, tk, h_dim * CHUNK), lambda i: (0, 0, i, 0)),
        out_shape=jax.ShapeDtypeStruct((groups, 2, t // LANES, h_dim * CHUNK), BF16),
        compiler_params=_params("arbitrary"),
        name="to_chunks",
    )(xt)


def _from_chunks(y4, *, groups, h_dim, tk):
    t = y4.shape[2] * LANES
    return pl.pallas_call(
        functools.partial(_from_chunks_kernel, groups=groups, h_dim=h_dim),
        grid=(t // (tk * LANES),),
        in_specs=[pl.BlockSpec((groups, 2, tk, h_dim * CHUNK), lambda i: (0, 0, i, 0))],
        out_specs=pl.BlockSpec((groups * h_dim, tk * LANES), lambda i: (0, i)),
        out_shape=jax.ShapeDtypeStruct((groups * h_dim, t), BF16),
        compiler_params=_params("arbitrary"),
        name="from_chunks",
    )(y4)


def _ssm_apply(u4s, w, f, e, dec, *, n_chunks):
    groups = w.shape[0]
    blk = lambda shape: pl.BlockSpec((None,) + shape, lambda g: (g,) + (0,) * len(shape))
    kern = functools.partial(_ssm_apply_kernel, n_seq=len(u4s), n_chunks=tuple(n_chunks))
    return pl.pallas_call(
        kern,
        grid=(groups,),
        in_specs=[blk(u.shape[1:]) for u in u4s]
        + [blk(w.shape[1:]), blk(f.shape[1:]), blk(e.shape[1:]), blk(dec.shape[1:])],
        out_specs=[blk(u.shape[1:]) for u in u4s],
        out_shape=[jax.ShapeDtypeStruct(u.shape, BF16) for u in u4s],
        compiler_params=_params("arbitrary"),
        name="ssm_apply",
    )(*u4s, w, f, e, dec)


def _attn_bias_kernel(rb_ref, out_ref, pair_ref, *, hd):
    hp = pl.program_id(0)
    gw = GRID_W
    cq = lax.broadcasted_iota(jnp.int32, (gw, LANES), 0)
    lane = lax.broadcasted_iota(jnp.int32, (gw, LANES), 1)
    left = lane < gw
    ck = lane & (gw - 1)
    cstart = jnp.clip(cq - WIN_COLS // 2, 0, gw - WIN_COLS)
    valid = (ck >= cstart) & (ck < cstart + WIN_COLS)
    dc = ck - cq + (WIN_COLS - 1)
    n_dr = 2 * WIN_ROWS - 1
    for head in range(2):
        hidx = hp * 2 + head
        for dr in range(n_dr - 1):
            acc = jnp.zeros((gw, LANES), F32)
            for i in range(2 * WIN_COLS - 1):
                coef = jnp.where(left, rb_ref[hidx, dr, i], rb_ref[hidx, dr + 1, i])
                acc = jnp.where(dc == i, coef, acc)
            pair_ref[head, dr] = jnp.where(valid, acc * LOG2E, MASK_VALUE)
    for dr0 in range(WIN_ROWS):
        for head in range(2):
            for wq in range(WIN_ROWS // 2):
                out_ref[dr0, head * gw:(head + 1) * gw, wq * LANES:(wq + 1) * LANES] = pair_ref[head, dr0 + 2 * wq]


def _attn_bias(rel_bias):
    nh = rel_bias.shape[0]
    gw = GRID_W
    return pl.pallas_call(
        functools.partial(_attn_bias_kernel, hd=gw),
        grid=(nh // 2,),
        in_specs=[pl.BlockSpec(memory_space=pltpu.SMEM)],
        out_specs=pl.BlockSpec((None, WIN_ROWS, 2 * gw, WIN_ROWS * gw), lambda h: (h, 0, 0, 0)),
        out_shape=jax.ShapeDtypeStruct((nh // 2, WIN_ROWS, 2 * gw, WIN_ROWS * gw), F32),
        scratch_shapes=[pltpu.VMEM((2, 2 * WIN_ROWS - 2, gw, LANES), F32)],
        compiler_params=_params("arbitrary"),
        name="attn_bias",
    )(rel_bias)


def _attn_kernel(q_ref, k_ref, v_ref, za_ref, bias_ref, o_ref, *, rows, hd):
    gw = GRID_W
    win = WIN_ROWS * gw
    lane = lax.broadcasted_iota(jnp.int32, (1, 2 * hd), 1)
    left = lane < hd

    def rows_body(gi, carry):
        slices, scores, probs = [], [], []
        for i in range(ROW_UNROLL):
            r = gi * ROW_UNROLL + i
            rstart = jnp.clip(r - WIN_ROWS // 2, 0, rows - WIN_ROWS)
            dr0 = rstart - r + (WIN_ROWS - 1)
            qs = pl.ds(pl.multiple_of(r * gw, gw), gw)
            ks = pl.ds(pl.multiple_of(rstart * gw, gw), win)
            slices.append((qs, ks))
            q = q_ref[qs, :]
            zero = jnp.zeros_like(q)
            q2 = jnp.concatenate([jnp.where(left, q, zero), jnp.where(left, zero, q)], axis=0)
            s = lax.dot_general(q2, k_ref[ks, :], (((1,), (1,)), ((), ())), preferred_element_type=F32)
            scores.append(s + bias_ref[dr0])
        for s in scores:
            m = jnp.max(s, axis=-1, keepdims=True)
            p = jnp.exp2(s - m)
            probs.append((p.astype(BF16), jnp.sum(p, axis=-1, keepdims=True)))
        for (qs, ks), (p, l) in zip(slices, probs):
            o2 = jnp.dot(p, v_ref[ks, :], preferred_element_type=F32) / l
            o = jnp.where(left, o2[:gw], o2[gw:]) * za_ref[qs, :].astype(F32)
            o_ref[qs, :] = o.astype(BF16)
        return carry

    lax.fori_loop(0, rows // ROW_UNROLL, rows_body, 0)


def _attention(proj3, bias, *, cols, aw, hd):
    b, l, _ = proj3.shape
    rows = l // GRID_W
    nhp = aw // (2 * hd)
    assert rows % ROW_UNROLL == 0
    col = lambda off: pl.BlockSpec((None, l, 2 * hd), lambda h, i, off=off: (i, 0, off + h))
    kern = functools.partial(_attn_kernel, rows=rows, hd=hd)
    return pl.pallas_call(
        kern,
        grid=(nhp, b),
        in_specs=[col(cols[0]), col(cols[1]), col(cols[2]), col(cols[3]),
                  pl.BlockSpec((None,) + bias.shape[1:], lambda h, i: (h, 0, 0, 0))],
        out_specs=pl.BlockSpec((None, l, 2 * hd), lambda h, i: (i, 0, h)),
        out_shape=jax.ShapeDtypeStruct((b, l, aw), BF16),
        compiler_params=_params("arbitrary", "arbitrary"),
        name="attention",
    )(proj3, proj3, proj3, proj3, bias)


def _gelu_tanh(x):
    c = math.sqrt(2.0 / math.pi)
    return 0.5 * x * (1.0 + jnp.tanh(c * (x + 0.044715 * (x * x * x))))


def _merge_kernel(yt_ref, ut_ref, zt_ref, oa_ref, gs_ref, ga_ref, d_ref, bg_ref, wg_ref, wbs_ref, wba_ref, m_ref):
    tm = m_ref.shape[0]
    sub = tm // MERGE_SPLIT
    parts = [slice(i * sub, (i + 1) * sub) for i in range(MERGE_SPLIT)]
    acts = [_gelu_tanh(yt_ref[:, p].astype(F32) + d_ref[...] * ut_ref[:, p].astype(F32)) for p in parts]
    mas = [jnp.dot(oa_ref[p, :], wba_ref[...], preferred_element_type=F32) for p in parts]
    gates = [jnp.dot(wg_ref[...], a.astype(BF16), preferred_element_type=F32) + bg_ref[...] for a in acts]
    osts = [(a * _sigmoid(g) * zt_ref[:, p].astype(F32)).astype(BF16) for a, g, p in zip(acts, gates, parts)]
    mss = [lax.dot_general(o, wbs_ref[...], (((0,), (0,)), ((), ())), preferred_element_type=F32) for o in osts]
    for p, ms, ma in zip(parts, mss, mas):
        m_ref[p, :] = (gs_ref[p, :].astype(F32) * ms + ga_ref[p, :].astype(F32) * ma).astype(BF16)


def _outproj_kernel(m_ref, x_ref, wo_ref, out_ref):
    out_ref[...] = x_ref[...] + jnp.dot(m_ref[...], wo_ref[...], preferred_element_type=F32)


def _const_spec(shape):
    return pl.BlockSpec(shape, lambda i: (0,) * len(shape), pipeline_mode=pl.Buffered(1))


def _merge(yt, projt, oa, proj, dcol, bgcol, wgt, wbs, wba, *, sw, aw, tm):
    t = oa.shape[0]
    d = wbs.shape[1]
    assert (4 * aw) % d == 0
    n_gs = (4 * aw) // d
    return pl.pallas_call(
        _merge_kernel,
        grid=(t // tm,),
        in_specs=[
            pl.BlockSpec((sw, tm), lambda i: (0, i)),
            pl.BlockSpec((sw, tm), lambda i: (0, i)),
            pl.BlockSpec((sw, tm), lambda i: (1, i)),
            pl.BlockSpec((tm, aw), lambda i: (i, 0)),
            pl.BlockSpec((tm, d), lambda i: (i, n_gs)),
            pl.BlockSpec((tm, d), lambda i: (i, n_gs + 1)),
            _const_spec((sw, 1)), _const_spec((sw, 1)),
            _const_spec(wgt.shape), _const_spec(wbs.shape), _const_spec(wba.shape),
        ],
        out_specs=pl.BlockSpec((tm, d), lambda i: (i, 0)),
        out_shape=jax.ShapeDtypeStruct((t, d), BF16),
        compiler_params=_params("arbitrary"),
        name="merge",
    )(yt, projt, projt, oa, proj, proj, dcol, bgcol, wgt, wbs, wba)


def _outproj(m, x, wo, *, tm):
    t, d = x.shape
    return pl.pallas_call(
        _outproj_kernel,
        grid=(t // tm,),
        in_specs=[pl.BlockSpec((tm, d), lambda i: (i, 0)), pl.BlockSpec((tm, d), lambda i: (i, 0)),
                  _const_spec(wo.shape)],
        out_specs=pl.BlockSpec((tm, d), lambda i: (i, 0)),
        out_shape=jax.ShapeDtypeStruct((t, d), x.dtype),
        compiler_params=_params("arbitrary"),
        name="outproj",
    )(m, x, wo)


def _tile(n, pref):
    t = min(n, pref)
    while n % t:
        t //= 2
    return t


def _layer(xs, norm_g, w_in, a_re, a_im, log_dt, b_re, b_im, c_re, c_im, d_skip, w_glu, b_glu,
           q_g, k_g, rel_bias, w_bs, w_ba, w_out):
    d_model = w_in.shape[0]
    sw = d_skip.shape[0]
    groups, p_dim = a_re.shape[1], a_re.shape[2]
    h_dim = sw // groups
    hd = q_g.shape[0]
    aw = w_ba.shape[0]
    lc = CHUNK
    assert h_dim * lc % LANES == 0 and lc * 2 == LANES and h_dim % 2 == 0

    wt = w_in[:, :2 * sw].T.astype(BF16)
    wr = w_in[:, 2 * sw:].astype(BF16)
    tn = _tile(math.gcd(sw, aw), INPROJ_TN)
    acts_t = ["none"] * (sw // tn) + ["silu"] * (sw // tn)
    acts_r = (["qnorm"] * (aw // tn) + ["knorm"] * (aw // tn) + ["none"] * (aw // tn) + ["silu"] * (aw // tn)
              + ["sigmoid"] * (2 * d_model // tn))
    pair = 2 * hd
    attn_cols = tuple(n * aw // pair for n in range(4))
    g2 = norm_g.reshape(1, d_model).astype(F32)
    dup = lambda v: jnp.concatenate([v, v], axis=-1)
    abase = jnp.stack([a_re, a_im, jnp.broadcast_to(log_dt[..., None], a_re.shape)], axis=1)
    abase = abase.transpose(2, 0, 1, 3).astype(F32)
    arow = dup(abase)[:, :, :, None, :]
    bt = dup(jnp.stack([b_re, b_im], axis=1).transpose(2, 0, 1, 4, 3).astype(F32))
    cc = jnp.stack([c_re, c_im], axis=1).transpose(2, 0, 1, 3, 4).astype(F32)
    c2 = dup(cc)
    ct = cc.transpose(0, 1, 2, 4, 3)
    assert tn % hd == 0
    qg_row = jnp.tile(q_g.astype(F32) * (hd ** -0.5 * LOG2E), tn // hd).reshape(1, tn)
    kg_row = jnp.tile(k_g.astype(F32), tn // hd).reshape(1, tn)
    dcol = d_skip.astype(F32).reshape(sw, 1)
    bgcol = b_glu.astype(F32).reshape(sw, 1)
    wgt = w_glu.T.astype(BF16)
    wbs = w_bs.astype(BF16)
    wba = w_ba.astype(BF16)
    wo = w_out.astype(BF16)

    w_t, f_t, e_t, dec = _ssm_ops(arow, bt, c2, ct, groups=groups, h_dim=h_dim, p_dim=p_dim)
    bias = _attn_bias(rel_bias.astype(F32))

    projts, projs, u3s, ncs = [], [], [], []
    for x in xs:
        b, l, _ = x.shape
        t = b * l
        assert l % (GRID_W * ROW_UNROLL) == 0 and l % (WIN_ROWS * GRID_W) == 0
        assert (l // lc) & (l // lc - 1) == 0 and l // lc <= 2 ** SCAN_LEVELS
        projt, proj = _inproj(x.reshape(t, d_model), g2, wt, wr, qg_row, kg_row, acts_t=acts_t, acts_r=acts_r,
                              tm=_tile(t, INPROJ_TM), tn=tn, hd=hd)
        projts.append(projt)
        projs.append(proj)
        u3s.append(_to_chunks(projt, groups=groups, h_dim=h_dim, tk=_tile(t // LANES, RELAYOUT_TK)))
        ncs.append(l // lc)

    y4s = _ssm_apply(u3s, w_t.reshape(groups, h_dim * lc, h_dim * lc), f_t.reshape(groups, h_dim * lc, 2 * LANES),
                     e_t, dec, n_chunks=ncs)

    outs = []
    for x, projt, proj, y4 in zip(xs, projts, projs, y4s):
        b, l, _ = x.shape
        t = b * l
        yt = _from_chunks(y4, groups=groups, h_dim=h_dim, tk=_tile(t // LANES, RELAYOUT_TK))
        oa = _attention(proj.reshape(b, l, -1), bias, cols=attn_cols, aw=aw, hd=hd).reshape(t, aw)
        m = _merge(yt, projt, oa, proj, dcol, bgcol, wgt, wbs, wba, sw=sw, aw=aw, tm=_tile(t, MERGE_TM))
        out = _outproj(m, x.reshape(t, d_model), wo, tm=_tile(t, OUTPROJ_TM))
        outs.append(out.reshape(b, l, d_model))
    return outs


def kernel(x_prompt, x_sample, norm_g, w_in, ssm_a_re, ssm_a_im, ssm_log_dt, ssm_b_re, ssm_b_im, ssm_c_re, ssm_c_im, ssm_d, w_glu, b_glu, q_norm_g, k_norm_g, rel_bias, w_branch_ssm, w_branch_attn, w_out):
    xs = [x_prompt, x_sample]
    for layer in range(norm_g.shape[0]):
        xs = _layer(xs, norm_g[layer], w_in[layer], ssm_a_re[layer], ssm_a_im[layer], ssm_log_dt[layer],
                    ssm_b_re[layer], ssm_b_im[layer], ssm_c_re[layer], ssm_c_im[layer], ssm_d[layer],
                    w_glu[layer], b_glu[layer], q_norm_g[layer], k_norm_g[layer], rel_bias[layer],
                    w_branch_ssm[layer], w_branch_attn[layer], w_out[layer])
    return (xs[0], xs[1])
```

```python
import functools
import math

import numpy as np
import jax
import jax.numpy as jnp
from jax import lax
from jax.experimental import pallas as pl
from jax.experimental.pallas import tpu as pltpu

NORM_EPS = 1e-6
MASK_VALUE = -1e30
LOG2E = math.log2(math.e)
GRID_W = 64
WIN_ROWS = 8
WIN_COLS = 16
CHUNK = 64
LANES = 128
ROW_UNROLL = 8
NORM_SPLIT = 4
MERGE_SPLIT = 2
SSM_GROUPS_PER_STEP = 2
SCAN_LEVELS = 7
VMEM_LIMIT_BYTES = 56 * 1024 * 1024
INPROJ_TM = 1024
INPROJ_TN = 1024
MERGE_TM = 512
OUTPROJ_TM = 512
RELAYOUT_TK = 8

F32 = jnp.float32
BF16 = jnp.bfloat16


def _params(*sem):
    return pltpu.CompilerParams(dimension_semantics=sem, vmem_limit_bytes=VMEM_LIMIT_BYTES)


def _sigmoid(x):
    return 0.5 * jnp.tanh(0.5 * x) + 0.5


def _head_norm(r, gain, hd):
    lane = lax.broadcasted_iota(jnp.int32, (1, 2 * hd), 1)
    left = lane < hd
    out = []
    for c in range(r.shape[1] // (2 * hd)):
        x = r[:, c * 2 * hd:(c + 1) * 2 * hd]
        x2 = x * x
        sa = jnp.sum(jnp.where(left, x2, 0.0), axis=-1, keepdims=True)
        sb = jnp.sum(jnp.where(left, 0.0, x2), axis=-1, keepdims=True)
        ms = jnp.where(left, sa, sb) * (1.0 / hd)
        out.append(x * lax.rsqrt(ms + NORM_EPS) * gain[:, c * 2 * hd:(c + 1) * 2 * hd])
    return jnp.concatenate(out, axis=1)


def _activate(r, act, gains, hd):
    if act == "silu":
        return r * _sigmoid(r)
    if act == "sigmoid":
        return _sigmoid(r)
    if act in gains:
        return _head_norm(r, gains[act][...], hd)
    return r


ACTS = ("none", "silu", "sigmoid", "qnorm", "knorm")


def _inproj_kernel(x_ref, g_ref, wt_ref, w_ref, qg_ref, kg_ref, outt_ref, out_ref, h_ref, *, acts_t, acts_r, hd):
    j = pl.program_id(1)
    n_t = len(acts_t)
    gains = {"qnorm": qg_ref, "knorm": kg_ref}

    @pl.when(j == 0)
    def _():
        x = x_ref[...]
        ms = jnp.mean(x * x, axis=-1, keepdims=True)
        h_ref[...] = (x * lax.rsqrt(ms + NORM_EPS) * g_ref[...]).astype(BF16)

    def tiles_with(acts, act, base):
        cond = None
        for idx, a in enumerate(acts):
            if a == act:
                c = j == base + idx
                cond = c if cond is None else cond | c
        return cond

    for act in ACTS:
        cond = tiles_with(acts_t, act, 0)
        if cond is not None:
            assert act not in gains
            @pl.when(cond)
            def _(act=act):
                r = lax.dot_general(wt_ref[...], h_ref[...], (((1,), (1,)), ((), ())), preferred_element_type=F32)
                outt_ref[...] = _activate(r, act, gains, hd).astype(BF16)
        cond = tiles_with(acts_r, act, n_t)
        if cond is not None:
            @pl.when(cond)
            def _(act=act):
                n_sub = NORM_SPLIT if act in gains else 1
                sub = h_ref.shape[0] // n_sub
                for p in range(n_sub):
                    rows = slice(p * sub, (p + 1) * sub)
                    r = jnp.dot(h_ref[rows, :], w_ref[...], preferred_element_type=F32)
                    out_ref[rows, :] = _activate(r, act, gains, hd).astype(BF16)


def _inproj(x, g, wt, w, qg, kg, *, acts_t, acts_r, tm, tn, hd):
    t, d = x.shape
    n_t, n_r = len(acts_t), len(acts_r)
    assert wt.shape == (n_t * tn, d) and w.shape == (d, n_r * tn) and set(acts_t) | set(acts_r) <= set(ACTS)
    kern = functools.partial(_inproj_kernel, acts_t=tuple(acts_t), acts_r=tuple(acts_r), hd=hd)
    return pl.pallas_call(
        kern,
        grid=(t // tm, n_t + n_r),
        in_specs=[
            pl.BlockSpec((tm, d), lambda i, j: (i, 0)),
            pl.BlockSpec((1, d), lambda i, j: (0, 0)),
            pl.BlockSpec((tn, d), lambda i, j: (jnp.minimum(j, n_t - 1), 0)),
            pl.BlockSpec((d, tn), lambda i, j: (0, jnp.maximum(j - n_t, 0))),
            pl.BlockSpec((1, tn), lambda i, j: (0, 0)),
            pl.BlockSpec((1, tn), lambda i, j: (0, 0)),
        ],
        out_specs=[
            pl.BlockSpec((tn, tm), lambda i, j: (jnp.minimum(j, n_t - 1), i)),
            pl.BlockSpec((tm, tn), lambda i, j: (i, jnp.maximum(j - n_t, 0))),
        ],
        out_shape=[jax.ShapeDtypeStruct((n_t * tn, t), BF16), jax.ShapeDtypeStruct((t, n_r * tn), BF16)],
        scratch_shapes=[pltpu.VMEM((tm, d), BF16)],
        compiler_params=_params("arbitrary", "arbitrary"),
        name="inproj",
    )(x, g, wt, w, qg, kg)


def _cmul(ar, ai, br, bi):
    return ar * br - ai * bi, ar * bi + ai * br


def _hi_lo(x):
    if x.dtype == BF16:
        return x, None
    hi = x.astype(BF16)
    return hi, (x - hi.astype(F32)).astype(BF16)


def _dot_hi_lo(a, b):
    ah, al = _hi_lo(a)
    bh, bl = _hi_lo(b)
    acc = jnp.dot(ah, bh, preferred_element_type=F32)
    if bl is not None:
        acc = acc + jnp.dot(ah, bl, preferred_element_type=F32)
    if al is not None:
        acc = acc + jnp.dot(al, bh, preferred_element_type=F32)
    return acc


def _discretize(a_re, a_im, log_dt):
    lam_re = jnp.minimum(a_re, -1e-4)
    lam_im = a_im
    dt = jnp.exp(log_dt)
    mag = jnp.exp(lam_re * dt)
    abar_re = mag * jnp.cos(lam_im * dt)
    abar_im = mag * jnp.sin(lam_im * dt)
    den = lam_re * lam_re + lam_im * lam_im
    f_re = ((abar_re - 1.0) * lam_re + abar_im * lam_im) / den
    f_im = (abar_im * lam_re - (abar_re - 1.0) * lam_im) / den
    return abar_re, abar_im, f_re, f_im


def _squarings(ar, ai, n):
    out = [(ar, ai)]
    for _ in range(n - 1):
        ar, ai = _cmul(ar, ai, ar, ai)
        out.append((ar, ai))
    return out


def _power_table(sq, exps, shape):
    pr = jnp.ones(shape, F32)
    pi = jnp.zeros(shape, F32)
    for k, (ar, ai) in enumerate(sq):
        bit = ((exps >> k) & 1) == 1
        fr = jnp.where(bit, ar, 1.0)
        fi = jnp.where(bit, ai, 0.0)
        pr, pi = _cmul(pr, pi, fr, fi)
    return pr, pi


def _ssm_ops_kernel(arow_ref, bt_ref, c_ref, ct_ref, w_ref, f_ref, e_ref, dec_ref, k_ref, *, h_dim, p_dim):
    lc = CHUNK
    nbits = lc.bit_length()
    lane = lax.broadcasted_iota(jnp.int32, (1, LANES), 1)
    left = lane < lc
    sub = lax.broadcasted_iota(jnp.int32, (lc, 1), 0)
    eye = lax.broadcasted_iota(jnp.int32, (p_dim, 1), 0) == lane

    q2 = []
    rmat = []
    bb_rows = []
    sq_rows = []
    for d in range(2):
        abr, abi, fr, fi = _discretize(arow_ref[d, 0], arow_ref[d, 1], arow_ref[d, 2])
        bbr, bbi = _cmul(fr, fi, bt_ref[d, 0], bt_ref[d, 1])
        bb_rows.append((bbr, bbi))
        sq_row = _squarings(abr, abi, nbits + SCAN_LEVELS)
        sq_rows.append(sq_row)
        cr, ci = c_ref[d, 0], c_ref[d, 1]
        blocks = []
        for h in range(h_dim):
            qr, qi = _cmul(cr, ci, bbr[h:h + 1, :], bbi[h:h + 1, :])
            blocks.append(jnp.where(left, qr, -qi))
        q2.append(jnp.concatenate(blocks, axis=0))
        cabr = jnp.sum(jnp.where(eye, abr, 0.0), axis=-1, keepdims=True)
        cabi = jnp.sum(jnp.where(eye, abi, 0.0), axis=-1, keepdims=True)
        sq_col = _squarings(cabr, cabi, nbits)
        lag = lane - lc
        if d == 0:
            exps, live = jnp.maximum(lag, 0), lag >= 0
        else:
            exps, live = jnp.maximum(-lag, 0), lag <= 0
        vr, vi = _power_table(sq_col, exps, (p_dim, LANES))
        rmat.append(jnp.concatenate([jnp.where(live, vr, 0.0), jnp.where(live, vi, 0.0)], axis=0))
        if d == 0:
            er, ei = _cmul(jnp.where(left, pltpu.roll(vr, lc, 1), vr), jnp.where(left, pltpu.roll(vi, lc, 1), vi),
                           cabr, cabi)
        else:
            er, ei = jnp.where(left, vr, pltpu.roll(vr, lc, 1)), jnp.where(left, vi, pltpu.roll(vi, lc, 1))
        n_rep = (h_dim * lc) // LANES
        er = jnp.concatenate([er] * n_rep, axis=1)
        ei = jnp.concatenate([ei] * n_rep, axis=1)
        rep_r = lax.broadcasted_iota(jnp.int32, (h_dim, h_dim * lc), 0)
        rep_c = lax.broadcasted_iota(jnp.int32, (h_dim, h_dim * lc), 1)
        rep = (_slot_channel(rep_c >> (lc.bit_length() - 1), h_dim) == rep_r).astype(BF16)
        cer = _dot_hi_lo(ct_ref[d, 0], rep)
        cei = _dot_hi_lo(ct_ref[d, 1], rep)
        gr, gi = _cmul(cer, cei, er, ei)
        e_ref[2 * d * p_dim:(2 * d + 1) * p_dim, :] = gr.astype(BF16)
        e_ref[(2 * d + 1) * p_dim:(2 * d + 2) * p_dim, :] = (-gi).astype(BF16)
        for k in range(SCAN_LEVELS):
            dr_, di_ = sq_row[nbits - 1 + k]
            dec_ref[(d * SCAN_LEVELS + k) * 2:(d * SCAN_LEVELS + k) * 2 + 1, :] = dr_
            dec_ref[(d * SCAN_LEVELS + k) * 2 + 1:(d * SCAN_LEVELS + k) * 2 + 2, :] = jnp.where(left, -di_, di_)

    kbi = _dot_hi_lo(q2[0], rmat[0]) + _dot_hi_lo(q2[1], rmat[1])
    k_ref[0] = kbi
    k_ref[1] = pltpu.roll(kbi, lc, 1)

    fexp = (lc - 1 - sub, sub)
    vrow = [_power_table(sq_rows[d][:nbits], fexp[d], (lc, LANES)) for d in range(2)]

    hh = h_dim // 2
    pre_roll_left = ((lane + sub) & (LANES - 1)) < lc

    def per_h(h, carry):
        slot = 2 * (h % hh) + h // hh
        for d in range(2):
            bbr, bbi = bb_rows[d]
            sel = (lax.broadcasted_iota(jnp.int32, (h_dim, 1), 0) == h).astype(F32)
            br = jnp.sum(bbr * sel, axis=0, keepdims=True)
            bi = jnp.sum(bbi * sel, axis=0, keepdims=True)
            zr, zi = _cmul(vrow[d][0], vrow[d][1], br, bi)
            f_ref[slot, :, d * LANES:(d + 1) * LANES] = jnp.where(left, zr, zi).astype(BF16)
        for m in range(hh):
            ra = k_ref[1, pl.ds(h * h_dim + m, 1), :]
            rb = k_ref[0, pl.ds(h * h_dim + m + hh, 1), :]
            src = jnp.where(pre_roll_left, jnp.broadcast_to(ra, (lc, LANES)), jnp.broadcast_to(rb, (lc, LANES)))
            tile = pltpu.roll(src, 0, 1, stride=1, stride_axis=0)
            w_ref[slot, :, m * LANES:(m + 1) * LANES] = tile.astype(BF16)
        return carry

    lax.fori_loop(0, h_dim, per_h, 0, unroll=4)


def _ssm_ops(arow, bt, c2, ct, *, groups, h_dim, p_dim):
    lc = CHUNK
    kern = functools.partial(_ssm_ops_kernel, h_dim=h_dim, p_dim=p_dim)
    blk = lambda shape: pl.BlockSpec((None,) + shape, lambda g: (g,) + (0,) * len(shape))
    return pl.pallas_call(
        kern,
        grid=(groups,),
        in_specs=[blk((2, 3, 1, LANES)), blk((2, 2, h_dim, LANES)),
                  blk((2, 2, h_dim, LANES)), blk((2, 2, p_dim, h_dim))],
        out_specs=[blk((h_dim, lc, h_dim * lc)), blk((h_dim, lc, 2 * LANES)),
                   blk((4 * p_dim, h_dim * lc)), blk((4 * SCAN_LEVELS, LANES))],
        out_shape=[jax.ShapeDtypeStruct((groups, h_dim, lc, h_dim * lc), BF16),
                   jax.ShapeDtypeStruct((groups, h_dim, lc, 2 * LANES), BF16),
                   jax.ShapeDtypeStruct((groups, 4 * p_dim, h_dim * lc), BF16),
                   jax.ShapeDtypeStruct((groups, 4 * SCAN_LEVELS, LANES), F32)],
        scratch_shapes=[pltpu.VMEM((2, h_dim * h_dim, LANES), F32)],
        compiler_params=_params("arbitrary"),
        name="ssm_ops",
    )(arow, bt, c2, ct)


def _decay_mul(x, dec_ref, d, level):
    row = (d * SCAN_LEVELS + level) * 2
    return x * dec_ref[row:row + 1, :] + pltpu.roll(x, CHUNK, 1) * dec_ref[row + 1:row + 2, :]


def _pair_scan(z, dec_ref, d, n_pairs):
    m = z.shape[0]
    cidx = lax.broadcasted_iota(jnp.int32, (m, 1), 0) & (n_pairs - 1)

    def shifted(x, sh):
        if d == 0:
            return jnp.where(cidx >= sh, pltpu.roll(x, sh, 0), 0.0)
        return jnp.where(cidx < n_pairs - sh, pltpu.roll(x, m - sh, 0), 0.0)

    x = z
    for k in range(n_pairs.bit_length() - 1):
        x = x + _decay_mul(shifted(x, 1 << k), dec_ref, d, k + 1)
    return shifted(x, 1)


def _lane_halves(a, b, left):
    ax = a.ndim - 1
    return jnp.where(left, a, pltpu.roll(b, CHUNK, ax)), jnp.where(left, pltpu.roll(a, CHUNK, ax), b)


def _ssm_apply_kernel(*refs, n_seq, n_chunks):
    u_refs = refs[:n_seq]
    w_ref, f_ref, e_ref, dec_ref = refs[n_seq:n_seq + 4]
    y_refs = refs[n_seq + 4:]
    units = [(g, i) for g in range(SSM_GROUPS_PER_STEP) for i in range(n_seq)]
    us = [u_refs[i][g].reshape(2 * u_refs[i].shape[2], u_refs[i].shape[3]) for g, i in units]
    zs = [jnp.dot(u, f_ref[g], preferred_element_type=F32) for u, (g, _) in zip(us, units)]
    accs = [jnp.dot(u, w_ref[g], preferred_element_type=F32) for u, (g, _) in zip(us, units)]
    states = []
    for z, (g, i) in zip(zs, units):
        dec = dec_ref.at[g]
        nc = n_chunks[i]
        half = z.shape[0] // 2
        ze, zo = z[:half], z[half:]
        pf = _pair_scan(_decay_mul(ze[:, :LANES], dec, 0, 0) + zo[:, :LANES], dec, 0, nc // 2)
        sf = jnp.concatenate([pf, _decay_mul(pf, dec, 0, 0) + ze[:, :LANES]], axis=0)
        pb = _pair_scan(ze[:, LANES:] + _decay_mul(zo[:, LANES:], dec, 1, 0), dec, 1, nc // 2)
        sb = jnp.concatenate([_decay_mul(pb, dec, 1, 0) + zo[:, LANES:], pb], axis=0)
        states.append(jnp.concatenate([sf, sb], axis=1).astype(BF16))
    for (g, i), acc, s in zip(units, accs, states):
        acc = acc + jnp.dot(s, e_ref[g], preferred_element_type=F32)
        y_refs[i][g] = acc.astype(BF16).reshape(y_refs[i].shape[1:])


def _slot_channel(slot, h_dim):
    return (slot >> 1) + (h_dim // 2) * (slot & 1)


def _to_chunks_kernel(x_ref, o_ref, *, groups, h_dim):
    tk = o_ref.shape[2]
    hh = h_dim // 2
    left = lax.broadcasted_iota(jnp.int32, (1, 1, LANES), 2) < CHUNK
    x = x_ref[...].astype(F32).reshape(groups, 2, hh, tk * LANES)
    ev, od = [], []
    for k in range(tk):
        e, o = _lane_halves(x[:, 0, :, k * LANES:(k + 1) * LANES], x[:, 1, :, k * LANES:(k + 1) * LANES], left)
        ev.append(e)
        od.append(o)
    for p, parts in enumerate((ev, od)):
        o_ref[:, p] = jnp.stack(parts, axis=1).reshape(groups, tk, hh * LANES).astype(o_ref.dtype)


def _from_chunks_kernel(x_ref, o_ref, *, groups, h_dim):
    tk = x_ref.shape[2]
    hh = h_dim // 2
    left = lax.broadcasted_iota(jnp.int32, (1, 1, LANES), 2) < CHUNK
    ye = x_ref[:, 0].astype(F32).reshape(groups, tk, hh, LANES)
    yo = x_ref[:, 1].astype(F32).reshape(groups, tk, hh, LANES)
    lo, hi = zip(*[_lane_halves(ye[:, k], yo[:, k], left) for k in range(tk)])
    y = jnp.stack([jnp.concatenate(lo, axis=-1), jnp.concatenate(hi, axis=-1)], axis=1)
    o_ref[...] = y.reshape(groups * h_dim, tk * LANES).astype(o_ref.dtype)


def _to_chunks(xt, *, groups, h_dim, tk):
    t = xt.shape[1]
    return pl.pallas_call(
        functools.partial(_to_chunks_kernel, groups=groups, h_dim=h_dim),
        grid=(t // (tk * LANES),),
        in_specs=[pl.BlockSpec((groups * h_dim, tk * LANES), lambda i: (0, i))],
        out_specs=pl.BlockSpec((groups, 2, tk, h_dim * CHUNK), lambda i: (0, 0, i, 0)),
        out_shape=jax.ShapeDtypeStruct((groups, 2, t // LANES, h_dim * CHUNK), BF16),
        compiler_params=_params("arbitrary"),
        name="to_chunks",
    )(xt)


def _from_chunks(y4, *, groups, h_dim, tk):
    t = y4.shape[2] * LANES
    return pl.pallas_call(
        functools.partial(_from_chunks_kernel, groups=groups, h_dim=h_dim),
        grid=(t // (tk * LANES),),
        in_specs=[pl.BlockSpec((groups, 2, tk, h_dim * CHUNK), lambda i: (0, 0, i, 0))],
        out_specs=pl.BlockSpec((groups * h_dim, tk * LANES), lambda i: (0, i)),
        out_shape=jax.ShapeDtypeStruct((groups * h_dim, t), BF16),
        compiler_params=_params("arbitrary"),
        name="from_chunks",
    )(y4)


def _ssm_apply(u4s, w, f, e, dec, *, n_chunks):
    groups = w.shape[0]
    assert groups % SSM_GROUPS_PER_STEP == 0
    blk = lambda shape: pl.BlockSpec((SSM_GROUPS_PER_STEP,) + shape, lambda g: (g,) + (0,) * len(shape))
    kern = functools.partial(_ssm_apply_kernel, n_seq=len(u4s), n_chunks=tuple(n_chunks))
    return pl.pallas_call(
        kern,
        grid=(groups // SSM_GROUPS_PER_STEP,),
        in_specs=[blk(u.shape[1:]) for u in u4s]
        + [blk(w.shape[1:]), blk(f.shape[1:]), blk(e.shape[1:]), blk(dec.shape[1:])],
        out_specs=[blk(u.shape[1:]) for u in u4s],
        out_shape=[jax.ShapeDtypeStruct(u.shape, BF16) for u in u4s],
        compiler_params=_params("arbitrary"),
        name="ssm_apply",
    )(*u4s, w, f, e, dec)


def _attn_bias_kernel(rb_ref, out_ref, pair_ref, *, hd):
    hp = pl.program_id(0)
    gw = GRID_W
    cq = lax.broadcasted_iota(jnp.int32, (gw, LANES), 0)
    lane = lax.broadcasted_iota(jnp.int32, (gw, LANES), 1)
    left = lane < gw
    ck = lane & (gw - 1)
    cstart = jnp.clip(cq - WIN_COLS // 2, 0, gw - WIN_COLS)
    valid = (ck >= cstart) & (ck < cstart + WIN_COLS)
    dc = ck - cq + (WIN_COLS - 1)
    n_dr = 2 * WIN_ROWS - 1
    for head in range(2):
        hidx = hp * 2 + head
        for dr in range(n_dr - 1):
            acc = jnp.zeros((gw, LANES), F32)
            for i in range(2 * WIN_COLS - 1):
                coef = jnp.where(left, rb_ref[hidx, dr, i], rb_ref[hidx, dr + 1, i])
                acc = jnp.where(dc == i, coef, acc)
            pair_ref[head, dr] = jnp.where(valid, acc * LOG2E, MASK_VALUE)
    for dr0 in range(WIN_ROWS):
        for head in range(2):
            for wq in range(WIN_ROWS // 2):
                out_ref[dr0, head * gw:(head + 1) * gw, wq * LANES:(wq + 1) * LANES] = pair_ref[head, dr0 + 2 * wq]


def _attn_bias(rel_bias):
    nh = rel_bias.shape[0]
    gw = GRID_W
    return pl.pallas_call(
        functools.partial(_attn_bias_kernel, hd=gw),
        grid=(nh // 2,),
        in_specs=[pl.BlockSpec(memory_space=pltpu.SMEM)],
        out_specs=pl.BlockSpec((None, WIN_ROWS, 2 * gw, WIN_ROWS * gw), lambda h: (h, 0, 0, 0)),
        out_shape=jax.ShapeDtypeStruct((nh // 2, WIN_ROWS, 2 * gw, WIN_ROWS * gw), F32),
        scratch_shapes=[pltpu.VMEM((2, 2 * WIN_ROWS - 2, gw, LANES), F32)],
        compiler_params=_params("arbitrary"),
        name="attn_bias",
    )(rel_bias)


def _attn_kernel(q_ref, k_ref, v_ref, za_ref, bias_ref, o_ref, *, rows, hd):
    gw = GRID_W
    win = WIN_ROWS * gw
    lane = lax.broadcasted_iota(jnp.int32, (1, 2 * hd), 1)
    left = lane < hd

    def rows_body(gi, carry):
        slices, scores, probs = [], [], []
        for i in range(ROW_UNROLL):
            r = gi * ROW_UNROLL + i
            rstart = jnp.clip(r - WIN_ROWS // 2, 0, rows - WIN_ROWS)
            dr0 = rstart - r + (WIN_ROWS - 1)
            qs = pl.ds(pl.multiple_of(r * gw, gw), gw)
            ks = pl.ds(pl.multiple_of(rstart * gw, gw), win)
            slices.append((qs, ks))
            q = q_ref[qs, :]
            zero = jnp.zeros_like(q)
            q2 = jnp.concatenate([jnp.where(left, q, zero), jnp.where(left, zero, q)], axis=0)
            s = lax.dot_general(q2, k_ref[ks, :], (((1,), (1,)), ((), ())), preferred_element_type=F32)
            scores.append(s + bias_ref[dr0])
        for s in scores:
            m = jnp.max(s, axis=-1, keepdims=True)
            p = jnp.exp2(s - m)
            probs.append((p.astype(BF16), jnp.sum(p, axis=-1, keepdims=True)))
        for (qs, ks), (p, l) in zip(slices, probs):
            o2 = jnp.dot(p, v_ref[ks, :], preferred_element_type=F32) / l
            o = jnp.where(left, o2[:gw], o2[gw:]) * za_ref[qs, :].astype(F32)
            o_ref[qs, :] = o.astype(BF16)
        return carry

    lax.fori_loop(0, rows // ROW_UNROLL, rows_body, 0)


def _attention(proj3, bias, *, cols, aw, hd):
    b, l, _ = proj3.shape
    rows = l // GRID_W
    nhp = aw // (2 * hd)
    assert rows % ROW_UNROLL == 0
    col = lambda off: pl.BlockSpec((None, l, 2 * hd), lambda h, i, off=off: (i, 0, off + h))
    kern = functools.partial(_attn_kernel, rows=rows, hd=hd)
    return pl.pallas_call(
        kern,
        grid=(nhp, b),
        in_specs=[col(cols[0]), col(cols[1]), col(cols[2]), col(cols[3]),
                  pl.BlockSpec((None,) + bias.shape[1:], lambda h, i: (h, 0, 0, 0))],
        out_specs=pl.BlockSpec((None, l, 2 * hd), lambda h, i: (i, 0, h)),
        out_shape=jax.ShapeDtypeStruct((b, l, aw), BF16),
        compiler_params=_params("arbitrary", "arbitrary"),
        name="attention",
    )(proj3, proj3, proj3, proj3, bias)


def _gelu_tanh(x):
    c = math.sqrt(2.0 / math.pi)
    return 0.5 * x * (1.0 + jnp.tanh(c * (x + 0.044715 * (x * x * x))))


def _merge_kernel(yt_ref, ut_ref, zt_ref, oa_ref, gs_ref, ga_ref, d_ref, bg_ref, wg_ref, wbs_ref, wba_ref, m_ref):
    tm = m_ref.shape[0]
    sub = tm // MERGE_SPLIT
    parts = [slice(i * sub, (i + 1) * sub) for i in range(MERGE_SPLIT)]
    acts = [_gelu_tanh(yt_ref[:, p].astype(F32) + d_ref[...] * ut_ref[:, p].astype(F32)) for p in parts]
    mas = [jnp.dot(oa_ref[p, :], wba_ref[...], preferred_element_type=F32) for p in parts]
    gates = [jnp.dot(wg_ref[...], a.astype(BF16), preferred_element_type=F32) + bg_ref[...] for a in acts]
    osts = [(a * _sigmoid(g) * zt_ref[:, p].astype(F32)).astype(BF16) for a, g, p in zip(acts, gates, parts)]
    mss = [lax.dot_general(o, wbs_ref[...], (((0,), (0,)), ((), ())), preferred_element_type=F32) for o in osts]
    for p, ms, ma in zip(parts, mss, mas):
        m_ref[p, :] = (gs_ref[p, :].astype(F32) * ms + ga_ref[p, :].astype(F32) * ma).astype(BF16)


def _outproj_kernel(m_ref, x_ref, wo_ref, out_ref):
    out_ref[...] = x_ref[...] + jnp.dot(m_ref[...], wo_ref[...], preferred_element_type=F32)


def _const_spec(shape):
    return pl.BlockSpec(shape, lambda i: (0,) * len(shape), pipeline_mode=pl.Buffered(1))


def _merge(yt, projt, oa, proj, dcol, bgcol, wgt, wbs, wba, *, sw, aw, tm):
    t = oa.shape[0]
    d = wbs.shape[1]
    assert (4 * aw) % d == 0
    n_gs = (4 * aw) // d
    return pl.pallas_call(
        _merge_kernel,
        grid=(t // tm,),
        in_specs=[
            pl.BlockSpec((sw, tm), lambda i: (0, i)),
            pl.BlockSpec((sw, tm), lambda i: (0, i)),
            pl.BlockSpec((sw, tm), lambda i: (1, i)),
            pl.BlockSpec((tm, aw), lambda i: (i, 0)),
            pl.BlockSpec((tm, d), lambda i: (i, n_gs)),
            pl.BlockSpec((tm, d), lambda i: (i, n_gs + 1)),
            _const_spec((sw, 1)), _const_spec((sw, 1)),
            _const_spec(wgt.shape), _const_spec(wbs.shape), _const_spec(wba.shape),
        ],
        out_specs=pl.BlockSpec((tm, d), lambda i: (i, 0)),
        out_shape=jax.ShapeDtypeStruct((t, d), BF16),
        compiler_params=_params("arbitrary"),
        name="merge",
    )(yt, projt, projt, oa, proj, proj, dcol, bgcol, wgt, wbs, wba)


def _outproj(m, x, wo, *, tm):
    t, d = x.shape
    return pl.pallas_call(
        _outproj_kernel,
        grid=(t // tm,),
        in_specs=[pl.BlockSpec((tm, d), lambda i: (i, 0)), pl.BlockSpec((tm, d), lambda i: (i, 0)),
                  _const_spec(wo.shape)],
        out_specs=pl.BlockSpec((tm, d), lambda i: (i, 0)),
        out_shape=jax.ShapeDtypeStruct((t, d), x.dtype),
        compiler_params=_params("arbitrary"),
        name="outproj",
    )(m, x, wo)


def _tile(n, pref):
    t = min(n, pref)
    while n % t:
        t //= 2
    return t


def _layer(xs, norm_g, w_in, a_re, a_im, log_dt, b_re, b_im, c_re, c_im, d_skip, w_glu, b_glu,
           q_g, k_g, rel_bias, w_bs, w_ba, w_out):
    d_model = w_in.shape[0]
    sw = d_skip.shape[0]
    groups, p_dim = a_re.shape[1], a_re.shape[2]
    h_dim = sw // groups
    hd = q_g.shape[0]
    aw = w_ba.shape[0]
    lc = CHUNK
    assert h_dim * lc % LANES == 0 and lc * 2 == LANES and h_dim % 2 == 0

    wt = w_in[:, :2 * sw].T.astype(BF16)
    wr = w_in[:, 2 * sw:].astype(BF16)
    tn = _tile(math.gcd(sw, aw), INPROJ_TN)
    acts_t = ["none"] * (sw // tn) + ["silu"] * (sw // tn)
    acts_r = (["qnorm"] * (aw // tn) + ["knorm"] * (aw // tn) + ["none"] * (aw // tn) + ["silu"] * (aw // tn)
              + ["sigmoid"] * (2 * d_model // tn))
    pair = 2 * hd
    attn_cols = tuple(n * aw // pair for n in range(4))
    g2 = norm_g.reshape(1, d_model).astype(F32)
    dup = lambda v: jnp.concatenate([v, v], axis=-1)
    abase = jnp.stack([a_re, a_im, jnp.broadcast_to(log_dt[..., None], a_re.shape)], axis=1)
    abase = abase.transpose(2, 0, 1, 3).astype(F32)
    arow = dup(abase)[:, :, :, None, :]
    bt = dup(jnp.stack([b_re, b_im], axis=1).transpose(2, 0, 1, 4, 3).astype(F32))
    cc = jnp.stack([c_re, c_im], axis=1).transpose(2, 0, 1, 3, 4).astype(F32)
    c2 = dup(cc)
    ct = cc.transpose(0, 1, 2, 4, 3)
    assert tn % hd == 0
    qg_row = jnp.tile(q_g.astype(F32) * (hd ** -0.5 * LOG2E), tn // hd).reshape(1, tn)
    kg_row = jnp.tile(k_g.astype(F32), tn // hd).reshape(1, tn)
    dcol = d_skip.astype(F32).reshape(sw, 1)
    bgcol = b_glu.astype(F32).reshape(sw, 1)
    wgt = w_glu.T.astype(BF16)
    wbs = w_bs.astype(BF16)
    wba = w_ba.astype(BF16)
    wo = w_out.astype(BF16)

    w_t, f_t, e_t, dec = _ssm_ops(arow, bt, c2, ct, groups=groups, h_dim=h_dim, p_dim=p_dim)
    bias = _attn_bias(rel_bias.astype(F32))

    projts, projs, u3s, ncs = [], [], [], []
    for x in xs:
        b, l, _ = x.shape
        t = b * l
        assert l % (GRID_W * ROW_UNROLL) == 0 and l % (WIN_ROWS * GRID_W) == 0
        assert (l // lc) & (l // lc - 1) == 0 and l // lc <= 2 ** SCAN_LEVELS
        projt, proj = _inproj(x.reshape(t, d_model), g2, wt, wr, qg_row, kg_row, acts_t=acts_t, acts_r=acts_r,
                              tm=_tile(t, INPROJ_TM), tn=tn, hd=hd)
        projts.append(projt)
        projs.append(proj)
        u3s.append(_to_chunks(projt, groups=groups, h_dim=h_dim, tk=_tile(t // LANES, RELAYOUT_TK)))
        ncs.append(l // lc)

    y4s = _ssm_apply(u3s, w_t.reshape(groups, h_dim * lc, h_dim * lc), f_t.reshape(groups, h_dim * lc, 2 * LANES),
                     e_t, dec, n_chunks=ncs)

    outs = []
    for x, projt, proj, y4 in zip(xs, projts, projs, y4s):
        b, l, _ = x.shape
        t = b * l
        yt = _from_chunks(y4, groups=groups, h_dim=h_dim, tk=_tile(t // LANES, RELAYOUT_TK))
        oa = _attention(proj.reshape(b, l, -1), bias, cols=attn_cols, aw=aw, hd=hd).reshape(t, aw)
        m = _merge(yt, projt, oa, proj, dcol, bgcol, wgt, wbs, wba, sw=sw, aw=aw, tm=_tile(t, MERGE_TM))
        out = _outproj(m, x.reshape(t, d_model), wo, tm=_tile(t, OUTPROJ_TM))
        outs.append(out.reshape(b, l, d_model))
    return outs


def kernel(x_prompt, x_sample, norm_g, w_in, ssm_a_re, ssm_a_im, ssm_log_dt, ssm_b_re, ssm_b_im, ssm_c_re, ssm_c_im, ssm_d, w_glu, b_glu, q_norm_g, k_norm_g, rel_bias, w_branch_ssm, w_branch_attn, w_out):
    xs = [x_prompt, x_sample]
    for layer in range(norm_g.shape[0]):
        xs = _layer(xs, norm_g[layer], w_in[layer], ssm_a_re[layer], ssm_a_im[layer], ssm_log_dt[layer],
                    ssm_b_re[layer], ssm_b_im[layer], ssm_c_re[layer], ssm_c_im[layer], ssm_d[layer],
                    w_glu[layer], b_glu[layer], q_norm_g[layer], k_norm_g[layer], rel_bias[layer],
                    w_branch_ssm[layer], w_branch_attn[layer], w_out[layer])
    return (xs[0], xs[1])
```

```python
import functools
import math

import numpy as np
import jax
import jax.numpy as jnp
from jax import lax
from jax.experimental import pallas as pl
from jax.experimental.pallas import tpu as pltpu

NORM_EPS = 1e-6
MASK_VALUE = -1e30
LOG2E = math.log2(math.e)
GRID_W = 64
WIN_ROWS = 8
WIN_COLS = 16
CHUNK = 64
LANES = 128
ROW_UNROLL = 8
NORM_SPLIT = 4
MERGE_SPLIT = 2
SSM_GROUPS_PER_STEP = 2
SCAN_LEVELS = 7
VMEM_LIMIT_BYTES = 56 * 1024 * 1024
INPROJ_TM = 1024
INPROJ_TN = 1024
MERGE_TM = 512
OUTPROJ_TM = 512
RELAYOUT_TK = 8

F32 = jnp.float32
BF16 = jnp.bfloat16


def _params(*sem):
    return pltpu.CompilerParams(dimension_semantics=sem, vmem_limit_bytes=VMEM_LIMIT_BYTES)


def _sigmoid(x):
    return 0.5 * jnp.tanh(0.5 * x) + 0.5


def _head_norm(r, gain, hd):
    lane = lax.broadcasted_iota(jnp.int32, (1, 2 * hd), 1)
    left = lane < hd
    out = []
    for c in range(r.shape[1] // (2 * hd)):
        x = r[:, c * 2 * hd:(c + 1) * 2 * hd]
        x2 = x * x
        sa = jnp.sum(jnp.where(left, x2, 0.0), axis=-1, keepdims=True)
        sb = jnp.sum(jnp.where(left, 0.0, x2), axis=-1, keepdims=True)
        ms = jnp.where(left, sa, sb) * (1.0 / hd)
        out.append(x * lax.rsqrt(ms + NORM_EPS) * gain[:, c * 2 * hd:(c + 1) * 2 * hd])
    return jnp.concatenate(out, axis=1)


def _activate(r, act, gains, hd):
    if act == "silu":
        return r * _sigmoid(r)
    if act == "sigmoid":
        return _sigmoid(r)
    if act in gains:
        return _head_norm(r, gains[act][...], hd)
    return r


ACTS = ("none", "silu", "sigmoid", "qnorm", "knorm")


def _inproj_kernel(x_ref, g_ref, wt_ref, w_ref, qg_ref, kg_ref, outt_ref, out_ref, h_ref, *, acts_t, acts_r, hd):
    j = pl.program_id(1)
    n_t = len(acts_t)
    gains = {"qnorm": qg_ref, "knorm": kg_ref}

    @pl.when(j == 0)
    def _():
        x = x_ref[...]
        ms = jnp.mean(x * x, axis=-1, keepdims=True)
        h_ref[...] = (x * lax.rsqrt(ms + NORM_EPS) * g_ref[...]).astype(BF16)

    def tiles_with(acts, act, base):
        cond = None
        for idx, a in enumerate(acts):
            if a == act:
                c = j == base + idx
                cond = c if cond is None else cond | c
        return cond

    for act in ACTS:
        cond = tiles_with(acts_t, act, 0)
        if cond is not None:
            assert act not in gains
            @pl.when(cond)
            def _(act=act):
                r = lax.dot_general(wt_ref[...], h_ref[...], (((1,), (1,)), ((), ())), preferred_element_type=F32)
                outt_ref[...] = _activate(r, act, gains, hd).astype(BF16)
        cond = tiles_with(acts_r, act, n_t)
        if cond is not None:
            @pl.when(cond)
            def _(act=act):
                n_sub = NORM_SPLIT if act in gains else 1
                sub = h_ref.shape[0] // n_sub
                for p in range(n_sub):
                    rows = slice(p * sub, (p + 1) * sub)
                    r = jnp.dot(h_ref[rows, :], w_ref[...], preferred_element_type=F32)
                    out_ref[rows, :] = _activate(r, act, gains, hd).astype(BF16)


def _inproj(x, g, wt, w, qg, kg, *, acts_t, acts_r, tm, tn, hd):
    t, d = x.shape
    n_t, n_r = len(acts_t), len(acts_r)
    assert wt.shape == (n_t * tn, d) and w.shape == (d, n_r * tn) and set(acts_t) | set(acts_r) <= set(ACTS)
    kern = functools.partial(_inproj_kernel, acts_t=tuple(acts_t), acts_r=tuple(acts_r), hd=hd)
    return pl.pallas_call(
        kern,
        grid=(t // tm, n_t + n_r),
        in_specs=[
            pl.BlockSpec((tm, d), lambda i, j: (i, 0)),
            pl.BlockSpec((1, d), lambda i, j: (0, 0)),
            pl.BlockSpec((tn, d), lambda i, j: (jnp.minimum(j, n_t - 1), 0)),
            pl.BlockSpec((d, tn), lambda i, j: (0, jnp.maximum(j - n_t, 0))),
            pl.BlockSpec((1, tn), lambda i, j: (0, 0)),
            pl.BlockSpec((1, tn), lambda i, j: (0, 0)),
        ],
        out_specs=[
            pl.BlockSpec((tn, tm), lambda i, j: (jnp.minimum(j, n_t - 1), i)),
            pl.BlockSpec((tm, tn), lambda i, j: (i, jnp.maximum(j - n_t, 0))),
        ],
        out_shape=[jax.ShapeDtypeStruct((n_t * tn, t), BF16), jax.ShapeDtypeStruct((t, n_r * tn), BF16)],
        scratch_shapes=[pltpu.VMEM((tm, d), BF16)],
        compiler_params=_params("arbitrary", "arbitrary"),
        name="inproj",
    )(x, g, wt, w, qg, kg)


def _cmul(ar, ai, br, bi):
    return ar * br - ai * bi, ar * bi + ai * br


def _hi_lo(x):
    if x.dtype == BF16:
        return x, None
    hi = x.astype(BF16)
    return hi, (x - hi.astype(F32)).astype(BF16)


def _dot_hi_lo(a, b):
    ah, al = _hi_lo(a)
    bh, bl = _hi_lo(b)
    acc = jnp.dot(ah, bh, preferred_element_type=F32)
    if bl is not None:
        acc = acc + jnp.dot(ah, bl, preferred_element_type=F32)
    if al is not None:
        acc = acc + jnp.dot(al, bh, preferred_element_type=F32)
    return acc


def _discretize(a_re, a_im, log_dt):
    lam_re = jnp.minimum(a_re, -1e-4)
    lam_im = a_im
    dt = jnp.exp(log_dt)
    mag = jnp.exp(lam_re * dt)
    abar_re = mag * jnp.cos(lam_im * dt)
    abar_im = mag * jnp.sin(lam_im * dt)
    den = lam_re * lam_re + lam_im * lam_im
    f_re = ((abar_re - 1.0) * lam_re + abar_im * lam_im) / den
    f_im = (abar_im * lam_re - (abar_re - 1.0) * lam_im) / den
    return abar_re, abar_im, f_re, f_im


def _squarings(ar, ai, n):
    out = [(ar, ai)]
    for _ in range(n - 1):
        ar, ai = _cmul(ar, ai, ar, ai)
        out.append((ar, ai))
    return out


def _power_table(sq, exps, shape):
    pr = jnp.ones(shape, F32)
    pi = jnp.zeros(shape, F32)
    for k, (ar, ai) in enumerate(sq):
        bit = ((exps >> k) & 1) == 1
        fr = jnp.where(bit, ar, 1.0)
        fi = jnp.where(bit, ai, 0.0)
        pr, pi = _cmul(pr, pi, fr, fi)
    return pr, pi


def _ssm_ops_kernel(arow_ref, bt_ref, c_ref, ct_ref, w_ref, f_ref, e_ref, dec_ref, k_ref, *, h_dim, p_dim):
    lc = CHUNK
    nbits = lc.bit_length()
    lane = lax.broadcasted_iota(jnp.int32, (1, LANES), 1)
    left = lane < lc
    sub = lax.broadcasted_iota(jnp.int32, (lc, 1), 0)
    eye = lax.broadcasted_iota(jnp.int32, (p_dim, 1), 0) == lane

    q2 = []
    rmat = []
    bb_rows = []
    sq_rows = []
    for d in range(2):
        abr, abi, fr, fi = _discretize(arow_ref[d, 0], arow_ref[d, 1], arow_ref[d, 2])
        bbr, bbi = _cmul(fr, fi, bt_ref[d, 0], bt_ref[d, 1])
        bb_rows.append((bbr, bbi))
        sq_row = _squarings(abr, abi, nbits + SCAN_LEVELS)
        sq_rows.append(sq_row)
        cr, ci = c_ref[d, 0], c_ref[d, 1]
        blocks = []
        for h in range(h_dim):
            qr, qi = _cmul(cr, ci, bbr[h:h + 1, :], bbi[h:h + 1, :])
            blocks.append(jnp.where(left, qr, -qi))
        q2.append(jnp.concatenate(blocks, axis=0))
        cabr = jnp.sum(jnp.where(eye, abr, 0.0), axis=-1, keepdims=True)
        cabi = jnp.sum(jnp.where(eye, abi, 0.0), axis=-1, keepdims=True)
        sq_col = _squarings(cabr, cabi, nbits)
        lag = lane - lc
        if d == 0:
            exps, live = jnp.maximum(lag, 0), lag >= 0
        else:
            exps, live = jnp.maximum(-lag, 0), lag <= 0
        vr, vi = _power_table(sq_col, exps, (p_dim, LANES))
        rmat.append(jnp.concatenate([jnp.where(live, vr, 0.0), jnp.where(live, vi, 0.0)], axis=0))
        if d == 0:
            er, ei = _cmul(jnp.where(left, pltpu.roll(vr, lc, 1), vr), jnp.where(left, pltpu.roll(vi, lc, 1), vi),
                           cabr, cabi)
        else:
            er, ei = jnp.where(left, vr, pltpu.roll(vr, lc, 1)), jnp.where(left, vi, pltpu.roll(vi, lc, 1))
        n_rep = (h_dim * lc) // LANES
        er = jnp.concatenate([er] * n_rep, axis=1)
        ei = jnp.concatenate([ei] * n_rep, axis=1)
        rep_r = lax.broadcasted_iota(jnp.int32, (h_dim, h_dim * lc), 0)
        rep_c = lax.broadcasted_iota(jnp.int32, (h_dim, h_dim * lc), 1)
        rep = (_slot_channel(rep_c >> (lc.bit_length() - 1), h_dim) == rep_r).astype(BF16)
        cer = _dot_hi_lo(ct_ref[d, 0], rep)
        cei = _dot_hi_lo(ct_ref[d, 1], rep)
        gr, gi = _cmul(cer, cei, er, ei)
        e_ref[2 * d * p_dim:(2 * d + 1) * p_dim, :] = gr.astype(BF16)
        e_ref[(2 * d + 1) * p_dim:(2 * d + 2) * p_dim, :] = (-gi).astype(BF16)
        for k in range(SCAN_LEVELS):
            dr_, di_ = sq_row[nbits - 1 + k]
            dec_ref[(d * SCAN_LEVELS + k) * 2:(d * SCAN_LEVELS + k) * 2 + 1, :] = dr_
            dec_ref[(d * SCAN_LEVELS + k) * 2 + 1:(d * SCAN_LEVELS + k) * 2 + 2, :] = jnp.where(left, -di_, di_)

    kbi = _dot_hi_lo(q2[0], rmat[0]) + _dot_hi_lo(q2[1], rmat[1])
    k_ref[0] = kbi
    k_ref[1] = pltpu.roll(kbi, lc, 1)

    fexp = (lc - 1 - sub, sub)
    vrow = [_power_table(sq_rows[d][:nbits], fexp[d], (lc, LANES)) for d in range(2)]

    hh = h_dim // 2
    pre_roll_left = ((lane + sub) & (LANES - 1)) < lc

    def per_h(h, carry):
        slot = 2 * (h % hh) + h // hh
        for d in range(2):
            bbr, bbi = bb_rows[d]
            sel = (lax.broadcasted_iota(jnp.int32, (h_dim, 1), 0) == h).astype(F32)
            br = jnp.sum(bbr * sel, axis=0, keepdims=True)
            bi = jnp.sum(bbi * sel, axis=0, keepdims=True)
            zr, zi = _cmul(vrow[d][0], vrow[d][1], br, bi)
            f_ref[slot, :, d * LANES:(d + 1) * LANES] = jnp.where(left, zr, zi).astype(BF16)
        for m in range(hh):
            ra = k_ref[1, pl.ds(h * h_dim + m, 1), :]
            rb = k_ref[0, pl.ds(h * h_dim + m + hh, 1), :]
            src = jnp.where(pre_roll_left, jnp.broadcast_to(ra, (lc, LANES)), jnp.broadcast_to(rb, (lc, LANES)))
            tile = pltpu.roll(src, 0, 1, stride=1, stride_axis=0)
            w_ref[slot, :, m * LANES:(m + 1) * LANES] = tile.astype(BF16)
        return carry

    lax.fori_loop(0, h_dim, per_h, 0, unroll=4)


def _ssm_ops(arow, bt, c2, ct, *, groups, h_dim, p_dim):
    lc = CHUNK
    kern = functools.partial(_ssm_ops_kernel, h_dim=h_dim, p_dim=p_dim)
    blk = lambda shape: pl.BlockSpec((None,) + shape, lambda g: (g,) + (0,) * len(shape))
    return pl.pallas_call(
        kern,
        grid=(groups,),
        in_specs=[blk((2, 3, 1, LANES)), blk((2, 2, h_dim, LANES)),
                  blk((2, 2, h_dim, LANES)), blk((2, 2, p_dim, h_dim))],
        out_specs=[blk((h_dim, lc, h_dim * lc)), blk((h_dim, lc, 2 * LANES)),
                   blk((4 * p_dim, h_dim * lc)), blk((4 * SCAN_LEVELS, LANES))],
        out_shape=[jax.ShapeDtypeStruct((groups, h_dim, lc, h_dim * lc), BF16),
                   jax.ShapeDtypeStruct((groups, h_dim, lc, 2 * LANES), BF16),
                   jax.ShapeDtypeStruct((groups, 4 * p_dim, h_dim * lc), BF16),
                   jax.ShapeDtypeStruct((groups, 4 * SCAN_LEVELS, LANES), F32)],
        scratch_shapes=[pltpu.VMEM((2, h_dim * h_dim, LANES), F32)],
        compiler_params=_params("arbitrary"),
        name="ssm_ops",
    )(arow, bt, c2, ct)


def _decay_mul(x, dec_ref, d, level):
    row = (d * SCAN_LEVELS + level) * 2
    return x * dec_ref[row:row + 1, :] + pltpu.roll(x, CHUNK, 1) * dec_ref[row + 1:row + 2, :]


def _pair_scan(z, dec_ref, d, n_pairs):
    m = z.shape[0]
    cidx = lax.broadcasted_iota(jnp.int32, (m, 1), 0) & (n_pairs - 1)

    def shifted(x, sh):
        if d == 0:
            return jnp.where(cidx >= sh, pltpu.roll(x, sh, 0), 0.0)
        return jnp.where(cidx < n_pairs - sh, pltpu.roll(x, m - sh, 0), 0.0)

    x = z
    for k in range(n_pairs.bit_length() - 1):
        x = x + _decay_mul(shifted(x, 1 << k), dec_ref, d, k + 1)
    return shifted(x, 1)


def _lane_halves(a, b, left):
    ax = a.ndim - 1
    return jnp.where(left, a, pltpu.roll(b, CHUNK, ax)), jnp.where(left, pltpu.roll(a, CHUNK, ax), b)


def _ssm_apply_kernel(*refs, n_seq, n_chunks):
    u_refs = refs[:n_seq]
    w_ref, f_ref, e_ref, dec_ref = refs[n_seq:n_seq + 4]
    y_refs = refs[n_seq + 4:]
    units = [(g, i) for g in range(SSM_GROUPS_PER_STEP) for i in range(n_seq)]
    us = [u_refs[i][g].reshape(2 * u_refs[i].shape[2], u_refs[i].shape[3]) for g, i in units]
    zs = [jnp.dot(u, f_ref[g], preferred_element_type=F32) for u, (g, _) in zip(us, units)]
    accs = [jnp.dot(u, w_ref[g], preferred_element_type=F32) for u, (g, _) in zip(us, units)]
    states = []
    for z, (g, i) in zip(zs, units):
        dec = dec_ref.at[g]
        nc = n_chunks[i]
        half = z.shape[0] // 2
        ze, zo = z[:half], z[half:]
        pf = _pair_scan(_decay_mul(ze[:, :LANES], dec, 0, 0) + zo[:, :LANES], dec, 0, nc // 2)
        sf = jnp.concatenate([pf, _decay_mul(pf, dec, 0, 0) + ze[:, :LANES]], axis=0)
        pb = _pair_scan(ze[:, LANES:] + _decay_mul(zo[:, LANES:], dec, 1, 0), dec, 1, nc // 2)
        sb = jnp.concatenate([_decay_mul(pb, dec, 1, 0) + zo[:, LANES:], pb], axis=0)
        states.append(jnp.concatenate([sf, sb], axis=1).astype(BF16))
    for (g, i), acc, s in zip(units, accs, states):
        acc = acc + jnp.dot(s, e_ref[g], preferred_element_type=F32)
        y_refs[i][g] = acc.astype(BF16).reshape(y_refs[i].shape[1:])


def _slot_channel(slot, h_dim):
    return (slot >> 1) + (h_dim // 2) * (slot & 1)


def _to_chunks_kernel(x_ref, o_ref, *, groups, h_dim):
    tk = o_ref.shape[2]
    hh = h_dim // 2
    left = lax.broadcasted_iota(jnp.int32, (1, 1, LANES), 2) < CHUNK
    x = x_ref[...].astype(F32).reshape(groups, 2, hh, tk * LANES)
    ev, od = [], []
    for k in range(tk):
        e, o = _lane_halves(x[:, 0, :, k * LANES:(k + 1) * LANES], x[:, 1, :, k * LANES:(k + 1) * LANES], left)
        ev.append(e)
        od.append(o)
    for p, parts in enumerate((ev, od)):
        o_ref[:, p] = jnp.stack(parts, axis=1).reshape(groups, tk, hh * LANES).astype(o_ref.dtype)


def _from_chunks_kernel(x_ref, o_ref, *, groups, h_dim):
    tk = x_ref.shape[2]
    hh = h_dim // 2
    left = lax.broadcasted_iota(jnp.int32, (1, 1, LANES), 2) < CHUNK
    ye = x_ref[:, 0].astype(F32).reshape(groups, tk, hh, LANES)
    yo = x_ref[:, 1].astype(F32).reshape(groups, tk, hh, LANES)
    lo, hi = zip(*[_lane_halves(ye[:, k], yo[:, k], left) for k in range(tk)])
    y = jnp.stack([jnp.concatenate(lo, axis=-1), jnp.concatenate(hi, axis=-1)], axis=1)
    o_ref[...] = y.reshape(groups * h_dim, tk * LANES).astype(o_ref.dtype)


def _to_chunks(xt, *, groups, h_dim, tk):
    t = xt.shape[1]
    return pl.pallas_call(
        functools.partial(_to_chunks_kernel, groups=groups, h_dim=h_dim),
        grid=(t // (tk * LANES),),
        in_specs=[pl.BlockSpec((groups * h_dim, tk * LANES), lambda i: (0, i))],
        out_specs=pl.BlockSpec((groups, 2, tk, h_dim * CHUNK), lambda i: (0, 0, i, 0)),
        out_shape=jax.ShapeDtypeStruct((groups, 2, t // LANES, h_dim * CHUNK), BF16),
        compiler_params=_params("arbitrary"),
        name="to_chunks",
    )(xt)


def _from_chunks(y4, *, groups, h_dim, tk):
    t = y4.shape[2] * LANES
    return pl.pallas_call(
        functools.partial(_from_chunks_kernel, groups=groups, h_dim=h_dim),
        grid=(t // (tk * LANES),),
        in_specs=[pl.BlockSpec((groups, 2, tk, h_dim * CHUNK), lambda i: (0, 0, i, 0))],
        out_specs=pl.BlockSpec((groups * h_dim, tk * LANES), lambda i: (0, i)),
        out_shape=jax.ShapeDtypeStruct((groups * h_dim, t), BF16),
        compiler_params=_params("arbitrary"),
        name="from_chunks",
    )(y4)


def _ssm_apply(u4s, w, f, e, dec, *, n_chunks):
    groups = w.shape[0]
    assert groups % SSM_GROUPS_PER_STEP == 0
    blk = lambda shape: pl.BlockSpec((SSM_GROUPS_PER_STEP,) + shape, lambda g: (g,) + (0,) * len(shape))
    kern = functools.partial(_ssm_apply_kernel, n_seq=len(u4s), n_chunks=tuple(n_chunks))
    return pl.pallas_call(
        kern,
        grid=(groups // SSM_GROUPS_PER_STEP,),
        in_specs=[blk(u.shape[1:]) for u in u4s]
        + [blk(w.shape[1:]), blk(f.shape[1:]), blk(e.shape[1:]), blk(dec.shape[1:])],
        out_specs=[blk(u.shape[1:]) for u in u4s],
        out_shape=[jax.ShapeDtypeStruct(u.shape, BF16) for u in u4s],
        compiler_params=_params("arbitrary"),
        name="ssm_apply",
    )(*u4s, w, f, e, dec)


def _attn_bias_kernel(rb_ref, out_ref, pair_ref, *, hd):
    hp = pl.program_id(0)
    gw = GRID_W
    cq = lax.broadcasted_iota(jnp.int32, (gw, LANES), 0)
    lane = lax.broadcasted_iota(jnp.int32, (gw, LANES), 1)
    left = lane < gw
    ck = lane & (gw - 1)
    cstart = jnp.clip(cq - WIN_COLS // 2, 0, gw - WIN_COLS)
    valid = (ck >= cstart) & (ck < cstart + WIN_COLS)
    dc = ck - cq + (WIN_COLS - 1)
    n_dr = 2 * WIN_ROWS - 1
    for head in range(2):
        hidx = hp * 2 + head
        for dr in range(n_dr - 1):
            acc = jnp.zeros((gw, LANES), F32)
            for i in range(2 * WIN_COLS - 1):
                coef = jnp.where(left, rb_ref[hidx, dr, i], rb_ref[hidx, dr + 1, i])
                acc = jnp.where(dc == i, coef, acc)
            pair_ref[head, dr] = jnp.where(valid, acc * LOG2E, MASK_VALUE)
    for dr0 in range(WIN_ROWS):
        for head in range(2):
            for wq in range(WIN_ROWS // 2):
                out_ref[dr0, head * gw:(head + 1) * gw, wq * LANES:(wq + 1) * LANES] = pair_ref[head, dr0 + 2 * wq]


def _attn_bias(rel_bias):
    nh = rel_bias.shape[0]
    gw = GRID_W
    return pl.pallas_call(
        functools.partial(_attn_bias_kernel, hd=gw),
        grid=(nh // 2,),
        in_specs=[pl.BlockSpec(memory_space=pltpu.SMEM)],
        out_specs=pl.BlockSpec((None, WIN_ROWS, 2 * gw, WIN_ROWS * gw), lambda h: (h, 0, 0, 0)),
        out_shape=jax.ShapeDtypeStruct((nh // 2, WIN_ROWS, 2 * gw, WIN_ROWS * gw), F32),
        scratch_shapes=[pltpu.VMEM((2, 2 * WIN_ROWS - 2, gw, LANES), F32)],
        compiler_params=_params("arbitrary"),
        name="attn_bias",
    )(rel_bias)


def _attn_kernel(q_ref, k_ref, v_ref, za_ref, bias_ref, o_ref, *, rows, hd):
    gw = GRID_W
    win = WIN_ROWS * gw
    lane = lax.broadcasted_iota(jnp.int32, (1, 2 * hd), 1)
    left = lane < hd

    def rows_body(gi, carry):
        slices, scores = [], []
        for i in range(ROW_UNROLL):
            r = gi * ROW_UNROLL + i
            rstart = jnp.clip(r - WIN_ROWS // 2, 0, rows - WIN_ROWS)
            dr0 = rstart - r + (WIN_ROWS - 1)
            qs = pl.ds(pl.multiple_of(r * gw, gw), gw)
            ks = pl.ds(pl.multiple_of(rstart * gw, gw), win)
            slices.append((qs, ks))
            q = q_ref[qs, :]
            zero = jnp.zeros_like(q)
            q2 = jnp.concatenate([jnp.where(left, q, zero), jnp.where(left, zero, q)], axis=0)
            s = lax.dot_general(q2, k_ref[ks, :], (((1,), (1,)), ((), ())), preferred_element_type=F32)
            scores.append(s + bias_ref[dr0])
        for (qs, ks), s in zip(slices, scores):
            m = jnp.max(s, axis=-1, keepdims=True)
            p = jnp.exp2(s - m)
            l = jnp.sum(p, axis=-1, keepdims=True)
            o2 = jnp.dot(p.astype(BF16), v_ref[ks, :], preferred_element_type=F32) / l
            o = jnp.where(left, o2[:gw], o2[gw:]) * za_ref[qs, :].astype(F32)
            o_ref[qs, :] = o.astype(BF16)
        return carry

    lax.fori_loop(0, rows // ROW_UNROLL, rows_body, 0)


def _attention(proj3, bias, *, cols, aw, hd):
    b, l, _ = proj3.shape
    rows = l // GRID_W
    nhp = aw // (2 * hd)
    assert rows % ROW_UNROLL == 0
    col = lambda off: pl.BlockSpec((None, l, 2 * hd), lambda h, i, off=off: (i, 0, off + h))
    kern = functools.partial(_attn_kernel, rows=rows, hd=hd)
    return pl.pallas_call(
        kern,
        grid=(nhp, b),
        in_specs=[col(cols[0]), col(cols[1]), col(cols[2]), col(cols[3]),
                  pl.BlockSpec((None,) + bias.shape[1:], lambda h, i: (h, 0, 0, 0))],
        out_specs=pl.BlockSpec((None, l, 2 * hd), lambda h, i: (i, 0, h)),
        out_shape=jax.ShapeDtypeStruct((b, l, aw), BF16),
        compiler_params=_params("arbitrary", "arbitrary"),
        name="attention",
    )(proj3, proj3, proj3, proj3, bias)


def _gelu_tanh(x):
    c = math.sqrt(2.0 / math.pi)
    return 0.5 * x * (1.0 + jnp.tanh(c * (x + 0.044715 * (x * x * x))))


def _merge_kernel(yt_ref, ut_ref, zt_ref, oa_ref, gs_ref, ga_ref, d_ref, bg_ref, wg_ref, wbs_ref, wba_ref, m_ref):
    tm = m_ref.shape[0]
    sub = tm // MERGE_SPLIT
    parts = [slice(i * sub, (i + 1) * sub) for i in range(MERGE_SPLIT)]
    acts = [_gelu_tanh(yt_ref[:, p].astype(F32) + d_ref[...] * ut_ref[:, p].astype(F32)) for p in parts]
    mas = [jnp.dot(oa_ref[p, :], wba_ref[...], preferred_element_type=F32) for p in parts]
    gates = [jnp.dot(wg_ref[...], a.astype(BF16), preferred_element_type=F32) + bg_ref[...] for a in acts]
    osts = [(a * _sigmoid(g) * zt_ref[:, p].astype(F32)).astype(BF16) for a, g, p in zip(acts, gates, parts)]
    mss = [lax.dot_general(o, wbs_ref[...], (((0,), (0,)), ((), ())), preferred_element_type=F32) for o in osts]
    for p, ms, ma in zip(parts, mss, mas):
        m_ref[p, :] = (gs_ref[p, :].astype(F32) * ms + ga_ref[p, :].astype(F32) * ma).astype(BF16)


def _outproj_kernel(m_ref, x_ref, wo_ref, out_ref):
    out_ref[...] = x_ref[...] + jnp.dot(m_ref[...], wo_ref[...], preferred_element_type=F32)


def _const_spec(shape):
    return pl.BlockSpec(shape, lambda i: (0,) * len(shape), pipeline_mode=pl.Buffered(1))


def _merge(yt, projt, oa, proj, dcol, bgcol, wgt, wbs, wba, *, sw, aw, tm):
    t = oa.shape[0]
    d = wbs.shape[1]
    assert (4 * aw) % d == 0
    n_gs = (4 * aw) // d
    return pl.pallas_call(
        _merge_kernel,
        grid=(t // tm,),
        in_specs=[
            pl.BlockSpec((sw, tm), lambda i: (0, i)),
            pl.BlockSpec((sw, tm), lambda i: (0, i)),
            pl.BlockSpec((sw, tm), lambda i: (1, i)),
            pl.BlockSpec((tm, aw), lambda i: (i, 0)),
            pl.BlockSpec((tm, d), lambda i: (i, n_gs)),
            pl.BlockSpec((tm, d), lambda i: (i, n_gs + 1)),
            _const_spec((sw, 1)), _const_spec((sw, 1)),
            _const_spec(wgt.shape), _const_spec(wbs.shape), _const_spec(wba.shape),
        ],
        out_specs=pl.BlockSpec((tm, d), lambda i: (i, 0)),
        out_shape=jax.ShapeDtypeStruct((t, d), BF16),
        compiler_params=_params("arbitrary"),
        name="merge",
    )(yt, projt, projt, oa, proj, proj, dcol, bgcol, wgt, wbs, wba)


def _outproj(m, x, wo, *, tm):
    t, d = x.shape
    return pl.pallas_call(
        _outproj_kernel,
        grid=(t // tm,),
        in_specs=[pl.BlockSpec((tm, d), lambda i: (i, 0)), pl.BlockSpec((tm, d), lambda i: (i, 0)),
                  _const_spec(wo.shape)],
        out_specs=pl.BlockSpec((tm, d), lambda i: (i, 0)),
        out_shape=jax.ShapeDtypeStruct((t, d), x.dtype),
        compiler_params=_params("arbitrary"),
        name="outproj",
    )(m, x, wo)


def _tile(n, pref):
    t = min(n, pref)
    while n % t:
        t //= 2
    return t


def _layer(xs, norm_g, w_in, a_re, a_im, log_dt, b_re, b_im, c_re, c_im, d_skip, w_glu, b_glu,
           q_g, k_g, rel_bias, w_bs, w_ba, w_out):
    d_model = w_in.shape[0]
    sw = d_skip.shape[0]
    groups, p_dim = a_re.shape[1], a_re.shape[2]
    h_dim = sw // groups
    hd = q_g.shape[0]
    aw = w_ba.shape[0]
    lc = CHUNK
    assert h_dim * lc % LANES == 0 and lc * 2 == LANES and h_dim % 2 == 0

    wt = w_in[:, :2 * sw].astype(BF16).T
    wr = w_in[:, 2 * sw:].astype(BF16)
    tn = _tile(math.gcd(sw, aw), INPROJ_TN)
    acts_t = ["none"] * (sw // tn) + ["silu"] * (sw // tn)
    acts_r = (["qnorm"] * (aw // tn) + ["knorm"] * (aw // tn) + ["none"] * (aw // tn) + ["silu"] * (aw // tn)
              + ["sigmoid"] * (2 * d_model // tn))
    pair = 2 * hd
    attn_cols = tuple(n * aw // pair for n in range(4))
    g2 = norm_g.reshape(1, d_model).astype(F32)
    dup = lambda v: jnp.concatenate([v, v], axis=-1)
    abase = jnp.stack([a_re, a_im, jnp.broadcast_to(log_dt[..., None], a_re.shape)], axis=1)
    abase = abase.transpose(2, 0, 1, 3).astype(F32)
    arow = dup(abase)[:, :, :, None, :]
    bt = dup(jnp.stack([b_re, b_im], axis=1).transpose(2, 0, 1, 4, 3).astype(F32))
    cc = jnp.stack([c_re, c_im], axis=1).transpose(2, 0, 1, 3, 4).astype(F32)
    c2 = dup(cc)
    ct = cc.transpose(0, 1, 2, 4, 3)
    assert tn % hd == 0
    qg_row = jnp.tile(q_g.astype(F32) * (hd ** -0.5 * LOG2E), tn // hd).reshape(1, tn)
    kg_row = jnp.tile(k_g.astype(F32), tn // hd).reshape(1, tn)
    dcol = d_skip.astype(F32).reshape(sw, 1)
    bgcol = b_glu.astype(F32).reshape(sw, 1)
    wgt = w_glu.astype(BF16).T
    wbs = w_bs.astype(BF16)
    wba = w_ba.astype(BF16)
    wo = w_out.astype(BF16)

    w_t, f_t, e_t, dec = _ssm_ops(arow, bt, c2, ct, groups=groups, h_dim=h_dim, p_dim=p_dim)
    bias = _attn_bias(rel_bias.astype(F32))

    projts, projs, u3s, ncs = [], [], [], []
    for x in xs:
        b, l, _ = x.shape
        t = b * l
        assert l % (GRID_W * ROW_UNROLL) == 0 and l % (WIN_ROWS * GRID_W) == 0
        assert (l // lc) & (l // lc - 1) == 0 and l // lc <= 2 ** SCAN_LEVELS
        projt, proj = _inproj(x.reshape(t, d_model), g2, wt, wr, qg_row, kg_row, acts_t=acts_t, acts_r=acts_r,
                              tm=_tile(t, INPROJ_TM), tn=tn, hd=hd)
        projts.append(projt)
        projs.append(proj)
        u3s.append(_to_chunks(projt, groups=groups, h_dim=h_dim, tk=_tile(t // LANES, RELAYOUT_TK)))
        ncs.append(l // lc)

    y4s = _ssm_apply(u3s, w_t.reshape(groups, h_dim * lc, h_dim * lc), f_t.reshape(groups, h_dim * lc, 2 * LANES),
                     e_t, dec, n_chunks=ncs)

    outs = []
    for x, projt, proj, y4 in zip(xs, projts, projs, y4s):
        b, l, _ = x.shape
        t = b * l
        yt = _from_chunks(y4, groups=groups, h_dim=h_dim, tk=_tile(t // LANES, RELAYOUT_TK))
        oa = _attention(proj.reshape(b, l, -1), bias, cols=attn_cols, aw=aw, hd=hd).reshape(t, aw)
        m = _merge(yt, projt, oa, proj, dcol, bgcol, wgt, wbs, wba, sw=sw, aw=aw, tm=_tile(t, MERGE_TM))
        out = _outproj(m, x.reshape(t, d_model), wo, tm=_tile(t, OUTPROJ_TM))
        outs.append(out.reshape(b, l, d_model))
    return outs


def kernel(x_prompt, x_sample, norm_g, w_in, ssm_a_re, ssm_a_im, ssm_log_dt, ssm_b_re, ssm_b_im, ssm_c_re, ssm_c_im, ssm_d, w_glu, b_glu, q_norm_g, k_norm_g, rel_bias, w_branch_ssm, w_branch_attn, w_out):
    xs = [x_prompt, x_sample]
    for layer in range(norm_g.shape[0]):
        xs = _layer(xs, norm_g[layer], w_in[layer], ssm_a_re[layer], ssm_a_im[layer], ssm_log_dt[layer],
                    ssm_b_re[layer], ssm_b_im[layer], ssm_c_re[layer], ssm_c_im[layer], ssm_d[layer],
                    w_glu[layer], b_glu[layer], q_norm_g[layer], k_norm_g[layer], rel_bias[layer],
                    w_branch_ssm[layer], w_branch_attn[layer], w_out[layer])
    return (xs[0], xs[1])
```

```python
import functools
import math

import numpy as np
import jax
import jax.numpy as jnp
from jax import lax
from jax.experimental import pallas as pl
from jax.experimental.pallas import tpu as pltpu

NORM_EPS = 1e-6
MASK_VALUE = -1e30
LOG2E = math.log2(math.e)
GRID_W = 64
WIN_ROWS = 8
WIN_COLS = 16
CHUNK = 64
LANES = 128
ROW_UNROLL = 8
NORM_SPLIT = 4
MERGE_SPLIT = 2
SSM_GROUPS_PER_STEP = 2
SCAN_LEVELS = 7
VMEM_LIMIT_BYTES = 56 * 1024 * 1024
INPROJ_TM = 1024
INPROJ_TN = 1024
MERGE_TM = 512
OUTPROJ_TM = 512
RELAYOUT_TK = 8

F32 = jnp.float32
BF16 = jnp.bfloat16


def _params(*sem):
    return pltpu.CompilerParams(dimension_semantics=sem, vmem_limit_bytes=VMEM_LIMIT_BYTES)


def _sigmoid(x):
    return 0.5 * jnp.tanh(0.5 * x) + 0.5


def _head_norm(r, gain, hd):
    lane = lax.broadcasted_iota(jnp.int32, (1, 2 * hd), 1)
    left = lane < hd
    out = []
    for c in range(r.shape[1] // (2 * hd)):
        x = r[:, c * 2 * hd:(c + 1) * 2 * hd]
        x2 = x * x
        sa = jnp.sum(jnp.where(left, x2, 0.0), axis=-1, keepdims=True)
        sb = jnp.sum(jnp.where(left, 0.0, x2), axis=-1, keepdims=True)
        ms = jnp.where(left, sa, sb) * (1.0 / hd)
        out.append(x * lax.rsqrt(ms + NORM_EPS) * gain[:, c * 2 * hd:(c + 1) * 2 * hd])
    return jnp.concatenate(out, axis=1)


def _activate(r, act, gains, hd):
    if act == "silu":
        return r * _sigmoid(r)
    if act == "sigmoid":
        return _sigmoid(r)
    if act in gains:
        return _head_norm(r, gains[act][...], hd)
    return r


ACTS = ("none", "silu", "sigmoid", "qnorm", "knorm")


def _inproj_kernel(x_ref, g_ref, wt_ref, w_ref, qg_ref, kg_ref, outt_ref, out_ref, h_ref, *, acts_t, acts_r, hd):
    j = pl.program_id(1)
    n_t = len(acts_t)
    gains = {"qnorm": qg_ref, "knorm": kg_ref}

    @pl.when(j == 0)
    def _():
        x = x_ref[...]
        ms = jnp.mean(x * x, axis=-1, keepdims=True)
        h_ref[...] = (x * lax.rsqrt(ms + NORM_EPS) * g_ref[...]).astype(BF16)

    def tiles_with(acts, act, base):
        cond = None
        for idx, a in enumerate(acts):
            if a == act:
                c = j == base + idx
                cond = c if cond is None else cond | c
        return cond

    for act in ACTS:
        cond = tiles_with(acts_t, act, 0)
        if cond is not None:
            assert act not in gains
            @pl.when(cond)
            def _(act=act):
                r = lax.dot_general(wt_ref[...], h_ref[...], (((1,), (1,)), ((), ())), preferred_element_type=F32)
                outt_ref[...] = _activate(r, act, gains, hd).astype(BF16)
        cond = tiles_with(acts_r, act, n_t)
        if cond is not None:
            @pl.when(cond)
            def _(act=act):
                n_sub = NORM_SPLIT if act in gains else 1
                sub = h_ref.shape[0] // n_sub
                for p in range(n_sub):
                    rows = slice(p * sub, (p + 1) * sub)
                    r = jnp.dot(h_ref[rows, :], w_ref[...], preferred_element_type=F32)
                    out_ref[rows, :] = _activate(r, act, gains, hd).astype(BF16)


def _inproj(x, g, wt, w, qg, kg, *, acts_t, acts_r, tm, tn, hd):
    t, d = x.shape
    n_t, n_r = len(acts_t), len(acts_r)
    assert wt.shape == (n_t * tn, d) and w.shape == (d, (n_t + n_r) * tn) and set(acts_t) | set(acts_r) <= set(ACTS)
    kern = functools.partial(_inproj_kernel, acts_t=tuple(acts_t), acts_r=tuple(acts_r), hd=hd)
    return pl.pallas_call(
        kern,
        grid=(t // tm, n_t + n_r),
        in_specs=[
            pl.BlockSpec((tm, d), lambda i, j: (i, 0)),
            pl.BlockSpec((1, d), lambda i, j: (0, 0)),
            pl.BlockSpec((tn, d), lambda i, j: (jnp.minimum(j, n_t - 1), 0)),
            pl.BlockSpec((d, tn), lambda i, j: (0, jnp.maximum(j, n_t))),
            pl.BlockSpec((1, tn), lambda i, j: (0, 0)),
            pl.BlockSpec((1, tn), lambda i, j: (0, 0)),
        ],
        out_specs=[
            pl.BlockSpec((tn, tm), lambda i, j: (jnp.minimum(j, n_t - 1), i)),
            pl.BlockSpec((tm, tn), lambda i, j: (i, jnp.maximum(j - n_t, 0))),
        ],
        out_shape=[jax.ShapeDtypeStruct((n_t * tn, t), BF16), jax.ShapeDtypeStruct((t, n_r * tn), BF16)],
        scratch_shapes=[pltpu.VMEM((tm, d), BF16)],
        compiler_params=_params("arbitrary", "arbitrary"),
        name="inproj",
    )(x, g, wt, w, qg, kg)


def _cmul(ar, ai, br, bi):
    return ar * br - ai * bi, ar * bi + ai * br


def _hi_lo(x):
    if x.dtype == BF16:
        return x, None
    hi = x.astype(BF16)
    return hi, (x - hi.astype(F32)).astype(BF16)


def _dot_hi_lo(a, b):
    ah, al = _hi_lo(a)
    bh, bl = _hi_lo(b)
    acc = jnp.dot(ah, bh, preferred_element_type=F32)
    if bl is not None:
        acc = acc + jnp.dot(ah, bl, preferred_element_type=F32)
    if al is not None:
        acc = acc + jnp.dot(al, bh, preferred_element_type=F32)
    return acc


def _discretize(a_re, a_im, log_dt):
    lam_re = jnp.minimum(a_re, -1e-4)
    lam_im = a_im
    dt = jnp.exp(log_dt)
    mag = jnp.exp(lam_re * dt)
    abar_re = mag * jnp.cos(lam_im * dt)
    abar_im = mag * jnp.sin(lam_im * dt)
    den = lam_re * lam_re + lam_im * lam_im
    f_re = ((abar_re - 1.0) * lam_re + abar_im * lam_im) / den
    f_im = (abar_im * lam_re - (abar_re - 1.0) * lam_im) / den
    return abar_re, abar_im, f_re, f_im


def _squarings(ar, ai, n):
    out = [(ar, ai)]
    for _ in range(n - 1):
        ar, ai = _cmul(ar, ai, ar, ai)
        out.append((ar, ai))
    return out


def _power_table(sq, exps, shape):
    pr = jnp.ones(shape, F32)
    pi = jnp.zeros(shape, F32)
    for k, (ar, ai) in enumerate(sq):
        bit = ((exps >> k) & 1) == 1
        fr = jnp.where(bit, ar, 1.0)
        fi = jnp.where(bit, ai, 0.0)
        pr, pi = _cmul(pr, pi, fr, fi)
    return pr, pi


def _ssm_ops_kernel(arow_ref, bt_ref, c_ref, ct_ref, w_ref, f_ref, e_ref, dec_ref, k_ref, *, h_dim, p_dim):
    lc = CHUNK
    nbits = lc.bit_length()
    lane = lax.broadcasted_iota(jnp.int32, (1, LANES), 1)
    left = lane < lc
    sub = lax.broadcasted_iota(jnp.int32, (lc, 1), 0)
    eye = lax.broadcasted_iota(jnp.int32, (p_dim, 1), 0) == lane

    q2 = []
    rmat = []
    bb_rows = []
    sq_rows = []
    for d in range(2):
        abr, abi, fr, fi = _discretize(arow_ref[d, 0], arow_ref[d, 1], arow_ref[d, 2])
        bbr, bbi = _cmul(fr, fi, bt_ref[d, 0], bt_ref[d, 1])
        bb_rows.append((bbr, bbi))
        sq_row = _squarings(abr, abi, nbits + SCAN_LEVELS)
        sq_rows.append(sq_row)
        cr, ci = c_ref[d, 0], c_ref[d, 1]
        blocks = []
        for h in range(h_dim):
            qr, qi = _cmul(cr, ci, bbr[h:h + 1, :], bbi[h:h + 1, :])
            blocks.append(jnp.where(left, qr, -qi))
        q2.append(jnp.concatenate(blocks, axis=0))
        cabr = jnp.sum(jnp.where(eye, abr, 0.0), axis=-1, keepdims=True)
        cabi = jnp.sum(jnp.where(eye, abi, 0.0), axis=-1, keepdims=True)
        sq_col = _squarings(cabr, cabi, nbits)
        lag = lane - lc
        if d == 0:
            exps, live = jnp.maximum(lag, 0), lag >= 0
        else:
            exps, live = jnp.maximum(-lag, 0), lag <= 0
        vr, vi = _power_table(sq_col, exps, (p_dim, LANES))
        rmat.append(jnp.concatenate([jnp.where(live, vr, 0.0), jnp.where(live, vi, 0.0)], axis=0))
        if d == 0:
            er, ei = _cmul(jnp.where(left, pltpu.roll(vr, lc, 1), vr), jnp.where(left, pltpu.roll(vi, lc, 1), vi),
                           cabr, cabi)
        else:
            er, ei = jnp.where(left, vr, pltpu.roll(vr, lc, 1)), jnp.where(left, vi, pltpu.roll(vi, lc, 1))
        n_rep = (h_dim * lc) // LANES
        er = jnp.concatenate([er] * n_rep, axis=1)
        ei = jnp.concatenate([ei] * n_rep, axis=1)
        rep_r = lax.broadcasted_iota(jnp.int32, (h_dim, h_dim * lc), 0)
        rep_c = lax.broadcasted_iota(jnp.int32, (h_dim, h_dim * lc), 1)
        rep = (_slot_channel(rep_c >> (lc.bit_length() - 1), h_dim) == rep_r).astype(BF16)
        cer = _dot_hi_lo(ct_ref[d, 0], rep)
        cei = _dot_hi_lo(ct_ref[d, 1], rep)
        gr, gi = _cmul(cer, cei, er, ei)
        e_ref[2 * d * p_dim:(2 * d + 1) * p_dim, :] = gr.astype(BF16)
        e_ref[(2 * d + 1) * p_dim:(2 * d + 2) * p_dim, :] = (-gi).astype(BF16)
        for k in range(SCAN_LEVELS):
            dr_, di_ = sq_row[nbits - 1 + k]
            dec_ref[(d * SCAN_LEVELS + k) * 2:(d * SCAN_LEVELS + k) * 2 + 1, :] = dr_
            dec_ref[(d * SCAN_LEVELS + k) * 2 + 1:(d * SCAN_LEVELS + k) * 2 + 2, :] = jnp.where(left, -di_, di_)

    kbi = _dot_hi_lo(q2[0], rmat[0]) + _dot_hi_lo(q2[1], rmat[1])
    k_ref[0] = kbi
    k_ref[1] = pltpu.roll(kbi, lc, 1)

    fexp = (lc - 1 - sub, sub)
    vrow = [_power_table(sq_rows[d][:nbits], fexp[d], (lc, LANES)) for d in range(2)]

    hh = h_dim // 2
    pre_roll_left = ((lane + sub) & (LANES - 1)) < lc

    def per_h(h, carry):
        slot = 2 * (h % hh) + h // hh
        for d in range(2):
            bbr, bbi = bb_rows[d]
            sel = (lax.broadcasted_iota(jnp.int32, (h_dim, 1), 0) == h).astype(F32)
            br = jnp.sum(bbr * sel, axis=0, keepdims=True)
            bi = jnp.sum(bbi * sel, axis=0, keepdims=True)
            zr, zi = _cmul(vrow[d][0], vrow[d][1], br, bi)
            f_ref[slot, :, d * LANES:(d + 1) * LANES] = jnp.where(left, zr, zi).astype(BF16)
        for m in range(hh):
            ra = k_ref[1, pl.ds(h * h_dim + m, 1), :]
            rb = k_ref[0, pl.ds(h * h_dim + m + hh, 1), :]
            src = jnp.where(pre_roll_left, jnp.broadcast_to(ra, (lc, LANES)), jnp.broadcast_to(rb, (lc, LANES)))
            tile = pltpu.roll(src, 0, 1, stride=1, stride_axis=0)
            w_ref[slot, :, m * LANES:(m + 1) * LANES] = tile.astype(BF16)
        return carry

    lax.fori_loop(0, h_dim, per_h, 0, unroll=4)


def _ssm_ops(arow, bt, c2, ct, *, groups, h_dim, p_dim):
    lc = CHUNK
    kern = functools.partial(_ssm_ops_kernel, h_dim=h_dim, p_dim=p_dim)
    blk = lambda shape: pl.BlockSpec((None,) + shape, lambda g: (g,) + (0,) * len(shape))
    return pl.pallas_call(
        kern,
        grid=(groups,),
        in_specs=[blk((2, 3, 1, LANES)), blk((2, 2, h_dim, LANES)),
                  blk((2, 2, h_dim, LANES)), blk((2, 2, p_dim, h_dim))],
        out_specs=[blk((h_dim, lc, h_dim * lc)), blk((h_dim, lc, 2 * LANES)),
                   blk((4 * p_dim, h_dim * lc)), blk((4 * SCAN_LEVELS, LANES))],
        out_shape=[jax.ShapeDtypeStruct((groups, h_dim, lc, h_dim * lc), BF16),
                   jax.ShapeDtypeStruct((groups, h_dim, lc, 2 * LANES), BF16),
                   jax.ShapeDtypeStruct((groups, 4 * p_dim, h_dim * lc), BF16),
                   jax.ShapeDtypeStruct((groups, 4 * SCAN_LEVELS, LANES), F32)],
        scratch_shapes=[pltpu.VMEM((2, h_dim * h_dim, LANES), F32)],
        compiler_params=_params("arbitrary"),
        name="ssm_ops",
    )(arow, bt, c2, ct)


def _decay_mul(x, dec_ref, d, level):
    row = (d * SCAN_LEVELS + level) * 2
    return x * dec_ref[row:row + 1, :] + pltpu.roll(x, CHUNK, 1) * dec_ref[row + 1:row + 2, :]


def _pair_scan(z, dec_ref, d, n_pairs):
    m = z.shape[0]
    cidx = lax.broadcasted_iota(jnp.int32, (m, 1), 0) & (n_pairs - 1)

    def shifted(x, sh):
        if d == 0:
            return jnp.where(cidx >= sh, pltpu.roll(x, sh, 0), 0.0)
        return jnp.where(cidx < n_pairs - sh, pltpu.roll(x, m - sh, 0), 0.0)

    x = z
    for k in range(n_pairs.bit_length() - 1):
        x = x + _decay_mul(shifted(x, 1 << k), dec_ref, d, k + 1)
    return shifted(x, 1)


def _lane_halves(a, b, left):
    ax = a.ndim - 1
    return jnp.where(left, a, pltpu.roll(b, CHUNK, ax)), jnp.where(left, pltpu.roll(a, CHUNK, ax), b)


def _ssm_apply_kernel(*refs, n_seq, n_chunks):
    u_refs = refs[:n_seq]
    w_ref, f_ref, e_ref, dec_ref = refs[n_seq:n_seq + 4]
    y_refs = refs[n_seq + 4:]
    units = [(g, i) for g in range(SSM_GROUPS_PER_STEP) for i in range(n_seq)]
    us = [u_refs[i][g].reshape(2 * u_refs[i].shape[2], u_refs[i].shape[3]) for g, i in units]
    zs = [jnp.dot(u, f_ref[g], preferred_element_type=F32) for u, (g, _) in zip(us, units)]
    accs = [jnp.dot(u, w_ref[g], preferred_element_type=F32) for u, (g, _) in zip(us, units)]
    states = []
    for z, (g, i) in zip(zs, units):
        dec = dec_ref.at[g]
        nc = n_chunks[i]
        half = z.shape[0] // 2
        ze, zo = z[:half], z[half:]
        pf = _pair_scan(_decay_mul(ze[:, :LANES], dec, 0, 0) + zo[:, :LANES], dec, 0, nc // 2)
        sf = jnp.concatenate([pf, _decay_mul(pf, dec, 0, 0) + ze[:, :LANES]], axis=0)
        pb = _pair_scan(ze[:, LANES:] + _decay_mul(zo[:, LANES:], dec, 1, 0), dec, 1, nc // 2)
        sb = jnp.concatenate([_decay_mul(pb, dec, 1, 0) + zo[:, LANES:], pb], axis=0)
        states.append(jnp.concatenate([sf, sb], axis=1).astype(BF16))
    for (g, i), acc, s in zip(units, accs, states):
        acc = acc + jnp.dot(s, e_ref[g], preferred_element_type=F32)
        y_refs[i][g] = acc.astype(BF16).reshape(y_refs[i].shape[1:])


def _slot_channel(slot, h_dim):
    return (slot >> 1) + (h_dim // 2) * (slot & 1)


def _to_chunks_kernel(x_ref, o_ref, *, groups, h_dim):
    tk = o_ref.shape[2]
    hh = h_dim // 2
    left = lax.broadcasted_iota(jnp.int32, (1, 1, LANES), 2) < CHUNK
    x = x_ref[...].astype(F32).reshape(groups, 2, hh, tk * LANES)
    ev, od = [], []
    for k in range(tk):
        e, o = _lane_halves(x[:, 0, :, k * LANES:(k + 1) * LANES], x[:, 1, :, k * LANES:(k + 1) * LANES], left)
        ev.append(e)
        od.append(o)
    for p, parts in enumerate((ev, od)):
        o_ref[:, p] = jnp.stack(parts, axis=1).reshape(groups, tk, hh * LANES).astype(o_ref.dtype)


def _from_chunks_kernel(x_ref, o_ref, *, groups, h_dim):
    tk = x_ref.shape[2]
    hh = h_dim // 2
    left = lax.broadcasted_iota(jnp.int32, (1, 1, LANES), 2) < CHUNK
    ye = x_ref[:, 0].astype(F32).reshape(groups, tk, hh, LANES)
    yo = x_ref[:, 1].astype(F32).reshape(groups, tk, hh, LANES)
    lo, hi = zip(*[_lane_halves(ye[:, k], yo[:, k], left) for k in range(tk)])
    y = jnp.stack([jnp.concatenate(lo, axis=-1), jnp.concatenate(hi, axis=-1)], axis=1)
    o_ref[...] = y.reshape(groups * h_dim, tk * LANES).astype(o_ref.dtype)


def _to_chunks(xt, *, groups, h_dim, tk):
    t = xt.shape[1]
    return pl.pallas_call(
        functools.partial(_to_chunks_kernel, groups=groups, h_dim=h_dim),
        grid=(t // (tk * LANES),),
        in_specs=[pl.BlockSpec((groups * h_dim, tk * LANES), lambda i: (0, i))],
        out_specs=pl.BlockSpec((groups, 2, tk, h_dim * CHUNK), lambda i: (0, 0, i, 0)),
        out_shape=jax.ShapeDtypeStruct((groups, 2, t // LANES, h_dim * CHUNK), BF16),
        compiler_params=_params("arbitrary"),
        name="to_chunks",
    )(xt)


def _from_chunks(y4, *, groups, h_dim, tk):
    t = y4.shape[2] * LANES
    return pl.pallas_call(
        functools.partial(_from_chunks_kernel, groups=groups, h_dim=h_dim),
        grid=(t // (tk * LANES),),
        in_specs=[pl.BlockSpec((groups, 2, tk, h_dim * CHUNK), lambda i: (0, 0, i, 0))],
        out_specs=pl.BlockSpec((groups * h_dim, tk * LANES), lambda i: (0, i)),
        out_shape=jax.ShapeDtypeStruct((groups * h_dim, t), BF16),
        compiler_params=_params("arbitrary"),
        name="from_chunks",
    )(y4)


def _ssm_apply(u4s, w, f, e, dec, *, n_chunks):
    groups = w.shape[0]
    assert groups % SSM_GROUPS_PER_STEP == 0
    blk = lambda shape: pl.BlockSpec((SSM_GROUPS_PER_STEP,) + shape, lambda g: (g,) + (0,) * len(shape))
    kern = functools.partial(_ssm_apply_kernel, n_seq=len(u4s), n_chunks=tuple(n_chunks))
    return pl.pallas_call(
        kern,
        grid=(groups // SSM_GROUPS_PER_STEP,),
        in_specs=[blk(u.shape[1:]) for u in u4s]
        + [blk(w.shape[1:]), blk(f.shape[1:]), blk(e.shape[1:]), blk(dec.shape[1:])],
        out_specs=[blk(u.shape[1:]) for u in u4s],
        out_shape=[jax.ShapeDtypeStruct(u.shape, BF16) for u in u4s],
        compiler_params=_params("arbitrary"),
        name="ssm_apply",
    )(*u4s, w, f, e, dec)


def _attn_bias_kernel(rb_ref, out_ref, pair_ref, *, hd):
    gw = GRID_W
    cq = lax.broadcasted_iota(jnp.int32, (gw, LANES), 0)
    lane = lax.broadcasted_iota(jnp.int32, (gw, LANES), 1)
    ck = lane & (gw - 1)
    cstart = jnp.clip(cq - WIN_COLS // 2, 0, gw - WIN_COLS)
    valid = (ck >= cstart) & (ck < cstart + WIN_COLS)
    pre_roll_left = ((lane + cq) & (LANES - 1)) < gw
    n_dr = 2 * WIN_ROWS - 1
    for head in range(2):
        rows = rb_ref[head]
        rows_l = pltpu.roll(rows, LANES - (WIN_COLS - 1), 1)
        rows_r = pltpu.roll(rows, gw - (WIN_COLS - 1), 1)
        for dr in range(n_dr - 1):
            src = jnp.where(pre_roll_left, jnp.broadcast_to(rows_l[dr:dr + 1], (gw, LANES)),
                            jnp.broadcast_to(rows_r[dr + 1:dr + 2], (gw, LANES)))
            tile = pltpu.roll(src, 0, 1, stride=1, stride_axis=0)
            pair_ref[head, dr] = jnp.where(valid, tile * LOG2E, MASK_VALUE)
    for dr0 in range(WIN_ROWS):
        for head in range(2):
            for wq in range(WIN_ROWS // 2):
                out_ref[dr0, head * gw:(head + 1) * gw, wq * LANES:(wq + 1) * LANES] = pair_ref[head, dr0 + 2 * wq]


def _attn_bias(rel_bias):
    nh, n_dr, n_dc = rel_bias.shape
    gw = GRID_W
    rel_bias = jnp.pad(rel_bias, ((0, 0), (0, 0), (0, LANES - n_dc)))
    return pl.pallas_call(
        functools.partial(_attn_bias_kernel, hd=gw),
        grid=(nh // 2,),
        in_specs=[pl.BlockSpec((2, n_dr, LANES), lambda h: (h, 0, 0))],
        out_specs=pl.BlockSpec((None, WIN_ROWS, 2 * gw, WIN_ROWS * gw), lambda h: (h, 0, 0, 0)),
        out_shape=jax.ShapeDtypeStruct((nh // 2, WIN_ROWS, 2 * gw, WIN_ROWS * gw), F32),
        scratch_shapes=[pltpu.VMEM((2, 2 * WIN_ROWS - 2, gw, LANES), F32)],
        compiler_params=_params("arbitrary"),
        name="attn_bias",
    )(rel_bias)


def _attn_kernel(q_ref, k_ref, v_ref, za_ref, bias_ref, o_ref, *, rows, hd):
    gw = GRID_W
    win = WIN_ROWS * gw
    lane = lax.broadcasted_iota(jnp.int32, (1, 2 * hd), 1)
    left = lane < hd

    def rows_body(gi, carry):
        slices, scores = [], []
        for i in range(ROW_UNROLL):
            r = gi * ROW_UNROLL + i
            rstart = jnp.clip(r - WIN_ROWS // 2, 0, rows - WIN_ROWS)
            dr0 = rstart - r + (WIN_ROWS - 1)
            qs = pl.ds(pl.multiple_of(r * gw, gw), gw)
            ks = pl.ds(pl.multiple_of(rstart * gw, gw), win)
            slices.append((qs, ks))
            q = q_ref[qs, :]
            zero = jnp.zeros_like(q)
            q2 = jnp.concatenate([jnp.where(left, q, zero), jnp.where(left, zero, q)], axis=0)
            s = lax.dot_general(q2, k_ref[ks, :], (((1,), (1,)), ((), ())), preferred_element_type=F32)
            scores.append(s + bias_ref[dr0])
        for (qs, ks), s in zip(slices, scores):
            m = jnp.max(s, axis=-1, keepdims=True)
            p = jnp.exp2(s - m)
            l = jnp.sum(p, axis=-1, keepdims=True)
            o2 = jnp.dot(p.astype(BF16), v_ref[ks, :], preferred_element_type=F32) / l
            o = jnp.where(left, o2[:gw], o2[gw:]) * za_ref[qs, :].astype(F32)
            o_ref[qs, :] = o.astype(BF16)
        return carry

    lax.fori_loop(0, rows // ROW_UNROLL, rows_body, 0)


def _attention(proj3, bias, *, cols, aw, hd):
    b, l, _ = proj3.shape
    rows = l // GRID_W
    nhp = aw // (2 * hd)
    assert rows % ROW_UNROLL == 0
    col = lambda off: pl.BlockSpec((None, l, 2 * hd), lambda h, i, off=off: (i, 0, off + h))
    kern = functools.partial(_attn_kernel, rows=rows, hd=hd)
    return pl.pallas_call(
        kern,
        grid=(nhp, b),
        in_specs=[col(cols[0]), col(cols[1]), col(cols[2]), col(cols[3]),
                  pl.BlockSpec((None,) + bias.shape[1:], lambda h, i: (h, 0, 0, 0))],
        out_specs=pl.BlockSpec((None, l, 2 * hd), lambda h, i: (i, 0, h)),
        out_shape=jax.ShapeDtypeStruct((b, l, aw), BF16),
        compiler_params=_params("arbitrary", "arbitrary"),
        name="attention",
    )(proj3, proj3, proj3, proj3, bias)


def _gelu_tanh(x):
    c = math.sqrt(2.0 / math.pi)
    return 0.5 * x * (1.0 + jnp.tanh(c * (x + 0.044715 * (x * x * x))))


def _merge_kernel(yt_ref, ut_ref, zt_ref, oa_ref, gs_ref, ga_ref, d_ref, bg_ref, wg_ref, wbs_ref, wba_ref, m_ref):
    tm = m_ref.shape[0]
    sub = tm // MERGE_SPLIT
    parts = [slice(i * sub, (i + 1) * sub) for i in range(MERGE_SPLIT)]
    acts = [_gelu_tanh(yt_ref[:, p].astype(F32) + d_ref[...] * ut_ref[:, p].astype(F32)) for p in parts]
    mas = [jnp.dot(oa_ref[p, :], wba_ref[...], preferred_element_type=F32) for p in parts]
    gates = [jnp.dot(wg_ref[...], a.astype(BF16), preferred_element_type=F32) + bg_ref[...] for a in acts]
    osts = [(a * _sigmoid(g) * zt_ref[:, p].astype(F32)).astype(BF16) for a, g, p in zip(acts, gates, parts)]
    mss = [lax.dot_general(o, wbs_ref[...], (((0,), (0,)), ((), ())), preferred_element_type=F32) for o in osts]
    for p, ms, ma in zip(parts, mss, mas):
        m_ref[p, :] = (gs_ref[p, :].astype(F32) * ms + ga_ref[p, :].astype(F32) * ma).astype(BF16)


def _outproj_kernel(m_ref, x_ref, wo_ref, out_ref):
    out_ref[...] = x_ref[...] + jnp.dot(m_ref[...], wo_ref[...], preferred_element_type=F32)


def _const_spec(shape):
    return pl.BlockSpec(shape, lambda i: (0,) * len(shape), pipeline_mode=pl.Buffered(1))


def _merge(yt, projt, oa, proj, dcol, bgcol, wgt, wbs, wba, *, sw, aw, tm):
    t = oa.shape[0]
    d = wbs.shape[1]
    assert (4 * aw) % d == 0
    n_gs = (4 * aw) // d
    return pl.pallas_call(
        _merge_kernel,
        grid=(t // tm,),
        in_specs=[
            pl.BlockSpec((sw, tm), lambda i: (0, i)),
            pl.BlockSpec((sw, tm), lambda i: (0, i)),
            pl.BlockSpec((sw, tm), lambda i: (1, i)),
            pl.BlockSpec((tm, aw), lambda i: (i, 0)),
            pl.BlockSpec((tm, d), lambda i: (i, n_gs)),
            pl.BlockSpec((tm, d), lambda i: (i, n_gs + 1)),
            _const_spec((sw, 1)), _const_spec((sw, 1)),
            _const_spec(wgt.shape), _const_spec(wbs.shape), _const_spec(wba.shape),
        ],
        out_specs=pl.BlockSpec((tm, d), lambda i: (i, 0)),
        out_shape=jax.ShapeDtypeStruct((t, d), BF16),
        compiler_params=_params("arbitrary"),
        name="merge",
    )(yt, projt, projt, oa, proj, proj, dcol, bgcol, wgt, wbs, wba)


def _outproj(m, x, wo, *, tm):
    t, d = x.shape
    return pl.pallas_call(
        _outproj_kernel,
        grid=(t // tm,),
        in_specs=[pl.BlockSpec((tm, d), lambda i: (i, 0)), pl.BlockSpec((tm, d), lambda i: (i, 0)),
                  _const_spec(wo.shape)],
        out_specs=pl.BlockSpec((tm, d), lambda i: (i, 0)),
        out_shape=jax.ShapeDtypeStruct((t, d), x.dtype),
        compiler_params=_params("arbitrary"),
        name="outproj",
    )(m, x, wo)


def _tile(n, pref):
    t = min(n, pref)
    while n % t:
        t //= 2
    return t


def _layer(xs, norm_g, w_in, a_re, a_im, log_dt, b_re, b_im, c_re, c_im, d_skip, w_glu, b_glu,
           q_g, k_g, rel_bias, w_bs, w_ba, w_out):
    d_model = w_in.shape[0]
    sw = d_skip.shape[0]
    groups, p_dim = a_re.shape[1], a_re.shape[2]
    h_dim = sw // groups
    hd = q_g.shape[0]
    aw = w_ba.shape[0]
    lc = CHUNK
    assert h_dim * lc % LANES == 0 and lc * 2 == LANES and h_dim % 2 == 0

    wr = w_in.astype(BF16)
    wt = wr[:, :2 * sw].T
    tn = _tile(math.gcd(sw, aw), INPROJ_TN)
    acts_t = ["none"] * (sw // tn) + ["silu"] * (sw // tn)
    acts_r = (["qnorm"] * (aw // tn) + ["knorm"] * (aw // tn) + ["none"] * (aw // tn) + ["silu"] * (aw // tn)
              + ["sigmoid"] * (2 * d_model // tn))
    pair = 2 * hd
    attn_cols = tuple(n * aw // pair for n in range(4))
    g2 = norm_g.reshape(1, d_model).astype(F32)
    dup = lambda v: jnp.concatenate([v, v], axis=-1)
    abase = jnp.stack([a_re, a_im, jnp.broadcast_to(log_dt[..., None], a_re.shape)], axis=1)
    abase = abase.transpose(2, 0, 1, 3).astype(F32)
    arow = dup(abase)[:, :, :, None, :]
    bt = dup(jnp.stack([b_re, b_im], axis=1).transpose(2, 0, 1, 4, 3).astype(F32))
    cc = jnp.stack([c_re, c_im], axis=1).transpose(2, 0, 1, 3, 4).astype(F32)
    c2 = dup(cc)
    ct = cc.transpose(0, 1, 2, 4, 3)
    assert tn % hd == 0
    qg_row = jnp.tile(q_g.astype(F32) * (hd ** -0.5 * LOG2E), tn // hd).reshape(1, tn)
    kg_row = jnp.tile(k_g.astype(F32), tn // hd).reshape(1, tn)
    dcol = d_skip.astype(F32).reshape(sw, 1)
    bgcol = b_glu.astype(F32).reshape(sw, 1)
    wgt = w_glu.astype(BF16).T
    wbs = w_bs.astype(BF16)
    wba = w_ba.astype(BF16)
    wo = w_out.astype(BF16)

    w_t, f_t, e_t, dec = _ssm_ops(arow, bt, c2, ct, groups=groups, h_dim=h_dim, p_dim=p_dim)
    bias = _attn_bias(rel_bias.astype(F32))

    projts, projs, u3s, ncs = [], [], [], []
    for x in xs:
        b, l, _ = x.shape
        t = b * l
        assert l % (GRID_W * ROW_UNROLL) == 0 and l % (WIN_ROWS * GRID_W) == 0
        assert (l // lc) & (l // lc - 1) == 0 and l // lc <= 2 ** SCAN_LEVELS
        projt, proj = _inproj(x.reshape(t, d_model), g2, wt, wr, qg_row, kg_row, acts_t=acts_t, acts_r=acts_r,
                              tm=_tile(t, INPROJ_TM), tn=tn, hd=hd)
        projts.append(projt)
        projs.append(proj)
        u3s.append(_to_chunks(projt, groups=groups, h_dim=h_dim, tk=_tile(t // LANES, RELAYOUT_TK)))
        ncs.append(l // lc)

    y4s = _ssm_apply(u3s, w_t.reshape(groups, h_dim * lc, h_dim * lc), f_t.reshape(groups, h_dim * lc, 2 * LANES),
                     e_t, dec, n_chunks=ncs)

    outs = []
    for x, projt, proj, y4 in zip(xs, projts, projs, y4s):
        b, l, _ = x.shape
        t = b * l
        yt = _from_chunks(y4, groups=groups, h_dim=h_dim, tk=_tile(t // LANES, RELAYOUT_TK))
        oa = _attention(proj.reshape(b, l, -1), bias, cols=attn_cols, aw=aw, hd=hd).reshape(t, aw)
        m = _merge(yt, projt, oa, proj, dcol, bgcol, wgt, wbs, wba, sw=sw, aw=aw, tm=_tile(t, MERGE_TM))
        out = _outproj(m, x.reshape(t, d_model), wo, tm=_tile(t, OUTPROJ_TM))
        outs.append(out.reshape(b, l, d_model))
    return outs


def kernel(x_prompt, x_sample, norm_g, w_in, ssm_a_re, ssm_a_im, ssm_log_dt, ssm_b_re, ssm_b_im, ssm_c_re, ssm_c_im, ssm_d, w_glu, b_glu, q_norm_g, k_norm_g, rel_bias, w_branch_ssm, w_branch_attn, w_out):
    xs = [x_prompt, x_sample]
    for layer in range(norm_g.shape[0]):
        xs = _layer(xs, norm_g[layer], w_in[layer], ssm_a_re[layer], ssm_a_im[layer], ssm_log_dt[layer],
                    ssm_b_re[layer], ssm_b_im[layer], ssm_c_re[layer], ssm_c_im[layer], ssm_d[layer],
                    w_glu[layer], b_glu[layer], q_norm_g[layer], k_norm_g[layer], rel_bias[layer],
                    w_branch_ssm[layer], w_branch_attn[layer], w_out[layer])
    return (xs[0], xs[1])
```

```python
import functools
import math

import numpy as np
import jax
import jax.numpy as jnp
from jax import lax
from jax.experimental import pallas as pl
from jax.experimental.pallas import tpu as pltpu

NORM_EPS = 1e-6
MASK_VALUE = -1e30
LOG2E = math.log2(math.e)
GRID_W = 64
WIN_ROWS = 8
WIN_COLS = 16
CHUNK = 64
LANES = 128
ROW_UNROLL = 8
NORM_SPLIT = 4
MERGE_SPLIT = 2
SSM_GROUPS_PER_STEP = 4
SCAN_LEVELS = 7
VMEM_LIMIT_BYTES = 56 * 1024 * 1024
INPROJ_TM = 1024
INPROJ_TN = 1024
MERGE_TM = 512
OUTPROJ_TM = 512
RELAYOUT_TK = 8

F32 = jnp.float32
BF16 = jnp.bfloat16


def _params(*sem):
    return pltpu.CompilerParams(dimension_semantics=sem, vmem_limit_bytes=VMEM_LIMIT_BYTES)


def _sigmoid(x):
    return 0.5 * jnp.tanh(0.5 * x) + 0.5


def _head_norm(r, gain, hd):
    lane = lax.broadcasted_iota(jnp.int32, (1, 2 * hd), 1)
    left = lane < hd
    out = []
    for c in range(r.shape[1] // (2 * hd)):
        x = r[:, c * 2 * hd:(c + 1) * 2 * hd]
        x2 = x * x
        sa = jnp.sum(jnp.where(left, x2, 0.0), axis=-1, keepdims=True)
        sb = jnp.sum(jnp.where(left, 0.0, x2), axis=-1, keepdims=True)
        ms = jnp.where(left, sa, sb) * (1.0 / hd)
        out.append(x * lax.rsqrt(ms + NORM_EPS) * gain[:, c * 2 * hd:(c + 1) * 2 * hd])
    return jnp.concatenate(out, axis=1)


def _activate(r, act, gains, hd):
    if act == "silu":
        return r * _sigmoid(r)
    if act == "sigmoid":
        return _sigmoid(r)
    if act in gains:
        return _head_norm(r, gains[act][...], hd)
    return r


ACTS = ("none", "silu", "sigmoid", "qnorm", "knorm")


def _inproj_kernel(x_ref, g_ref, wt_ref, w_ref, qg_ref, kg_ref, outt_ref, out_ref, h_ref, *, acts_t, acts_r, hd):
    j = pl.program_id(1)
    n_t = len(acts_t)
    gains = {"qnorm": qg_ref, "knorm": kg_ref}

    @pl.when(j == 0)
    def _():
        x = x_ref[...]
        ms = jnp.mean(x * x, axis=-1, keepdims=True)
        h_ref[...] = (x * lax.rsqrt(ms + NORM_EPS) * g_ref[...]).astype(BF16)

    def tiles_with(acts, act, base):
        cond = None
        for idx, a in enumerate(acts):
            if a == act:
                c = j == base + idx
                cond = c if cond is None else cond | c
        return cond

    for act in ACTS:
        cond = tiles_with(acts_t, act, 0)
        if cond is not None:
            assert act not in gains
            @pl.when(cond)
            def _(act=act):
                r = lax.dot_general(wt_ref[...], h_ref[...], (((1,), (1,)), ((), ())), preferred_element_type=F32)
                outt_ref[...] = _activate(r, act, gains, hd).astype(BF16)
        cond = tiles_with(acts_r, act, n_t)
        if cond is not None:
            @pl.when(cond)
            def _(act=act):
                n_sub = NORM_SPLIT if act in gains else 1
                sub = h_ref.shape[0] // n_sub
                for p in range(n_sub):
                    rows = slice(p * sub, (p + 1) * sub)
                    r = jnp.dot(h_ref[rows, :], w_ref[...], preferred_element_type=F32)
                    out_ref[rows, :] = _activate(r, act, gains, hd).astype(BF16)


def _inproj(x, g, wt, w, qg, kg, *, acts_t, acts_r, tm, tn, hd):
    t, d = x.shape
    n_t, n_r = len(acts_t), len(acts_r)
    assert wt.shape == (n_t * tn, d) and w.shape == (d, (n_t + n_r) * tn) and set(acts_t) | set(acts_r) <= set(ACTS)
    kern = functools.partial(_inproj_kernel, acts_t=tuple(acts_t), acts_r=tuple(acts_r), hd=hd)
    return pl.pallas_call(
        kern,
        grid=(t // tm, n_t + n_r),
        in_specs=[
            pl.BlockSpec((tm, d), lambda i, j: (i, 0)),
            pl.BlockSpec((1, d), lambda i, j: (0, 0)),
            pl.BlockSpec((tn, d), lambda i, j: (jnp.minimum(j, n_t - 1), 0)),
            pl.BlockSpec((d, tn), lambda i, j: (0, jnp.maximum(j, n_t))),
            pl.BlockSpec((1, tn), lambda i, j: (0, 0)),
            pl.BlockSpec((1, tn), lambda i, j: (0, 0)),
        ],
        out_specs=[
            pl.BlockSpec((tn, tm), lambda i, j: (jnp.minimum(j, n_t - 1), i)),
            pl.BlockSpec((tm, tn), lambda i, j: (i, jnp.maximum(j - n_t, 0))),
        ],
        out_shape=[jax.ShapeDtypeStruct((n_t * tn, t), BF16), jax.ShapeDtypeStruct((t, n_r * tn), BF16)],
        scratch_shapes=[pltpu.VMEM((tm, d), BF16)],
        compiler_params=_params("arbitrary", "arbitrary"),
        name="inproj",
    )(x, g, wt, w, qg, kg)


def _cmul(ar, ai, br, bi):
    return ar * br - ai * bi, ar * bi + ai * br


def _hi_lo(x):
    if x.dtype == BF16:
        return x, None
    hi = x.astype(BF16)
    return hi, (x - hi.astype(F32)).astype(BF16)


def _dot_hi_lo(a, b):
    ah, al = _hi_lo(a)
    bh, bl = _hi_lo(b)
    acc = jnp.dot(ah, bh, preferred_element_type=F32)
    if bl is not None:
        acc = acc + jnp.dot(ah, bl, preferred_element_type=F32)
    if al is not None:
        acc = acc + jnp.dot(al, bh, preferred_element_type=F32)
    return acc


def _discretize(a_re, a_im, log_dt):
    lam_re = jnp.minimum(a_re, -1e-4)
    lam_im = a_im
    dt = jnp.exp(log_dt)
    mag = jnp.exp(lam_re * dt)
    abar_re = mag * jnp.cos(lam_im * dt)
    abar_im = mag * jnp.sin(lam_im * dt)
    den = lam_re * lam_re + lam_im * lam_im
    f_re = ((abar_re - 1.0) * lam_re + abar_im * lam_im) / den
    f_im = (abar_im * lam_re - (abar_re - 1.0) * lam_im) / den
    return abar_re, abar_im, f_re, f_im


def _squarings(ar, ai, n):
    out = [(ar, ai)]
    for _ in range(n - 1):
        ar, ai = _cmul(ar, ai, ar, ai)
        out.append((ar, ai))
    return out


def _power_table(sq, exps, shape):
    pr = pi = None
    for k, (ar, ai) in enumerate(sq):
        bit = ((exps >> k) & 1) == 1
        fr = jnp.broadcast_to(jnp.where(bit, ar, 1.0), shape)
        fi = jnp.broadcast_to(jnp.where(bit, ai, 0.0), shape)
        pr, pi = (fr, fi) if pr is None else _cmul(pr, pi, fr, fi)
    return pr, pi


def _ssm_ops_kernel(arow_ref, bt_ref, c_ref, ct_ref, w_ref, f_ref, e_ref, dec_ref, k_ref, *, h_dim, p_dim):
    lc = CHUNK
    nbits = lc.bit_length()
    lane = lax.broadcasted_iota(jnp.int32, (1, LANES), 1)
    left = lane < lc
    sub = lax.broadcasted_iota(jnp.int32, (lc, 1), 0)
    eye = lax.broadcasted_iota(jnp.int32, (p_dim, 1), 0) == lane

    q2 = []
    rmat = []
    bb_rows = []
    sq_rows = []
    for d in range(2):
        abr, abi, fr, fi = _discretize(arow_ref[d, 0], arow_ref[d, 1], arow_ref[d, 2])
        bbr, bbi = _cmul(fr, fi, bt_ref[d, 0], bt_ref[d, 1])
        bb_rows.append((bbr, bbi))
        sq_row = _squarings(abr, abi, nbits + SCAN_LEVELS)
        sq_rows.append(sq_row)
        cr, ci = c_ref[d, 0], c_ref[d, 1]
        blocks = []
        for h in range(h_dim):
            qr, qi = _cmul(cr, ci, bbr[h:h + 1, :], bbi[h:h + 1, :])
            blocks.append(jnp.where(left, qr, -qi))
        q2.append(jnp.concatenate(blocks, axis=0))
        cabr = jnp.sum(jnp.where(eye, abr, 0.0), axis=-1, keepdims=True)
        cabi = jnp.sum(jnp.where(eye, abi, 0.0), axis=-1, keepdims=True)
        sq_col = _squarings(cabr, cabi, nbits)
        lag = lane - lc
        if d == 0:
            exps, live = jnp.maximum(lag, 0), lag >= 0
        else:
            exps, live = jnp.maximum(-lag, 0), lag <= 0
        vr, vi = _power_table(sq_col, exps, (p_dim, LANES))
        rmat.append(jnp.concatenate([jnp.where(live, vr, 0.0), jnp.where(live, vi, 0.0)], axis=0))
        if d == 0:
            er, ei = _cmul(jnp.where(left, pltpu.roll(vr, lc, 1), vr), jnp.where(left, pltpu.roll(vi, lc, 1), vi),
                           cabr, cabi)
        else:
            er, ei = jnp.where(left, vr, pltpu.roll(vr, lc, 1)), jnp.where(left, vi, pltpu.roll(vi, lc, 1))
        n_rep = (h_dim * lc) // LANES
        er = jnp.concatenate([er] * n_rep, axis=1)
        ei = jnp.concatenate([ei] * n_rep, axis=1)
        rep_r = lax.broadcasted_iota(jnp.int32, (h_dim, h_dim * lc), 0)
        rep_c = lax.broadcasted_iota(jnp.int32, (h_dim, h_dim * lc), 1)
        rep = (_slot_channel(rep_c >> (lc.bit_length() - 1), h_dim) == rep_r).astype(BF16)
        cer = _dot_hi_lo(ct_ref[d, 0], rep)
        cei = _dot_hi_lo(ct_ref[d, 1], rep)
        gr, gi = _cmul(cer, cei, er, ei)
        e_ref[2 * d * p_dim:(2 * d + 1) * p_dim, :] = gr.astype(BF16)
        e_ref[(2 * d + 1) * p_dim:(2 * d + 2) * p_dim, :] = (-gi).astype(BF16)
        for k in range(SCAN_LEVELS):
            dr_, di_ = sq_row[nbits - 1 + k]
            dec_ref[(d * SCAN_LEVELS + k) * 2:(d * SCAN_LEVELS + k) * 2 + 1, :] = dr_
            dec_ref[(d * SCAN_LEVELS + k) * 2 + 1:(d * SCAN_LEVELS + k) * 2 + 2, :] = jnp.where(left, -di_, di_)

    kbi = _dot_hi_lo(q2[0], rmat[0]) + _dot_hi_lo(q2[1], rmat[1])
    k_ref[0] = kbi
    k_ref[1] = pltpu.roll(kbi, lc, 1)

    fexp = (lc - 1 - sub, sub)
    vrow = [_power_table(sq_rows[d][:nbits], fexp[d], (lc, LANES)) for d in range(2)]

    hh = h_dim // 2
    pre_roll_left = ((lane + sub) & (LANES - 1)) < lc

    def per_h(h, carry):
        slot = 2 * (h % hh) + h // hh
        for d in range(2):
            bbr, bbi = bb_rows[d]
            sel = (lax.broadcasted_iota(jnp.int32, (h_dim, 1), 0) == h).astype(F32)
            br = jnp.sum(bbr * sel, axis=0, keepdims=True)
            bi = jnp.sum(bbi * sel, axis=0, keepdims=True)
            zr, zi = _cmul(vrow[d][0], vrow[d][1], br, bi)
            f_ref[slot, :, d * LANES:(d + 1) * LANES] = jnp.where(left, zr, zi).astype(BF16)
        for m in range(hh):
            ra = k_ref[1, pl.ds(h * h_dim + m, 1), :]
            rb = k_ref[0, pl.ds(h * h_dim + m + hh, 1), :]
            src = jnp.where(pre_roll_left, jnp.broadcast_to(ra, (lc, LANES)), jnp.broadcast_to(rb, (lc, LANES)))
            tile = pltpu.roll(src, 0, 1, stride=1, stride_axis=0)
            w_ref[slot, :, m * LANES:(m + 1) * LANES] = tile.astype(BF16)
        return carry

    lax.fori_loop(0, h_dim, per_h, 0, unroll=4)


def _ssm_ops(arow, bt, c2, ct, *, groups, h_dim, p_dim):
    lc = CHUNK
    kern = functools.partial(_ssm_ops_kernel, h_dim=h_dim, p_dim=p_dim)
    blk = lambda shape: pl.BlockSpec((None,) + shape, lambda g: (g,) + (0,) * len(shape))
    return pl.pallas_call(
        kern,
        grid=(groups,),
        in_specs=[blk((2, 3, 1, LANES)), blk((2, 2, h_dim, LANES)),
                  blk((2, 2, h_dim, LANES)), blk((2, 2, p_dim, h_dim))],
        out_specs=[blk((h_dim, lc, h_dim * lc)), blk((h_dim, lc, 2 * LANES)),
                   blk((4 * p_dim, h_dim * lc)), blk((4 * SCAN_LEVELS, LANES))],
        out_shape=[jax.ShapeDtypeStruct((groups, h_dim, lc, h_dim * lc), BF16),
                   jax.ShapeDtypeStruct((groups, h_dim, lc, 2 * LANES), BF16),
                   jax.ShapeDtypeStruct((groups, 4 * p_dim, h_dim * lc), BF16),
                   jax.ShapeDtypeStruct((groups, 4 * SCAN_LEVELS, LANES), F32)],
        scratch_shapes=[pltpu.VMEM((2, h_dim * h_dim, LANES), F32)],
        compiler_params=_params("arbitrary"),
        name="ssm_ops",
    )(arow, bt, c2, ct)


def _decay_mul(x, dec_ref, d, level):
    row = (d * SCAN_LEVELS + level) * 2
    return x * dec_ref[row:row + 1, :] + pltpu.roll(x, CHUNK, 1) * dec_ref[row + 1:row + 2, :]


def _pair_scan(z, dec_ref, d, n_pairs):
    m = z.shape[0]
    cidx = lax.broadcasted_iota(jnp.int32, (m, 1), 0) & (n_pairs - 1)

    def shifted(x, sh):
        if d == 0:
            return jnp.where(cidx >= sh, pltpu.roll(x, sh, 0), 0.0)
        return jnp.where(cidx < n_pairs - sh, pltpu.roll(x, m - sh, 0), 0.0)

    x = z
    for k in range(n_pairs.bit_length() - 1):
        x = x + _decay_mul(shifted(x, 1 << k), dec_ref, d, k + 1)
    return shifted(x, 1)


def _lane_halves(a, b, left):
    ax = a.ndim - 1
    return jnp.where(left, a, pltpu.roll(b, CHUNK, ax)), jnp.where(left, pltpu.roll(a, CHUNK, ax), b)


def _ssm_apply_kernel(*refs, n_seq, n_chunks):
    u_refs = refs[:n_seq]
    w_ref, f_ref, e_ref, dec_ref = refs[n_seq:n_seq + 4]
    y_refs = refs[n_seq + 4:]
    units = [(g, i) for g in range(SSM_GROUPS_PER_STEP) for i in range(n_seq)]
    us = [u_refs[i][g].reshape(2 * u_refs[i].shape[2], u_refs[i].shape[3]) for g, i in units]
    zs = [jnp.dot(u, f_ref[g], preferred_element_type=F32) for u, (g, _) in zip(us, units)]
    accs = [jnp.dot(u, w_ref[g], preferred_element_type=F32) for u, (g, _) in zip(us, units)]
    states = []
    for z, (g, i) in zip(zs, units):
        dec = dec_ref.at[g]
        nc = n_chunks[i]
        half = z.shape[0] // 2
        ze, zo = z[:half], z[half:]
        pf = _pair_scan(_decay_mul(ze[:, :LANES], dec, 0, 0) + zo[:, :LANES], dec, 0, nc // 2)
        sf = jnp.concatenate([pf, _decay_mul(pf, dec, 0, 0) + ze[:, :LANES]], axis=0)
        pb = _pair_scan(ze[:, LANES:] + _decay_mul(zo[:, LANES:], dec, 1, 0), dec, 1, nc // 2)
        sb = jnp.concatenate([_decay_mul(pb, dec, 1, 0) + zo[:, LANES:], pb], axis=0)
        states.append(jnp.concatenate([sf, sb], axis=1).astype(BF16))
    for (g, i), acc, s in zip(units, accs, states):
        acc = acc + jnp.dot(s, e_ref[g], preferred_element_type=F32)
        y_refs[i][g] = acc.astype(BF16).reshape(y_refs[i].shape[1:])


def _slot_channel(slot, h_dim):
    return (slot >> 1) + (h_dim // 2) * (slot & 1)


def _to_chunks_kernel(x_ref, o_ref, *, groups, h_dim):
    tk = o_ref.shape[2]
    hh = h_dim // 2
    left = lax.broadcasted_iota(jnp.int32, (1, 1, LANES), 2) < CHUNK
    x = x_ref[...].astype(F32).reshape(groups, 2, hh, tk * LANES)
    ev, od = [], []
    for k in range(tk):
        e, o = _lane_halves(x[:, 0, :, k * LANES:(k + 1) * LANES], x[:, 1, :, k * LANES:(k + 1) * LANES], left)
        ev.append(e)
        od.append(o)
    for p, parts in enumerate((ev, od)):
        o_ref[:, p] = jnp.stack(parts, axis=1).reshape(groups, tk, hh * LANES).astype(o_ref.dtype)


def _from_chunks_kernel(x_ref, o_ref, *, groups, h_dim):
    tk = x_ref.shape[2]
    hh = h_dim // 2
    left = lax.broadcasted_iota(jnp.int32, (1, 1, LANES), 2) < CHUNK
    ye = x_ref[:, 0].astype(F32).reshape(groups, tk, hh, LANES)
    yo = x_ref[:, 1].astype(F32).reshape(groups, tk, hh, LANES)
    lo, hi = zip(*[_lane_halves(ye[:, k], yo[:, k], left) for k in range(tk)])
    y = jnp.stack([jnp.concatenate(lo, axis=-1), jnp.concatenate(hi, axis=-1)], axis=1)
    o_ref[...] = y.reshape(groups * h_dim, tk * LANES).astype(o_ref.dtype)


def _to_chunks(xt, *, groups, h_dim, tk):
    t = xt.shape[1]
    return pl.pallas_call(
        functools.partial(_to_chunks_kernel, groups=groups, h_dim=h_dim),
        grid=(t // (tk * LANES),),
        in_specs=[pl.BlockSpec((groups * h_dim, tk * LANES), lambda i: (0, i))],
        out_specs=pl.BlockSpec((groups, 2, tk, h_dim * CHUNK), lambda i: (0, 0, i, 0)),
        out_shape=jax.ShapeDtypeStruct((groups, 2, t // LANES, h_dim * CHUNK), BF16),
        compiler_params=_params("arbitrary"),
        name="to_chunks",
    )(xt)


def _from_chunks(y4, *, groups, h_dim, tk):
    t = y4.shape[2] * LANES
    return pl.pallas_call(
        functools.partial(_from_chunks_kernel, groups=groups, h_dim=h_dim),
        grid=(t // (tk * LANES),),
        in_specs=[pl.BlockSpec((groups, 2, tk, h_dim * CHUNK), lambda i: (0, 0, i, 0))],
        out_specs=pl.BlockSpec((groups * h_dim, tk * LANES), lambda i: (0, i)),
        out_shape=jax.ShapeDtypeStruct((groups * h_dim, t), BF16),
        compiler_params=_params("arbitrary"),
        name="from_chunks",
    )(y4)


def _ssm_apply(u4s, w, f, e, dec, *, n_chunks):
    groups = w.shape[0]
    assert groups % SSM_GROUPS_PER_STEP == 0
    blk = lambda shape: pl.BlockSpec((SSM_GROUPS_PER_STEP,) + shape, lambda g: (g,) + (0,) * len(shape))
    kern = functools.partial(_ssm_apply_kernel, n_seq=len(u4s), n_chunks=tuple(n_chunks))
    return pl.pallas_call(
        kern,
        grid=(groups // SSM_GROUPS_PER_STEP,),
        in_specs=[blk(u.shape[1:]) for u in u4s]
        + [blk(w.shape[1:]), blk(f.shape[1:]), blk(e.shape[1:]), blk(dec.shape[1:])],
        out_specs=[blk(u.shape[1:]) for u in u4s],
        out_shape=[jax.ShapeDtypeStruct(u.shape, BF16) for u in u4s],
        compiler_params=_params("arbitrary"),
        name="ssm_apply",
    )(*u4s, w, f, e, dec)


def _attn_bias_kernel(rb_ref, out_ref, pair_ref, *, hd):
    gw = GRID_W
    cq = lax.broadcasted_iota(jnp.int32, (gw, LANES), 0)
    lane = lax.broadcasted_iota(jnp.int32, (gw, LANES), 1)
    ck = lane & (gw - 1)
    cstart = jnp.clip(cq - WIN_COLS // 2, 0, gw - WIN_COLS)
    valid = (ck >= cstart) & (ck < cstart + WIN_COLS)
    pre_roll_left = ((lane + cq) & (LANES - 1)) < gw
    n_dr = 2 * WIN_ROWS - 1
    for head in range(2):
        rows = rb_ref[head]
        rows_l = pltpu.roll(rows, LANES - (WIN_COLS - 1), 1)
        rows_r = pltpu.roll(rows, gw - (WIN_COLS - 1), 1)
        for dr in range(n_dr - 1):
            src = jnp.where(pre_roll_left, jnp.broadcast_to(rows_l[dr:dr + 1], (gw, LANES)),
                            jnp.broadcast_to(rows_r[dr + 1:dr + 2], (gw, LANES)))
            tile = pltpu.roll(src, 0, 1, stride=1, stride_axis=0)
            pair_ref[head, dr] = jnp.where(valid, tile * LOG2E, MASK_VALUE)
    for dr0 in range(WIN_ROWS):
        for head in range(2):
            for wq in range(WIN_ROWS // 2):
                out_ref[dr0, head * gw:(head + 1) * gw, wq * LANES:(wq + 1) * LANES] = pair_ref[head, dr0 + 2 * wq]


def _attn_bias(rel_bias):
    nh, n_dr, n_dc = rel_bias.shape
    gw = GRID_W
    rel_bias = jnp.pad(rel_bias, ((0, 0), (0, 0), (0, LANES - n_dc)))
    return pl.pallas_call(
        functools.partial(_attn_bias_kernel, hd=gw),
        grid=(nh // 2,),
        in_specs=[pl.BlockSpec((2, n_dr, LANES), lambda h: (h, 0, 0))],
        out_specs=pl.BlockSpec((None, WIN_ROWS, 2 * gw, WIN_ROWS * gw), lambda h: (h, 0, 0, 0)),
        out_shape=jax.ShapeDtypeStruct((nh // 2, WIN_ROWS, 2 * gw, WIN_ROWS * gw), F32),
        scratch_shapes=[pltpu.VMEM((2, 2 * WIN_ROWS - 2, gw, LANES), F32)],
        compiler_params=_params("arbitrary"),
        name="attn_bias",
    )(rel_bias)


def _attn_kernel(q_ref, k_ref, v_ref, za_ref, bias_ref, o_ref, *, rows, hd):
    gw = GRID_W
    win = WIN_ROWS * gw
    lane = lax.broadcasted_iota(jnp.int32, (1, 2 * hd), 1)
    left = lane < hd

    def rows_body(gi, carry):
        slices, scores = [], []
        for i in range(ROW_UNROLL):
            r = gi * ROW_UNROLL + i
            rstart = jnp.clip(r - WIN_ROWS // 2, 0, rows - WIN_ROWS)
            dr0 = rstart - r + (WIN_ROWS - 1)
            qs = pl.ds(pl.multiple_of(r * gw, gw), gw)
            ks = pl.ds(pl.multiple_of(rstart * gw, gw), win)
            slices.append((qs, ks))
            q = q_ref[qs, :]
            zero = jnp.zeros_like(q)
            q2 = jnp.concatenate([jnp.where(left, q, zero), jnp.where(left, zero, q)], axis=0)
            s = lax.dot_general(q2, k_ref[ks, :], (((1,), (1,)), ((), ())), preferred_element_type=F32)
            scores.append(s + bias_ref[dr0])
        for (qs, ks), s in zip(slices, scores):
            m = jnp.max(s, axis=-1, keepdims=True)
            p = jnp.exp2(s - m)
            l = jnp.sum(p, axis=-1, keepdims=True)
            o2 = jnp.dot(p.astype(BF16), v_ref[ks, :], preferred_element_type=F32) / l
            o = jnp.where(left, o2[:gw], o2[gw:]) * za_ref[qs, :].astype(F32)
            o_ref[qs, :] = o.astype(BF16)
        return carry

    lax.fori_loop(0, rows // ROW_UNROLL, rows_body, 0)


def _attention(proj3, bias, *, cols, aw, hd):
    b, l, _ = proj3.shape
    rows = l // GRID_W
    nhp = aw // (2 * hd)
    assert rows % ROW_UNROLL == 0
    col = lambda off: pl.BlockSpec((None, l, 2 * hd), lambda h, i, off=off: (i, 0, off + h))
    kern = functools.partial(_attn_kernel, rows=rows, hd=hd)
    return pl.pallas_call(
        kern,
        grid=(nhp, b),
        in_specs=[col(cols[0]), col(cols[1]), col(cols[2]), col(cols[3]),
                  pl.BlockSpec((None,) + bias.shape[1:], lambda h, i: (h, 0, 0, 0))],
        out_specs=pl.BlockSpec((None, l, 2 * hd), lambda h, i: (i, 0, h)),
        out_shape=jax.ShapeDtypeStruct((b, l, aw), BF16),
        compiler_params=_params("arbitrary", "arbitrary"),
        name="attention",
    )(proj3, proj3, proj3, proj3, bias)


def _gelu_tanh(x):
    c = math.sqrt(2.0 / math.pi)
    return 0.5 * x * (1.0 + jnp.tanh(c * (x + 0.044715 * (x * x * x))))


def _merge_kernel(yt_ref, ut_ref, zt_ref, oa_ref, gs_ref, ga_ref, d_ref, bg_ref, wg_ref, wbs_ref, wba_ref, m_ref):
    tm = m_ref.shape[0]
    sub = tm // MERGE_SPLIT
    parts = [slice(i * sub, (i + 1) * sub) for i in range(MERGE_SPLIT)]
    acts = [_gelu_tanh(yt_ref[:, p].astype(F32) + d_ref[...] * ut_ref[:, p].astype(F32)) for p in parts]
    mas = [jnp.dot(oa_ref[p, :], wba_ref[...], preferred_element_type=F32) for p in parts]
    gates = [jnp.dot(wg_ref[...], a.astype(BF16), preferred_element_type=F32) + bg_ref[...] for a in acts]
    osts = [(a * _sigmoid(g) * zt_ref[:, p].astype(F32)).astype(BF16) for a, g, p in zip(acts, gates, parts)]
    mss = [lax.dot_general(o, wbs_ref[...], (((0,), (0,)), ((), ())), preferred_element_type=F32) for o in osts]
    for p, ms, ma in zip(parts, mss, mas):
        m_ref[p, :] = (gs_ref[p, :].astype(F32) * ms + ga_ref[p, :].astype(F32) * ma).astype(BF16)


def _outproj_kernel(m_ref, x_ref, wo_ref, out_ref):
    out_ref[...] = x_ref[...] + jnp.dot(m_ref[...], wo_ref[...], preferred_element_type=F32)


def _const_spec(shape):
    return pl.BlockSpec(shape, lambda i: (0,) * len(shape), pipeline_mode=pl.Buffered(1))


def _merge(yt, projt, oa, proj, dcol, bgcol, wgt, wbs, wba, *, sw, aw, tm):
    t = oa.shape[0]
    d = wbs.shape[1]
    assert (4 * aw) % d == 0
    n_gs = (4 * aw) // d
    return pl.pallas_call(
        _merge_kernel,
        grid=(t // tm,),
        in_specs=[
            pl.BlockSpec((sw, tm), lambda i: (0, i)),
            pl.BlockSpec((sw, tm), lambda i: (0, i)),
            pl.BlockSpec((sw, tm), lambda i: (1, i)),
            pl.BlockSpec((tm, aw), lambda i: (i, 0)),
            pl.BlockSpec((tm, d), lambda i: (i, n_gs)),
            pl.BlockSpec((tm, d), lambda i: (i, n_gs + 1)),
            _const_spec((sw, 1)), _const_spec((sw, 1)),
            _const_spec(wgt.shape), _const_spec(wbs.shape), _const_spec(wba.shape),
        ],
        out_specs=pl.BlockSpec((tm, d), lambda i: (i, 0)),
        out_shape=jax.ShapeDtypeStruct((t, d), BF16),
        compiler_params=_params("arbitrary"),
        name="merge",
    )(yt, projt, projt, oa, proj, proj, dcol, bgcol, wgt, wbs, wba)


def _outproj(m, x, wo, *, tm):
    t, d = x.shape
    return pl.pallas_call(
        _outproj_kernel,
        grid=(t // tm,),
        in_specs=[pl.BlockSpec((tm, d), lambda i: (i, 0)), pl.BlockSpec((tm, d), lambda i: (i, 0)),
                  _const_spec(wo.shape)],
        out_specs=pl.BlockSpec((tm, d), lambda i: (i, 0)),
        out_shape=jax.ShapeDtypeStruct((t, d), x.dtype),
        compiler_params=_params("arbitrary"),
        name="outproj",
    )(m, x, wo)


def _tile(n, pref):
    t = min(n, pref)
    while n % t:
        t //= 2
    return t


def _layer(xs, norm_g, w_in, a_re, a_im, log_dt, b_re, b_im, c_re, c_im, d_skip, w_glu, b_glu,
           q_g, k_g, rel_bias, w_bs, w_ba, w_out):
    d_model = w_in.shape[0]
    sw = d_skip.shape[0]
    groups, p_dim = a_re.shape[1], a_re.shape[2]
    h_dim = sw // groups
    hd = q_g.shape[0]
    aw = w_ba.shape[0]
    lc = CHUNK
    assert h_dim * lc % LANES == 0 and lc * 2 == LANES and h_dim % 2 == 0

    wr = w_in.astype(BF16)
    wt = wr[:, :2 * sw].T
    tn = _tile(math.gcd(sw, aw), INPROJ_TN)
    acts_t = ["none"] * (sw // tn) + ["silu"] * (sw // tn)
    acts_r = (["qnorm"] * (aw // tn) + ["knorm"] * (aw // tn) + ["none"] * (aw // tn) + ["silu"] * (aw // tn)
              + ["sigmoid"] * (2 * d_model // tn))
    pair = 2 * hd
    attn_cols = tuple(n * aw // pair for n in range(4))
    g2 = norm_g.reshape(1, d_model).astype(F32)
    dup = lambda v: jnp.concatenate([v, v], axis=-1)
    abase = jnp.stack([a_re, a_im, jnp.broadcast_to(log_dt[..., None], a_re.shape)], axis=1)
    abase = abase.transpose(2, 0, 1, 3).astype(F32)
    arow = dup(abase)[:, :, :, None, :]
    bt = dup(jnp.stack([b_re, b_im], axis=1).transpose(2, 0, 1, 4, 3).astype(F32))
    cc = jnp.stack([c_re, c_im], axis=1).transpose(2, 0, 1, 3, 4).astype(F32)
    c2 = dup(cc)
    ct = cc.transpose(0, 1, 2, 4, 3)
    assert tn % hd == 0
    qg_row = jnp.tile(q_g.astype(F32) * (hd ** -0.5 * LOG2E), tn // hd).reshape(1, tn)
    kg_row = jnp.tile(k_g.astype(F32), tn // hd).reshape(1, tn)
    dcol = d_skip.astype(F32).reshape(sw, 1)
    bgcol = b_glu.astype(F32).reshape(sw, 1)
    wgt = w_glu.astype(BF16).T
    wbs = w_bs.astype(BF16)
    wba = w_ba.astype(BF16)
    wo = w_out.astype(BF16)

    w_t, f_t, e_t, dec = _ssm_ops(arow, bt, c2, ct, groups=groups, h_dim=h_dim, p_dim=p_dim)
    bias = _attn_bias(rel_bias.astype(F32))

    projts, projs, u3s, ncs = [], [], [], []
    for x in xs:
        b, l, _ = x.shape
        t = b * l
        assert l % (GRID_W * ROW_UNROLL) == 0 and l % (WIN_ROWS * GRID_W) == 0
        assert (l // lc) & (l // lc - 1) == 0 and l // lc <= 2 ** SCAN_LEVELS
        projt, proj = _inproj(x.reshape(t, d_model), g2, wt, wr, qg_row, kg_row, acts_t=acts_t, acts_r=acts_r,
                              tm=_tile(t, INPROJ_TM), tn=tn, hd=hd)
        projts.append(projt)
        projs.append(proj)
        u3s.append(_to_chunks(projt, groups=groups, h_dim=h_dim, tk=_tile(t // LANES, RELAYOUT_TK)))
        ncs.append(l // lc)

    y4s = _ssm_apply(u3s, w_t.reshape(groups, h_dim * lc, h_dim * lc), f_t.reshape(groups, h_dim * lc, 2 * LANES),
                     e_t, dec, n_chunks=ncs)

    outs = []
    for x, projt, proj, y4 in zip(xs, projts, projs, y4s):
        b, l, _ = x.shape
        t = b * l
        yt = _from_chunks(y4, groups=groups, h_dim=h_dim, tk=_tile(t // LANES, RELAYOUT_TK))
        oa = _attention(proj.reshape(b, l, -1), bias, cols=attn_cols, aw=aw, hd=hd).reshape(t, aw)
        m = _merge(yt, projt, oa, proj, dcol, bgcol, wgt, wbs, wba, sw=sw, aw=aw, tm=_tile(t, MERGE_TM))
        out = _outproj(m, x.reshape(t, d_model), wo, tm=_tile(t, OUTPROJ_TM))
        outs.append(out.reshape(b, l, d_model))
    return outs


def kernel(x_prompt, x_sample, norm_g, w_in, ssm_a_re, ssm_a_im, ssm_log_dt, ssm_b_re, ssm_b_im, ssm_c_re, ssm_c_im, ssm_d, w_glu, b_glu, q_norm_g, k_norm_g, rel_bias, w_branch_ssm, w_branch_attn, w_out):
    xs = [x_prompt, x_sample]
    for layer in range(norm_g.shape[0]):
        xs = _layer(xs, norm_g[layer], w_in[layer], ssm_a_re[layer], ssm_a_im[layer], ssm_log_dt[layer],
                    ssm_b_re[layer], ssm_b_im[layer], ssm_c_re[layer], ssm_c_im[layer], ssm_d[layer],
                    w_glu[layer], b_glu[layer], q_norm_g[layer], k_norm_g[layer], rel_bias[layer],
                    w_branch_ssm[layer], w_branch_attn[layer], w_out[layer])
    return (xs[0], xs[1])
```

```python
import functools
import math

import jax
import jax.numpy as jnp
from jax import lax
from jax.experimental import pallas as pl
from jax.experimental.pallas import tpu as pltpu

NORM_EPS = 1e-6
MASK_VALUE = -1e30
LOG2E = math.log2(math.e)
GRID_W = 64
WIN_ROWS = 8
WIN_COLS = 16
CHUNK = 64
LANES = 128
ROW_UNROLL = 8
NORM_SPLIT = 4
MERGE_SPLIT = 2
SSM_GROUPS_PER_STEP = 4
SCAN_LEVELS = 7
VMEM_LIMIT_BYTES = 56 * 1024 * 1024
INPROJ_TM = 1024
INPROJ_TN = 1024
MERGE_TM = 512
OUTPROJ_TM = 512
RELAYOUT_TK = 8

F32 = jnp.float32
BF16 = jnp.bfloat16


def _params(*sem):
    return pltpu.CompilerParams(dimension_semantics=sem, vmem_limit_bytes=VMEM_LIMIT_BYTES)


def _sigmoid(x):
    return 0.5 * jnp.tanh(0.5 * x) + 0.5


def _head_norm(r, gain, hd):
    lane = lax.broadcasted_iota(jnp.int32, (1, 2 * hd), 1)
    left = lane < hd
    out = []
    for c in range(r.shape[1] // (2 * hd)):
        x = r[:, c * 2 * hd:(c + 1) * 2 * hd]
        x2 = x * x
        sa = jnp.sum(jnp.where(left, x2, 0.0), axis=-1, keepdims=True)
        sb = jnp.sum(jnp.where(left, 0.0, x2), axis=-1, keepdims=True)
        ms = jnp.where(left, sa, sb) * (1.0 / hd)
        out.append(x * lax.rsqrt(ms + NORM_EPS) * gain[:, c * 2 * hd:(c + 1) * 2 * hd])
    return jnp.concatenate(out, axis=1)


def _activate(r, act, gains, hd):
    if act == "silu":
        return r * _sigmoid(r)
    if act == "sigmoid":
        return _sigmoid(r)
    if act in gains:
        return _head_norm(r, gains[act][...], hd)
    return r


ACTS = ("none", "silu", "sigmoid", "qnorm", "knorm")


def _inproj_kernel(x_ref, g_ref, wt_ref, w_ref, qg_ref, kg_ref, outt_ref, out_ref, h_ref, *, acts_t, acts_r, hd):
    j = pl.program_id(1)
    n_t = len(acts_t)
    gains = {"qnorm": qg_ref, "knorm": kg_ref}

    @pl.when(j == 0)
    def _():
        x = x_ref[...]
        ms = jnp.mean(x * x, axis=-1, keepdims=True)
        h_ref[...] = (x * lax.rsqrt(ms + NORM_EPS) * g_ref[...]).astype(BF16)

    def tiles_with(acts, act, base):
        cond = None
        for idx, a in enumerate(acts):
            if a == act:
                c = j == base + idx
                cond = c if cond is None else cond | c
        return cond

    for act in ACTS:
        cond = tiles_with(acts_t, act, 0)
        if cond is not None:
            assert act not in gains
            @pl.when(cond)
            def _(act=act):
                r = lax.dot_general(wt_ref[...], h_ref[...], (((1,), (1,)), ((), ())), preferred_element_type=F32)
                outt_ref[...] = _activate(r, act, gains, hd).astype(BF16)
        cond = tiles_with(acts_r, act, n_t)
        if cond is not None:
            @pl.when(cond)
            def _(act=act):
                n_sub = NORM_SPLIT if act in gains else 1
                sub = h_ref.shape[0] // n_sub
                for p in range(n_sub):
                    rows = slice(p * sub, (p + 1) * sub)
                    r = jnp.dot(h_ref[rows, :], w_ref[...], preferred_element_type=F32)
                    out_ref[rows, :] = _activate(r, act, gains, hd).astype(BF16)


def _inproj(x, g, wt, w, qg, kg, *, acts_t, acts_r, tm, tn, hd):
    t, d = x.shape
    n_t, n_r = len(acts_t), len(acts_r)
    assert wt.shape == (n_t * tn, d) and w.shape == (d, (n_t + n_r) * tn) and set(acts_t) | set(acts_r) <= set(ACTS)
    kern = functools.partial(_inproj_kernel, acts_t=tuple(acts_t), acts_r=tuple(acts_r), hd=hd)
    return pl.pallas_call(
        kern,
        grid=(t // tm, n_t + n_r),
        in_specs=[
            pl.BlockSpec((tm, d), lambda i, j: (i, 0)),
            pl.BlockSpec((1, d), lambda i, j: (0, 0)),
            pl.BlockSpec((tn, d), lambda i, j: (jnp.minimum(j, n_t - 1), 0)),
            pl.BlockSpec((d, tn), lambda i, j: (0, jnp.maximum(j, n_t))),
            pl.BlockSpec((1, tn), lambda i, j: (0, 0)),
            pl.BlockSpec((1, tn), lambda i, j: (0, 0)),
        ],
        out_specs=[
            pl.BlockSpec((tn, tm), lambda i, j: (jnp.minimum(j, n_t - 1), i)),
            pl.BlockSpec((tm, tn), lambda i, j: (i, jnp.maximum(j - n_t, 0))),
        ],
        out_shape=[jax.ShapeDtypeStruct((n_t * tn, t), BF16), jax.ShapeDtypeStruct((t, n_r * tn), BF16)],
        scratch_shapes=[pltpu.VMEM((tm, d), BF16)],
        compiler_params=_params("arbitrary", "arbitrary"),
        name="inproj",
    )(x, g, wt, w, qg, kg)


def _cmul(ar, ai, br, bi):
    return ar * br - ai * bi, ar * bi + ai * br


def _hi_lo(x):
    if x.dtype == BF16:
        return x, None
    hi = x.astype(BF16)
    return hi, (x - hi.astype(F32)).astype(BF16)


def _dot_hi_lo(a, b):
    ah, al = _hi_lo(a)
    bh, bl = _hi_lo(b)
    acc = jnp.dot(ah, bh, preferred_element_type=F32)
    if bl is not None:
        acc = acc + jnp.dot(ah, bl, preferred_element_type=F32)
    if al is not None:
        acc = acc + jnp.dot(al, bh, preferred_element_type=F32)
    return acc


def _discretize(a_re, a_im, log_dt):
    lam_re = jnp.minimum(a_re, -1e-4)
    lam_im = a_im
    dt = jnp.exp(log_dt)
    mag = jnp.exp(lam_re * dt)
    abar_re = mag * jnp.cos(lam_im * dt)
    abar_im = mag * jnp.sin(lam_im * dt)
    den = lam_re * lam_re + lam_im * lam_im
    f_re = ((abar_re - 1.0) * lam_re + abar_im * lam_im) / den
    f_im = (abar_im * lam_re - (abar_re - 1.0) * lam_im) / den
    return abar_re, abar_im, f_re, f_im


def _squarings(ar, ai, n):
    out = [(ar, ai)]
    for _ in range(n - 1):
        ar, ai = _cmul(ar, ai, ar, ai)
        out.append((ar, ai))
    return out


def _power_table(sq, exps, shape):
    pr = pi = None
    for k, (ar, ai) in enumerate(sq):
        bit = ((exps >> k) & 1) == 1
        fr = jnp.broadcast_to(jnp.where(bit, ar, 1.0), shape)
        fi = jnp.broadcast_to(jnp.where(bit, ai, 0.0), shape)
        pr, pi = (fr, fi) if pr is None else _cmul(pr, pi, fr, fi)
    return pr, pi


def _ssm_ops_kernel(arow_ref, bt_ref, c_ref, ct_ref, w_ref, f_ref, e_ref, dec_ref, k_ref, *, h_dim, p_dim):
    lc = CHUNK
    nbits = lc.bit_length()
    lane = lax.broadcasted_iota(jnp.int32, (1, LANES), 1)
    left = lane < lc
    sub = lax.broadcasted_iota(jnp.int32, (lc, 1), 0)
    eye = lax.broadcasted_iota(jnp.int32, (p_dim, 1), 0) == lane

    q2 = []
    rmat = []
    bb_rows = []
    sq_rows = []
    for d in range(2):
        abr, abi, fr, fi = _discretize(arow_ref[d, 0], arow_ref[d, 1], arow_ref[d, 2])
        bbr, bbi = _cmul(fr, fi, bt_ref[d, 0], bt_ref[d, 1])
        bb_rows.append((bbr, bbi))
        sq_row = _squarings(abr, abi, nbits + SCAN_LEVELS)
        sq_rows.append(sq_row)
        cr, ci = c_ref[d, 0], c_ref[d, 1]
        blocks = []
        for h in range(h_dim):
            qr, qi = _cmul(cr, ci, bbr[h:h + 1, :], bbi[h:h + 1, :])
            blocks.append(jnp.where(left, qr, -qi))
        q2.append(jnp.concatenate(blocks, axis=0))
        cabr = jnp.sum(jnp.where(eye, abr, 0.0), axis=-1, keepdims=True)
        cabi = jnp.sum(jnp.where(eye, abi, 0.0), axis=-1, keepdims=True)
        sq_col = _squarings(cabr, cabi, nbits)
        lag = lane - lc
        if d == 0:
            exps, live = jnp.maximum(lag, 0), lag >= 0
        else:
            exps, live = jnp.maximum(-lag, 0), lag <= 0
        vr, vi = _power_table(sq_col, exps, (p_dim, LANES))
        rmat.append(jnp.concatenate([jnp.where(live, vr, 0.0), jnp.where(live, vi, 0.0)], axis=0))
        if d == 0:
            er, ei = _cmul(jnp.where(left, pltpu.roll(vr, lc, 1), vr), jnp.where(left, pltpu.roll(vi, lc, 1), vi),
                           cabr, cabi)
        else:
            er, ei = jnp.where(left, vr, pltpu.roll(vr, lc, 1)), jnp.where(left, vi, pltpu.roll(vi, lc, 1))
        n_rep = (h_dim * lc) // LANES
        er = jnp.concatenate([er] * n_rep, axis=1)
        ei = jnp.concatenate([ei] * n_rep, axis=1)
        rep_r = lax.broadcasted_iota(jnp.int32, (h_dim, h_dim * lc), 0)
        rep_c = lax.broadcasted_iota(jnp.int32, (h_dim, h_dim * lc), 1)
        rep = (_slot_channel(rep_c >> (lc.bit_length() - 1), h_dim) == rep_r).astype(BF16)
        cer = _dot_hi_lo(ct_ref[d, 0], rep)
        cei = _dot_hi_lo(ct_ref[d, 1], rep)
        gr, gi = _cmul(cer, cei, er, ei)
        e_ref[2 * d * p_dim:(2 * d + 1) * p_dim, :] = gr.astype(BF16)
        e_ref[(2 * d + 1) * p_dim:(2 * d + 2) * p_dim, :] = (-gi).astype(BF16)
        for k in range(SCAN_LEVELS):
            dr_, di_ = sq_row[nbits - 1 + k]
            dec_ref[(d * SCAN_LEVELS + k) * 2:(d * SCAN_LEVELS + k) * 2 + 1, :] = dr_
            dec_ref[(d * SCAN_LEVELS + k) * 2 + 1:(d * SCAN_LEVELS + k) * 2 + 2, :] = jnp.where(left, -di_, di_)

    kbi = _dot_hi_lo(q2[0], rmat[0]) + _dot_hi_lo(q2[1], rmat[1])
    k_ref[0] = kbi
    k_ref[1] = pltpu.roll(kbi, lc, 1)

    fexp = (lc - 1 - sub, sub)
    vrow = [_power_table(sq_rows[d][:nbits], fexp[d], (lc, LANES)) for d in range(2)]

    hh = h_dim // 2
    pre_roll_left = ((lane + sub) & (LANES - 1)) < lc

    def per_h(h, carry):
        slot = 2 * (h % hh) + h // hh
        for d in range(2):
            bbr, bbi = bb_rows[d]
            sel = (lax.broadcasted_iota(jnp.int32, (h_dim, 1), 0) == h).astype(F32)
            br = jnp.sum(bbr * sel, axis=0, keepdims=True)
            bi = jnp.sum(bbi * sel, axis=0, keepdims=True)
            zr, zi = _cmul(vrow[d][0], vrow[d][1], br, bi)
            f_ref[slot, :, d * LANES:(d + 1) * LANES] = jnp.where(left, zr, zi).astype(BF16)
        for m in range(hh):
            ra = k_ref[1, pl.ds(h * h_dim + m, 1), :]
            rb = k_ref[0, pl.ds(h * h_dim + m + hh, 1), :]
            src = jnp.where(pre_roll_left, jnp.broadcast_to(ra, (lc, LANES)), jnp.broadcast_to(rb, (lc, LANES)))
            tile = pltpu.roll(src, 0, 1, stride=1, stride_axis=0)
            w_ref[slot, :, m * LANES:(m + 1) * LANES] = tile.astype(BF16)
        return carry

    lax.fori_loop(0, h_dim, per_h, 0, unroll=4)


def _ssm_ops(arow, bt, c2, ct, *, groups, h_dim, p_dim):
    lc = CHUNK
    kern = functools.partial(_ssm_ops_kernel, h_dim=h_dim, p_dim=p_dim)
    blk = lambda shape: pl.BlockSpec((None,) + shape, lambda g: (g,) + (0,) * len(shape))
    return pl.pallas_call(
        kern,
        grid=(groups,),
        in_specs=[blk((2, 3, 1, LANES)), blk((2, 2, h_dim, LANES)),
                  blk((2, 2, h_dim, LANES)), blk((2, 2, p_dim, h_dim))],
        out_specs=[blk((h_dim, lc, h_dim * lc)), blk((h_dim, lc, 2 * LANES)),
                   blk((4 * p_dim, h_dim * lc)), blk((4 * SCAN_LEVELS, LANES))],
        out_shape=[jax.ShapeDtypeStruct((groups, h_dim, lc, h_dim * lc), BF16),
                   jax.ShapeDtypeStruct((groups, h_dim, lc, 2 * LANES), BF16),
                   jax.ShapeDtypeStruct((groups, 4 * p_dim, h_dim * lc), BF16),
                   jax.ShapeDtypeStruct((groups, 4 * SCAN_LEVELS, LANES), F32)],
        scratch_shapes=[pltpu.VMEM((2, h_dim * h_dim, LANES), F32)],
        compiler_params=_params("arbitrary"),
        name="ssm_ops",
    )(arow, bt, c2, ct)


def _decay_mul(x, dec_ref, d, level):
    row = (d * SCAN_LEVELS + level) * 2
    return x * dec_ref[row:row + 1, :] + pltpu.roll(x, CHUNK, 1) * dec_ref[row + 1:row + 2, :]


def _pair_scan(z, dec_ref, d, n_pairs):
    m = z.shape[0]
    cidx = lax.broadcasted_iota(jnp.int32, (m, 1), 0) & (n_pairs - 1)

    def shifted(x, sh):
        if d == 0:
            return jnp.where(cidx >= sh, pltpu.roll(x, sh, 0), 0.0)
        return jnp.where(cidx < n_pairs - sh, pltpu.roll(x, m - sh, 0), 0.0)

    x = z
    for k in range(n_pairs.bit_length() - 1):
        x = x + _decay_mul(shifted(x, 1 << k), dec_ref, d, k + 1)
    return shifted(x, 1)


def _lane_halves(a, b, left):
    ax = a.ndim - 1
    return jnp.where(left, a, pltpu.roll(b, CHUNK, ax)), jnp.where(left, pltpu.roll(a, CHUNK, ax), b)


def _ssm_apply_kernel(*refs, n_seq, n_chunks):
    u_refs = refs[:n_seq]
    w_ref, f_ref, e_ref, dec_ref = refs[n_seq:n_seq + 4]
    y_refs = refs[n_seq + 4:]
    units = [(g, i) for g in range(SSM_GROUPS_PER_STEP) for i in range(n_seq)]
    us = [u_refs[i][g].reshape(2 * u_refs[i].shape[2], u_refs[i].shape[3]) for g, i in units]
    zs = [jnp.dot(u, f_ref[g], preferred_element_type=F32) for u, (g, _) in zip(us, units)]
    accs = [jnp.dot(u, w_ref[g], preferred_element_type=F32) for u, (g, _) in zip(us, units)]
    states = []
    for z, (g, i) in zip(zs, units):
        dec = dec_ref.at[g]
        nc = n_chunks[i]
        half = z.shape[0] // 2
        ze, zo = z[:half], z[half:]
        pf = _pair_scan(_decay_mul(ze[:, :LANES], dec, 0, 0) + zo[:, :LANES], dec, 0, nc // 2)
        sf = jnp.concatenate([pf, _decay_mul(pf, dec, 0, 0) + ze[:, :LANES]], axis=0)
        pb = _pair_scan(ze[:, LANES:] + _decay_mul(zo[:, LANES:], dec, 1, 0), dec, 1, nc // 2)
        sb = jnp.concatenate([_decay_mul(pb, dec, 1, 0) + zo[:, LANES:], pb], axis=0)
        states.append(jnp.concatenate([sf, sb], axis=1).astype(BF16))
    for (g, i), acc, s in zip(units, accs, states):
        acc = acc + jnp.dot(s, e_ref[g], preferred_element_type=F32)
        y_refs[i][g] = acc.astype(BF16).reshape(y_refs[i].shape[1:])


def _slot_channel(slot, h_dim):
    return (slot >> 1) + (h_dim // 2) * (slot & 1)


def _to_chunks_kernel(x_ref, o_ref, *, groups, h_dim):
    tk = o_ref.shape[2]
    hh = h_dim // 2
    left = lax.broadcasted_iota(jnp.int32, (1, 1, LANES), 2) < CHUNK
    x = x_ref[...].astype(F32).reshape(groups, 2, hh, tk * LANES)
    ev, od = [], []
    for k in range(tk):
        e, o = _lane_halves(x[:, 0, :, k * LANES:(k + 1) * LANES], x[:, 1, :, k * LANES:(k + 1) * LANES], left)
        ev.append(e)
        od.append(o)
    for p, parts in enumerate((ev, od)):
        o_ref[:, p] = jnp.stack(parts, axis=1).reshape(groups, tk, hh * LANES).astype(o_ref.dtype)


def _from_chunks_kernel(x_ref, o_ref, *, groups, h_dim):
    tk = x_ref.shape[2]
    hh = h_dim // 2
    left = lax.broadcasted_iota(jnp.int32, (1, 1, LANES), 2) < CHUNK
    ye = x_ref[:, 0].astype(F32).reshape(groups, tk, hh, LANES)
    yo = x_ref[:, 1].astype(F32).reshape(groups, tk, hh, LANES)
    lo, hi = zip(*[_lane_halves(ye[:, k], yo[:, k], left) for k in range(tk)])
    y = jnp.stack([jnp.concatenate(lo, axis=-1), jnp.concatenate(hi, axis=-1)], axis=1)
    o_ref[...] = y.reshape(groups * h_dim, tk * LANES).astype(o_ref.dtype)


def _to_chunks(xt, *, groups, h_dim, tk):
    t = xt.shape[1]
    return pl.pallas_call(
        functools.partial(_to_chunks_kernel, groups=groups, h_dim=h_dim),
        grid=(t // (tk * LANES),),
        in_specs=[pl.BlockSpec((groups * h_dim, tk * LANES), lambda i: (0, i))],
        out_specs=pl.BlockSpec((groups, 2, tk, h_dim * CHUNK), lambda i: (0, 0, i, 0)),
        out_shape=jax.ShapeDtypeStruct((groups, 2, t // LANES, h_dim * CHUNK), BF16),
        compiler_params=_params("arbitrary"),
        name="to_chunks",
    )(xt)


def _from_chunks(y4, *, groups, h_dim, tk):
    t = y4.shape[2] * LANES
    return pl.pallas_call(
        functools.partial(_from_chunks_kernel, groups=groups, h_dim=h_dim),
        grid=(t // (tk * LANES),),
        in_specs=[pl.BlockSpec((groups, 2, tk, h_dim * CHUNK), lambda i: (0, 0, i, 0))],
        out_specs=pl.BlockSpec((groups * h_dim, tk * LANES), lambda i: (0, i)),
        out_shape=jax.ShapeDtypeStruct((groups * h_dim, t), BF16),
        compiler_params=_params("arbitrary"),
        name="from_chunks",
    )(y4)


def _ssm_apply(u4s, w, f, e, dec, *, n_chunks):
    groups = w.shape[0]
    assert groups % SSM_GROUPS_PER_STEP == 0
    blk = lambda shape: pl.BlockSpec((SSM_GROUPS_PER_STEP,) + shape, lambda g: (g,) + (0,) * len(shape))
    kern = functools.partial(_ssm_apply_kernel, n_seq=len(u4s), n_chunks=tuple(n_chunks))
    return pl.pallas_call(
        kern,
        grid=(groups // SSM_GROUPS_PER_STEP,),
        in_specs=[blk(u.shape[1:]) for u in u4s]
        + [blk(w.shape[1:]), blk(f.shape[1:]), blk(e.shape[1:]), blk(dec.shape[1:])],
        out_specs=[blk(u.shape[1:]) for u in u4s],
        out_shape=[jax.ShapeDtypeStruct(u.shape, BF16) for u in u4s],
        compiler_params=_params("arbitrary"),
        name="ssm_apply",
    )(*u4s, w, f, e, dec)


def _attn_bias_kernel(rb_ref, out_ref, pair_ref):
    gw = GRID_W
    cq = lax.broadcasted_iota(jnp.int32, (gw, LANES), 0)
    lane = lax.broadcasted_iota(jnp.int32, (gw, LANES), 1)
    ck = lane & (gw - 1)
    cstart = jnp.clip(cq - WIN_COLS // 2, 0, gw - WIN_COLS)
    valid = (ck >= cstart) & (ck < cstart + WIN_COLS)
    pre_roll_left = ((lane + cq) & (LANES - 1)) < gw
    n_dr = 2 * WIN_ROWS - 1
    for head in range(2):
        rows = rb_ref[head]
        rows_l = pltpu.roll(rows, LANES - (WIN_COLS - 1), 1)
        rows_r = pltpu.roll(rows, gw - (WIN_COLS - 1), 1)
        for dr in range(n_dr - 1):
            src = jnp.where(pre_roll_left, jnp.broadcast_to(rows_l[dr:dr + 1], (gw, LANES)),
                            jnp.broadcast_to(rows_r[dr + 1:dr + 2], (gw, LANES)))
            tile = pltpu.roll(src, 0, 1, stride=1, stride_axis=0)
            pair_ref[head, dr] = jnp.where(valid, tile * LOG2E, MASK_VALUE)
    for dr0 in range(WIN_ROWS):
        for head in range(2):
            for wq in range(WIN_ROWS // 2):
                out_ref[dr0, head * gw:(head + 1) * gw, wq * LANES:(wq + 1) * LANES] = pair_ref[head, dr0 + 2 * wq]


def _attn_bias(rel_bias):
    nh, n_dr, n_dc = rel_bias.shape
    gw = GRID_W
    rel_bias = jnp.pad(rel_bias, ((0, 0), (0, 0), (0, LANES - n_dc)))
    return pl.pallas_call(
        _attn_bias_kernel,
        grid=(nh // 2,),
        in_specs=[pl.BlockSpec((2, n_dr, LANES), lambda h: (h, 0, 0))],
        out_specs=pl.BlockSpec((None, WIN_ROWS, 2 * gw, WIN_ROWS * gw), lambda h: (h, 0, 0, 0)),
        out_shape=jax.ShapeDtypeStruct((nh // 2, WIN_ROWS, 2 * gw, WIN_ROWS * gw), F32),
        scratch_shapes=[pltpu.VMEM((2, 2 * WIN_ROWS - 2, gw, LANES), F32)],
        compiler_params=_params("arbitrary"),
        name="attn_bias",
    )(rel_bias)


def _attn_kernel(q_ref, k_ref, v_ref, za_ref, bias_ref, o_ref, *, rows, hd):
    gw = GRID_W
    win = WIN_ROWS * gw
    lane = lax.broadcasted_iota(jnp.int32, (1, 2 * hd), 1)
    left = lane < hd

    def rows_body(gi, carry):
        slices, scores = [], []
        for i in range(ROW_UNROLL):
            r = gi * ROW_UNROLL + i
            rstart = jnp.clip(r - WIN_ROWS // 2, 0, rows - WIN_ROWS)
            dr0 = rstart - r + (WIN_ROWS - 1)
            qs = pl.ds(pl.multiple_of(r * gw, gw), gw)
            ks = pl.ds(pl.multiple_of(rstart * gw, gw), win)
            slices.append((qs, ks))
            q = q_ref[qs, :]
            zero = jnp.zeros_like(q)
            q2 = jnp.concatenate([jnp.where(left, q, zero), jnp.where(left, zero, q)], axis=0)
            s = lax.dot_general(q2, k_ref[ks, :], (((1,), (1,)), ((), ())), preferred_element_type=F32)
            scores.append(s + bias_ref[dr0])
        for (qs, ks), s in zip(slices, scores):
            m = jnp.max(s, axis=-1, keepdims=True)
            p = jnp.exp2(s - m)
            l = jnp.sum(p, axis=-1, keepdims=True)
            o2 = jnp.dot(p.astype(BF16), v_ref[ks, :], preferred_element_type=F32) / l
            o = jnp.where(left, o2[:gw], o2[gw:]) * za_ref[qs, :].astype(F32)
            o_ref[qs, :] = o.astype(BF16)
        return carry

    lax.fori_loop(0, rows // ROW_UNROLL, rows_body, 0)


def _attention(proj3, bias, *, cols, aw, hd):
    b, l, _ = proj3.shape
    rows = l // GRID_W
    nhp = aw // (2 * hd)
    assert rows % ROW_UNROLL == 0
    col = lambda off: pl.BlockSpec((None, l, 2 * hd), lambda h, i, off=off: (i, 0, off + h))
    kern = functools.partial(_attn_kernel, rows=rows, hd=hd)
    return pl.pallas_call(
        kern,
        grid=(nhp, b),
        in_specs=[col(cols[0]), col(cols[1]), col(cols[2]), col(cols[3]),
                  pl.BlockSpec((None,) + bias.shape[1:], lambda h, i: (h, 0, 0, 0))],
        out_specs=pl.BlockSpec((None, l, 2 * hd), lambda h, i: (i, 0, h)),
        out_shape=jax.ShapeDtypeStruct((b, l, aw), BF16),
        compiler_params=_params("arbitrary", "arbitrary"),
        name="attention",
    )(proj3, proj3, proj3, proj3, bias)


def _gelu_tanh(x):
    c = math.sqrt(2.0 / math.pi)
    return 0.5 * x * (1.0 + jnp.tanh(c * (x + 0.044715 * (x * x * x))))


def _merge_kernel(yt_ref, ut_ref, zt_ref, oa_ref, gs_ref, ga_ref, d_ref, bg_ref, wg_ref, wbs_ref, wba_ref, m_ref):
    tm = m_ref.shape[0]
    sub = tm // MERGE_SPLIT
    parts = [slice(i * sub, (i + 1) * sub) for i in range(MERGE_SPLIT)]
    acts = [_gelu_tanh(yt_ref[:, p].astype(F32) + d_ref[...] * ut_ref[:, p].astype(F32)) for p in parts]
    mas = [jnp.dot(oa_ref[p, :], wba_ref[...], preferred_element_type=F32) for p in parts]
    gates = [jnp.dot(wg_ref[...], a.astype(BF16), preferred_element_type=F32) + bg_ref[...] for a in acts]
    osts = [(a * _sigmoid(g) * zt_ref[:, p].astype(F32)).astype(BF16) for a, g, p in zip(acts, gates, parts)]
    mss = [lax.dot_general(o, wbs_ref[...], (((0,), (0,)), ((), ())), preferred_element_type=F32) for o in osts]
    for p, ms, ma in zip(parts, mss, mas):
        m_ref[p, :] = (gs_ref[p, :].astype(F32) * ms + ga_ref[p, :].astype(F32) * ma).astype(BF16)


def _outproj_kernel(m_ref, x_ref, wo_ref, out_ref):
    out_ref[...] = x_ref[...] + jnp.dot(m_ref[...], wo_ref[...], preferred_element_type=F32)


def _const_spec(shape):
    return pl.BlockSpec(shape, lambda i: (0,) * len(shape), pipeline_mode=pl.Buffered(1))


def _merge(yt, projt, oa, proj, dcol, bgcol, wgt, wbs, wba, *, sw, aw, tm):
    t = oa.shape[0]
    d = wbs.shape[1]
    assert (4 * aw) % d == 0
    n_gs = (4 * aw) // d
    return pl.pallas_call(
        _merge_kernel,
        grid=(t // tm,),
        in_specs=[
            pl.BlockSpec((sw, tm), lambda i: (0, i)),
            pl.BlockSpec((sw, tm), lambda i: (0, i)),
            pl.BlockSpec((sw, tm), lambda i: (1, i)),
            pl.BlockSpec((tm, aw), lambda i: (i, 0)),
            pl.BlockSpec((tm, d), lambda i: (i, n_gs)),
            pl.BlockSpec((tm, d), lambda i: (i, n_gs + 1)),
            _const_spec((sw, 1)), _const_spec((sw, 1)),
            _const_spec(wgt.shape), _const_spec(wbs.shape), _const_spec(wba.shape),
        ],
        out_specs=pl.BlockSpec((tm, d), lambda i: (i, 0)),
        out_shape=jax.ShapeDtypeStruct((t, d), BF16),
        compiler_params=_params("arbitrary"),
        name="merge",
    )(yt, projt, projt, oa, proj, proj, dcol, bgcol, wgt, wbs, wba)


def _outproj(m, x, wo, *, tm):
    t, d = x.shape
    return pl.pallas_call(
        _outproj_kernel,
        grid=(t // tm,),
        in_specs=[pl.BlockSpec((tm, d), lambda i: (i, 0)), pl.BlockSpec((tm, d), lambda i: (i, 0)),
                  _const_spec(wo.shape)],
        out_specs=pl.BlockSpec((tm, d), lambda i: (i, 0)),
        out_shape=jax.ShapeDtypeStruct((t, d), x.dtype),
        compiler_params=_params("arbitrary"),
        name="outproj",
    )(m, x, wo)


def _tile(n, pref):
    t = min(n, pref)
    while n % t:
        t //= 2
    return t


def _layer(xs, norm_g, w_in, a_re, a_im, log_dt, b_re, b_im, c_re, c_im, d_skip, w_glu, b_glu,
           q_g, k_g, rel_bias, w_bs, w_ba, w_out):
    d_model = w_in.shape[0]
    sw = d_skip.shape[0]
    groups, p_dim = a_re.shape[1], a_re.shape[2]
    h_dim = sw // groups
    hd = q_g.shape[0]
    aw = w_ba.shape[0]
    lc = CHUNK
    assert h_dim * lc % LANES == 0 and lc * 2 == LANES and h_dim % 2 == 0

    wr = w_in.astype(BF16)
    wt = wr[:, :2 * sw].T
    tn = _tile(math.gcd(sw, aw), INPROJ_TN)
    acts_t = ["none"] * (sw // tn) + ["silu"] * (sw // tn)
    acts_r = (["qnorm"] * (aw // tn) + ["knorm"] * (aw // tn) + ["none"] * (aw // tn) + ["silu"] * (aw // tn)
              + ["sigmoid"] * (2 * d_model // tn))
    pair = 2 * hd
    attn_cols = tuple(n * aw // pair for n in range(4))
    g2 = norm_g.reshape(1, d_model).astype(F32)
    dup = lambda v: jnp.concatenate([v, v], axis=-1)
    abase = jnp.stack([a_re, a_im, jnp.broadcast_to(log_dt[..., None], a_re.shape)], axis=1)
    abase = abase.transpose(2, 0, 1, 3).astype(F32)
    arow = dup(abase)[:, :, :, None, :]
    bt = dup(jnp.stack([b_re, b_im], axis=1).transpose(2, 0, 1, 4, 3).astype(F32))
    cc = jnp.stack([c_re, c_im], axis=1).transpose(2, 0, 1, 3, 4).astype(F32)
    c2 = dup(cc)
    ct = cc.transpose(0, 1, 2, 4, 3)
    assert tn % hd == 0
    qg_row = jnp.tile(q_g.astype(F32) * (hd ** -0.5 * LOG2E), tn // hd).reshape(1, tn)
    kg_row = jnp.tile(k_g.astype(F32), tn // hd).reshape(1, tn)
    dcol = d_skip.astype(F32).reshape(sw, 1)
    bgcol = b_glu.astype(F32).reshape(sw, 1)
    wgt = w_glu.astype(BF16).T
    wbs = w_bs.astype(BF16)
    wba = w_ba.astype(BF16)
    wo = w_out.astype(BF16)

    w_t, f_t, e_t, dec = _ssm_ops(arow, bt, c2, ct, groups=groups, h_dim=h_dim, p_dim=p_dim)
    bias = _attn_bias(rel_bias.astype(F32))

    projts, projs, u3s, ncs = [], [], [], []
    for x in xs:
        b, l, _ = x.shape
        t = b * l
        assert l % (GRID_W * ROW_UNROLL) == 0 and l % (WIN_ROWS * GRID_W) == 0
        assert (l // lc) & (l // lc - 1) == 0 and l // lc <= 2 ** SCAN_LEVELS
        projt, proj = _inproj(x.reshape(t, d_model), g2, wt, wr, qg_row, kg_row, acts_t=acts_t, acts_r=acts_r,
                              tm=_tile(t, INPROJ_TM), tn=tn, hd=hd)
        projts.append(projt)
        projs.append(proj)
        u3s.append(_to_chunks(projt, groups=groups, h_dim=h_dim, tk=_tile(t // LANES, RELAYOUT_TK)))
        ncs.append(l // lc)

    y4s = _ssm_apply(u3s, w_t.reshape(groups, h_dim * lc, h_dim * lc), f_t.reshape(groups, h_dim * lc, 2 * LANES),
                     e_t, dec, n_chunks=ncs)

    outs = []
    for x, projt, proj, y4 in zip(xs, projts, projs, y4s):
        b, l, _ = x.shape
        t = b * l
        yt = _from_chunks(y4, groups=groups, h_dim=h_dim, tk=_tile(t // LANES, RELAYOUT_TK))
        oa = _attention(proj.reshape(b, l, -1), bias, cols=attn_cols, aw=aw, hd=hd).reshape(t, aw)
        m = _merge(yt, projt, oa, proj, dcol, bgcol, wgt, wbs, wba, sw=sw, aw=aw, tm=_tile(t, MERGE_TM))
        out = _outproj(m, x.reshape(t, d_model), wo, tm=_tile(t, OUTPROJ_TM))
        outs.append(out.reshape(b, l, d_model))
    return outs


def kernel(x_prompt, x_sample, norm_g, w_in, ssm_a_re, ssm_a_im, ssm_log_dt, ssm_b_re, ssm_b_im, ssm_c_re, ssm_c_im, ssm_d, w_glu, b_glu, q_norm_g, k_norm_g, rel_bias, w_branch_ssm, w_branch_attn, w_out):
    xs = [x_prompt, x_sample]
    for layer in range(norm_g.shape[0]):
        xs = _layer(xs, norm_g[layer], w_in[layer], ssm_a_re[layer], ssm_a_im[layer], ssm_log_dt[layer],
                    ssm_b_re[layer], ssm_b_im[layer], ssm_c_re[layer], ssm_c_im[layer], ssm_d[layer],
                    w_glu[layer], b_glu[layer], q_norm_g[layer], k_norm_g[layer], rel_bias[layer],
                    w_branch_ssm[layer], w_branch_attn[layer], w_out[layer])
    return (xs[0], xs[1])
```

```python
import functools
import math

import jax
import jax.numpy as jnp
from jax import lax
from jax.experimental import pallas as pl
from jax.experimental.pallas import tpu as pltpu

NORM_EPS = 1e-6
MASK_VALUE = -1e30
LOG2E = math.log2(math.e)
GRID_W = 64
WIN_ROWS = 8
WIN_COLS = 16
CHUNK = 64
LANES = 128
ROW_UNROLL = 16
NORM_SPLIT = 4
MERGE_SPLIT = 2
SSM_GROUPS_PER_STEP = 4
SCAN_LEVELS = 7
VMEM_LIMIT_BYTES = 56 * 1024 * 1024
INPROJ_TM = 1024
INPROJ_TN = 1024
MERGE_TM = 512
OUTPROJ_TM = 512
RELAYOUT_TK = 8

F32 = jnp.float32
BF16 = jnp.bfloat16


def _params(*sem):
    return pltpu.CompilerParams(dimension_semantics=sem, vmem_limit_bytes=VMEM_LIMIT_BYTES)


def _sigmoid(x):
    return 0.5 * jnp.tanh(0.5 * x) + 0.5


def _head_norm(r, gain, hd):
    lane = lax.broadcasted_iota(jnp.int32, (1, 2 * hd), 1)
    left = lane < hd
    out = []
    for c in range(r.shape[1] // (2 * hd)):
        x = r[:, c * 2 * hd:(c + 1) * 2 * hd]
        x2 = x * x
        sa = jnp.sum(jnp.where(left, x2, 0.0), axis=-1, keepdims=True)
        sb = jnp.sum(jnp.where(left, 0.0, x2), axis=-1, keepdims=True)
        ms = jnp.where(left, sa, sb) * (1.0 / hd)
        out.append(x * lax.rsqrt(ms + NORM_EPS) * gain[:, c * 2 * hd:(c + 1) * 2 * hd])
    return jnp.concatenate(out, axis=1)


def _activate(r, act, gains, hd):
    if act == "silu":
        return r * _sigmoid(r)
    if act == "sigmoid":
        return _sigmoid(r)
    if act in gains:
        return _head_norm(r, gains[act][...], hd)
    return r


ACTS = ("none", "silu", "sigmoid", "qnorm", "knorm")


def _inproj_kernel(x_ref, g_ref, wt_ref, w_ref, qg_ref, kg_ref, outt_ref, out_ref, h_ref, *, acts_t, acts_r, hd):
    j = pl.program_id(1)
    n_t = len(acts_t)
    gains = {"qnorm": qg_ref, "knorm": kg_ref}

    @pl.when(j == 0)
    def _():
        x = x_ref[...]
        ms = jnp.mean(x * x, axis=-1, keepdims=True)
        h_ref[...] = (x * lax.rsqrt(ms + NORM_EPS) * g_ref[...]).astype(BF16)

    def tiles_with(acts, act, base):
        cond = None
        for idx, a in enumerate(acts):
            if a == act:
                c = j == base + idx
                cond = c if cond is None else cond | c
        return cond

    for act in ACTS:
        cond = tiles_with(acts_t, act, 0)
        if cond is not None:
            assert act not in gains
            @pl.when(cond)
            def _(act=act):
                r = lax.dot_general(wt_ref[...], h_ref[...], (((1,), (1,)), ((), ())), preferred_element_type=F32)
                outt_ref[...] = _activate(r, act, gains, hd).astype(BF16)
        cond = tiles_with(acts_r, act, n_t)
        if cond is not None:
            @pl.when(cond)
            def _(act=act):
                n_sub = NORM_SPLIT if act in gains else 1
                sub = h_ref.shape[0] // n_sub
                for p in range(n_sub):
                    rows = slice(p * sub, (p + 1) * sub)
                    r = jnp.dot(h_ref[rows, :], w_ref[...], preferred_element_type=F32)
                    out_ref[rows, :] = _activate(r, act, gains, hd).astype(BF16)


def _inproj(x, g, wt, w, qg, kg, *, acts_t, acts_r, tm, tn, hd):
    t, d = x.shape
    n_t, n_r = len(acts_t), len(acts_r)
    assert wt.shape == (n_t * tn, d) and w.shape == (d, (n_t + n_r) * tn) and set(acts_t) | set(acts_r) <= set(ACTS)
    kern = functools.partial(_inproj_kernel, acts_t=tuple(acts_t), acts_r=tuple(acts_r), hd=hd)
    return pl.pallas_call(
        kern,
        grid=(t // tm, n_t + n_r),
        in_specs=[
            pl.BlockSpec((tm, d), lambda i, j: (i, 0)),
            pl.BlockSpec((1, d), lambda i, j: (0, 0)),
            pl.BlockSpec((tn, d), lambda i, j: (jnp.minimum(j, n_t - 1), 0)),
            pl.BlockSpec((d, tn), lambda i, j: (0, jnp.maximum(j, n_t))),
            pl.BlockSpec((1, tn), lambda i, j: (0, 0)),
            pl.BlockSpec((1, tn), lambda i, j: (0, 0)),
        ],
        out_specs=[
            pl.BlockSpec((tn, tm), lambda i, j: (jnp.minimum(j, n_t - 1), i)),
            pl.BlockSpec((tm, tn), lambda i, j: (i, jnp.maximum(j - n_t, 0))),
        ],
        out_shape=[jax.ShapeDtypeStruct((n_t * tn, t), BF16), jax.ShapeDtypeStruct((t, n_r * tn), BF16)],
        scratch_shapes=[pltpu.VMEM((tm, d), BF16)],
        compiler_params=_params("arbitrary", "arbitrary"),
        name="inproj",
    )(x, g, wt, w, qg, kg)


def _cmul(ar, ai, br, bi):
    return ar * br - ai * bi, ar * bi + ai * br


def _hi_lo(x):
    if x.dtype == BF16:
        return x, None
    hi = x.astype(BF16)
    return hi, (x - hi.astype(F32)).astype(BF16)


def _dot_hi_lo(a, b):
    ah, al = _hi_lo(a)
    bh, bl = _hi_lo(b)
    acc = jnp.dot(ah, bh, preferred_element_type=F32)
    if bl is not None:
        acc = acc + jnp.dot(ah, bl, preferred_element_type=F32)
    if al is not None:
        acc = acc + jnp.dot(al, bh, preferred_element_type=F32)
    return acc


def _discretize(a_re, a_im, log_dt):
    lam_re = jnp.minimum(a_re, -1e-4)
    lam_im = a_im
    dt = jnp.exp(log_dt)
    mag = jnp.exp(lam_re * dt)
    abar_re = mag * jnp.cos(lam_im * dt)
    abar_im = mag * jnp.sin(lam_im * dt)
    den = lam_re * lam_re + lam_im * lam_im
    f_re = ((abar_re - 1.0) * lam_re + abar_im * lam_im) / den
    f_im = (abar_im * lam_re - (abar_re - 1.0) * lam_im) / den
    return abar_re, abar_im, f_re, f_im


def _squarings(ar, ai, n):
    out = [(ar, ai)]
    for _ in range(n - 1):
        ar, ai = _cmul(ar, ai, ar, ai)
        out.append((ar, ai))
    return out


def _power_table(sq, exps, shape):
    pr = pi = None
    for k, (ar, ai) in enumerate(sq):
        bit = ((exps >> k) & 1) == 1
        fr = jnp.broadcast_to(jnp.where(bit, ar, 1.0), shape)
        fi = jnp.broadcast_to(jnp.where(bit, ai, 0.0), shape)
        pr, pi = (fr, fi) if pr is None else _cmul(pr, pi, fr, fi)
    return pr, pi


def _ssm_ops_kernel(arow_ref, bt_ref, c_ref, ct_ref, w_ref, f_ref, e_ref, dec_ref, k_ref, *, h_dim, p_dim):
    lc = CHUNK
    nbits = lc.bit_length()
    lane = lax.broadcasted_iota(jnp.int32, (1, LANES), 1)
    left = lane < lc
    sub = lax.broadcasted_iota(jnp.int32, (lc, 1), 0)
    eye = lax.broadcasted_iota(jnp.int32, (p_dim, 1), 0) == lane

    q2 = []
    rmat = []
    bb_rows = []
    sq_rows = []
    for d in range(2):
        abr, abi, fr, fi = _discretize(arow_ref[d, 0], arow_ref[d, 1], arow_ref[d, 2])
        bbr, bbi = _cmul(fr, fi, bt_ref[d, 0], bt_ref[d, 1])
        bb_rows.append((bbr, bbi))
        sq_row = _squarings(abr, abi, nbits + SCAN_LEVELS)
        sq_rows.append(sq_row)
        cr, ci = c_ref[d, 0], c_ref[d, 1]
        blocks = []
        for h in range(h_dim):
            qr, qi = _cmul(cr, ci, bbr[h:h + 1, :], bbi[h:h + 1, :])
            blocks.append(jnp.where(left, qr, -qi))
        q2.append(jnp.concatenate(blocks, axis=0))
        cabr = jnp.sum(jnp.where(eye, abr, 0.0), axis=-1, keepdims=True)
        cabi = jnp.sum(jnp.where(eye, abi, 0.0), axis=-1, keepdims=True)
        sq_col = _squarings(cabr, cabi, nbits)
        lag = lane - lc
        if d == 0:
            exps, live = jnp.maximum(lag, 0), lag >= 0
        else:
            exps, live = jnp.maximum(-lag, 0), lag <= 0
        vr, vi = _power_table(sq_col, exps, (p_dim, LANES))
        rmat.append(jnp.concatenate([jnp.where(live, vr, 0.0), jnp.where(live, vi, 0.0)], axis=0))
        if d == 0:
            er, ei = _cmul(jnp.where(left, pltpu.roll(vr, lc, 1), vr), jnp.where(left, pltpu.roll(vi, lc, 1), vi),
                           cabr, cabi)
        else:
            er, ei = jnp.where(left, vr, pltpu.roll(vr, lc, 1)), jnp.where(left, vi, pltpu.roll(vi, lc, 1))
        n_rep = (h_dim * lc) // LANES
        er = jnp.concatenate([er] * n_rep, axis=1)
        ei = jnp.concatenate([ei] * n_rep, axis=1)
        rep_r = lax.broadcasted_iota(jnp.int32, (h_dim, h_dim * lc), 0)
        rep_c = lax.broadcasted_iota(jnp.int32, (h_dim, h_dim * lc), 1)
        rep = (_slot_channel(rep_c >> (lc.bit_length() - 1), h_dim) == rep_r).astype(BF16)
        cer = _dot_hi_lo(ct_ref[d, 0], rep)
        cei = _dot_hi_lo(ct_ref[d, 1], rep)
        gr, gi = _cmul(cer, cei, er, ei)
        e_ref[2 * d * p_dim:(2 * d + 1) * p_dim, :] = gr.astype(BF16)
        e_ref[(2 * d + 1) * p_dim:(2 * d + 2) * p_dim, :] = (-gi).astype(BF16)
        for k in range(SCAN_LEVELS):
            dr_, di_ = sq_row[nbits - 1 + k]
            dec_ref[(d * SCAN_LEVELS + k) * 2:(d * SCAN_LEVELS + k) * 2 + 1, :] = dr_
            dec_ref[(d * SCAN_LEVELS + k) * 2 + 1:(d * SCAN_LEVELS + k) * 2 + 2, :] = jnp.where(left, -di_, di_)

    kbi = _dot_hi_lo(q2[0], rmat[0]) + _dot_hi_lo(q2[1], rmat[1])
    k_ref[0] = kbi
    k_ref[1] = pltpu.roll(kbi, lc, 1)

    fexp = (lc - 1 - sub, sub)
    vrow = [_power_table(sq_rows[d][:nbits], fexp[d], (lc, LANES)) for d in range(2)]

    hh = h_dim // 2
    pre_roll_left = ((lane + sub) & (LANES - 1)) < lc

    def per_h(h, carry):
        slot = 2 * (h % hh) + h // hh
        for d in range(2):
            bbr, bbi = bb_rows[d]
            sel = (lax.broadcasted_iota(jnp.int32, (h_dim, 1), 0) == h).astype(F32)
            br = jnp.sum(bbr * sel, axis=0, keepdims=True)
            bi = jnp.sum(bbi * sel, axis=0, keepdims=True)
            zr, zi = _cmul(vrow[d][0], vrow[d][1], br, bi)
            f_ref[slot, :, d * LANES:(d + 1) * LANES] = jnp.where(left, zr, zi).astype(BF16)
        for m in range(hh):
            ra = k_ref[1, pl.ds(h * h_dim + m, 1), :]
            rb = k_ref[0, pl.ds(h * h_dim + m + hh, 1), :]
            src = jnp.where(pre_roll_left, jnp.broadcast_to(ra, (lc, LANES)), jnp.broadcast_to(rb, (lc, LANES)))
            tile = pltpu.roll(src, 0, 1, stride=1, stride_axis=0)
            w_ref[slot, :, m * LANES:(m + 1) * LANES] = tile.astype(BF16)
        return carry

    lax.fori_loop(0, h_dim, per_h, 0, unroll=4)


def _ssm_ops(arow, bt, c2, ct, *, groups, h_dim, p_dim):
    lc = CHUNK
    kern = functools.partial(_ssm_ops_kernel, h_dim=h_dim, p_dim=p_dim)
    blk = lambda shape: pl.BlockSpec((None,) + shape, lambda g: (g,) + (0,) * len(shape))
    return pl.pallas_call(
        kern,
        grid=(groups,),
        in_specs=[blk((2, 3, 1, LANES)), blk((2, 2, h_dim, LANES)),
                  blk((2, 2, h_dim, LANES)), blk((2, 2, p_dim, h_dim))],
        out_specs=[blk((h_dim, lc, h_dim * lc)), blk((h_dim, lc, 2 * LANES)),
                   blk((4 * p_dim, h_dim * lc)), blk((4 * SCAN_LEVELS, LANES))],
        out_shape=[jax.ShapeDtypeStruct((groups, h_dim, lc, h_dim * lc), BF16),
                   jax.ShapeDtypeStruct((groups, h_dim, lc, 2 * LANES), BF16),
                   jax.ShapeDtypeStruct((groups, 4 * p_dim, h_dim * lc), BF16),
                   jax.ShapeDtypeStruct((groups, 4 * SCAN_LEVELS, LANES), F32)],
        scratch_shapes=[pltpu.VMEM((2, h_dim * h_dim, LANES), F32)],
        compiler_params=_params("arbitrary"),
        name="ssm_ops",
    )(arow, bt, c2, ct)


def _decay_mul(x, dec_ref, d, level):
    row = (d * SCAN_LEVELS + level) * 2
    return x * dec_ref[row:row + 1, :] + pltpu.roll(x, CHUNK, 1) * dec_ref[row + 1:row + 2, :]


def _pair_scan(z, dec_ref, d, n_pairs):
    m = z.shape[0]
    cidx = lax.broadcasted_iota(jnp.int32, (m, 1), 0) & (n_pairs - 1)

    def shifted(x, sh):
        if d == 0:
            return jnp.where(cidx >= sh, pltpu.roll(x, sh, 0), 0.0)
        return jnp.where(cidx < n_pairs - sh, pltpu.roll(x, m - sh, 0), 0.0)

    x = z
    for k in range(n_pairs.bit_length() - 1):
        x = x + _decay_mul(shifted(x, 1 << k), dec_ref, d, k + 1)
    return shifted(x, 1)


def _lane_halves(a, b, left):
    ax = a.ndim - 1
    return jnp.where(left, a, pltpu.roll(b, CHUNK, ax)), jnp.where(left, pltpu.roll(a, CHUNK, ax), b)


def _ssm_apply_kernel(*refs, n_seq, n_chunks):
    u_refs = refs[:n_seq]
    w_ref, f_ref, e_ref, dec_ref = refs[n_seq:n_seq + 4]
    y_refs = refs[n_seq + 4:]
    units = [(g, i) for g in range(SSM_GROUPS_PER_STEP) for i in range(n_seq)]
    us = [u_refs[i][g].reshape(2 * u_refs[i].shape[2], u_refs[i].shape[3]) for g, i in units]
    zs = [jnp.dot(u, f_ref[g], preferred_element_type=F32) for u, (g, _) in zip(us, units)]
    accs = [jnp.dot(u, w_ref[g], preferred_element_type=F32) for u, (g, _) in zip(us, units)]
    states = []
    for z, (g, i) in zip(zs, units):
        dec = dec_ref.at[g]
        nc = n_chunks[i]
        half = z.shape[0] // 2
        ze, zo = z[:half], z[half:]
        pf = _pair_scan(_decay_mul(ze[:, :LANES], dec, 0, 0) + zo[:, :LANES], dec, 0, nc // 2)
        sf = jnp.concatenate([pf, _decay_mul(pf, dec, 0, 0) + ze[:, :LANES]], axis=0)
        pb = _pair_scan(ze[:, LANES:] + _decay_mul(zo[:, LANES:], dec, 1, 0), dec, 1, nc // 2)
        sb = jnp.concatenate([_decay_mul(pb, dec, 1, 0) + zo[:, LANES:], pb], axis=0)
        states.append(jnp.concatenate([sf, sb], axis=1).astype(BF16))
    for (g, i), acc, s in zip(units, accs, states):
        acc = acc + jnp.dot(s, e_ref[g], preferred_element_type=F32)
        y_refs[i][g] = acc.astype(BF16).reshape(y_refs[i].shape[1:])


def _slot_channel(slot, h_dim):
    return (slot >> 1) + (h_dim // 2) * (slot & 1)


def _to_chunks_kernel(x_ref, o_ref, *, groups, h_dim):
    tk = o_ref.shape[2]
    hh = h_dim // 2
    left = lax.broadcasted_iota(jnp.int32, (1, 1, LANES), 2) < CHUNK
    x = x_ref[...].astype(F32).reshape(groups, 2, hh, tk * LANES)
    ev, od = [], []
    for k in range(tk):
        e, o = _lane_halves(x[:, 0, :, k * LANES:(k + 1) * LANES], x[:, 1, :, k * LANES:(k + 1) * LANES], left)
        ev.append(e)
        od.append(o)
    for p, parts in enumerate((ev, od)):
        o_ref[:, p] = jnp.stack(parts, axis=1).reshape(groups, tk, hh * LANES).astype(o_ref.dtype)


def _from_chunks_kernel(x_ref, o_ref, *, groups, h_dim):
    tk = x_ref.shape[2]
    hh = h_dim // 2
    left = lax.broadcasted_iota(jnp.int32, (1, 1, LANES), 2) < CHUNK
    ye = x_ref[:, 0].astype(F32).reshape(groups, tk, hh, LANES)
    yo = x_ref[:, 1].astype(F32).reshape(groups, tk, hh, LANES)
    lo, hi = zip(*[_lane_halves(ye[:, k], yo[:, k], left) for k in range(tk)])
    y = jnp.stack([jnp.concatenate(lo, axis=-1), jnp.concatenate(hi, axis=-1)], axis=1)
    o_ref[...] = y.reshape(groups * h_dim, tk * LANES).astype(o_ref.dtype)


def _to_chunks(xt, *, groups, h_dim, tk):
    t = xt.shape[1]
    return pl.pallas_call(
        functools.partial(_to_chunks_kernel, groups=groups, h_dim=h_dim),
        grid=(t // (tk * LANES),),
        in_specs=[pl.BlockSpec((groups * h_dim, tk * LANES), lambda i: (0, i))],
        out_specs=pl.BlockSpec((groups, 2, tk, h_dim * CHUNK), lambda i: (0, 0, i, 0)),
        out_shape=jax.ShapeDtypeStruct((groups, 2, t // LANES, h_dim * CHUNK), BF16),
        compiler_params=_params("arbitrary"),
        name="to_chunks",
    )(xt)


def _from_chunks(y4, *, groups, h_dim, tk):
    t = y4.shape[2] * LANES
    return pl.pallas_call(
        functools.partial(_from_chunks_kernel, groups=groups, h_dim=h_dim),
        grid=(t // (tk * LANES),),
        in_specs=[pl.BlockSpec((groups, 2, tk, h_dim * CHUNK), lambda i: (0, 0, i, 0))],
        out_specs=pl.BlockSpec((groups * h_dim, tk * LANES), lambda i: (0, i)),
        out_shape=jax.ShapeDtypeStruct((groups * h_dim, t), BF16),
        compiler_params=_params("arbitrary"),
        name="from_chunks",
    )(y4)


def _ssm_apply(u4s, w, f, e, dec, *, n_chunks):
    groups = w.shape[0]
    assert groups % SSM_GROUPS_PER_STEP == 0
    blk = lambda shape: pl.BlockSpec((SSM_GROUPS_PER_STEP,) + shape, lambda g: (g,) + (0,) * len(shape))
    kern = functools.partial(_ssm_apply_kernel, n_seq=len(u4s), n_chunks=tuple(n_chunks))
    return pl.pallas_call(
        kern,
        grid=(groups // SSM_GROUPS_PER_STEP,),
        in_specs=[blk(u.shape[1:]) for u in u4s]
        + [blk(w.shape[1:]), blk(f.shape[1:]), blk(e.shape[1:]), blk(dec.shape[1:])],
        out_specs=[blk(u.shape[1:]) for u in u4s],
        out_shape=[jax.ShapeDtypeStruct(u.shape, BF16) for u in u4s],
        compiler_params=_params("arbitrary"),
        name="ssm_apply",
    )(*u4s, w, f, e, dec)


def _attn_bias_kernel(rb_ref, out_ref, pair_ref):
    gw = GRID_W
    cq = lax.broadcasted_iota(jnp.int32, (gw, LANES), 0)
    lane = lax.broadcasted_iota(jnp.int32, (gw, LANES), 1)
    ck = lane & (gw - 1)
    cstart = jnp.clip(cq - WIN_COLS // 2, 0, gw - WIN_COLS)
    valid = (ck >= cstart) & (ck < cstart + WIN_COLS)
    pre_roll_left = ((lane + cq) & (LANES - 1)) < gw
    n_dr = 2 * WIN_ROWS - 1
    for head in range(2):
        rows = rb_ref[head]
        rows_l = pltpu.roll(rows, LANES - (WIN_COLS - 1), 1)
        rows_r = pltpu.roll(rows, gw - (WIN_COLS - 1), 1)
        for dr in range(n_dr - 1):
            src = jnp.where(pre_roll_left, jnp.broadcast_to(rows_l[dr:dr + 1], (gw, LANES)),
                            jnp.broadcast_to(rows_r[dr + 1:dr + 2], (gw, LANES)))
            tile = pltpu.roll(src, 0, 1, stride=1, stride_axis=0)
            pair_ref[head, dr] = jnp.where(valid, tile * LOG2E, MASK_VALUE)
    for dr0 in range(WIN_ROWS):
        for head in range(2):
            for wq in range(WIN_ROWS // 2):
                out_ref[dr0, head * gw:(head + 1) * gw, wq * LANES:(wq + 1) * LANES] = pair_ref[head, dr0 + 2 * wq]


def _attn_bias(rel_bias):
    nh, n_dr, n_dc = rel_bias.shape
    gw = GRID_W
    rel_bias = jnp.pad(rel_bias, ((0, 0), (0, 0), (0, LANES - n_dc)))
    return pl.pallas_call(
        _attn_bias_kernel,
        grid=(nh // 2,),
        in_specs=[pl.BlockSpec((2, n_dr, LANES), lambda h: (h, 0, 0))],
        out_specs=pl.BlockSpec((None, WIN_ROWS, 2 * gw, WIN_ROWS * gw), lambda h: (h, 0, 0, 0)),
        out_shape=jax.ShapeDtypeStruct((nh // 2, WIN_ROWS, 2 * gw, WIN_ROWS * gw), F32),
        scratch_shapes=[pltpu.VMEM((2, 2 * WIN_ROWS - 2, gw, LANES), F32)],
        compiler_params=_params("arbitrary"),
        name="attn_bias",
    )(rel_bias)


def _attn_kernel(q_ref, k_ref, v_ref, za_ref, bias_ref, o_ref, *, rows, hd):
    gw = GRID_W
    win = WIN_ROWS * gw
    lane = lax.broadcasted_iota(jnp.int32, (1, 2 * hd), 1)
    left = lane < hd

    def rows_body(gi, carry):
        slices, scores = [], []
        for i in range(ROW_UNROLL):
            r = gi * ROW_UNROLL + i
            rstart = jnp.clip(r - WIN_ROWS // 2, 0, rows - WIN_ROWS)
            dr0 = rstart - r + (WIN_ROWS - 1)
            qs = pl.ds(pl.multiple_of(r * gw, gw), gw)
            ks = pl.ds(pl.multiple_of(rstart * gw, gw), win)
            slices.append((qs, ks))
            q = q_ref[qs, :]
            zero = jnp.zeros_like(q)
            q2 = jnp.concatenate([jnp.where(left, q, zero), jnp.where(left, zero, q)], axis=0)
            s = lax.dot_general(q2, k_ref[ks, :], (((1,), (1,)), ((), ())), preferred_element_type=F32)
            scores.append(s + bias_ref[dr0])
        for (qs, ks), s in zip(slices, scores):
            m = jnp.max(s, axis=-1, keepdims=True)
            p = jnp.exp2(s - m)
            l = jnp.sum(p, axis=-1, keepdims=True)
            o2 = jnp.dot(p.astype(BF16), v_ref[ks, :], preferred_element_type=F32) / l
            o = jnp.where(left, o2[:gw], o2[gw:]) * za_ref[qs, :].astype(F32)
            o_ref[qs, :] = o.astype(BF16)
        return carry

    lax.fori_loop(0, rows // ROW_UNROLL, rows_body, 0)


def _attention(proj3, bias, *, cols, aw, hd):
    b, l, _ = proj3.shape
    rows = l // GRID_W
    nhp = aw // (2 * hd)
    assert rows % ROW_UNROLL == 0
    col = lambda off: pl.BlockSpec((None, l, 2 * hd), lambda h, i, off=off: (i, 0, off + h))
    kern = functools.partial(_attn_kernel, rows=rows, hd=hd)
    return pl.pallas_call(
        kern,
        grid=(nhp, b),
        in_specs=[col(cols[0]), col(cols[1]), col(cols[2]), col(cols[3]),
                  pl.BlockSpec((None,) + bias.shape[1:], lambda h, i: (h, 0, 0, 0))],
        out_specs=pl.BlockSpec((None, l, 2 * hd), lambda h, i: (i, 0, h)),
        out_shape=jax.ShapeDtypeStruct((b, l, aw), BF16),
        compiler_params=_params("arbitrary", "arbitrary"),
        name="attention",
    )(proj3, proj3, proj3, proj3, bias)


def _gelu_tanh(x):
    c = math.sqrt(2.0 / math.pi)
    return 0.5 * x * (1.0 + jnp.tanh(c * (x + 0.044715 * (x * x * x))))


def _merge_kernel(yt_ref, ut_ref, zt_ref, oa_ref, gs_ref, ga_ref, d_ref, bg_ref, wg_ref, wbs_ref, wba_ref, m_ref):
    tm = m_ref.shape[0]
    sub = tm // MERGE_SPLIT
    parts = [slice(i * sub, (i + 1) * sub) for i in range(MERGE_SPLIT)]
    acts = [_gelu_tanh(yt_ref[:, p].astype(F32) + d_ref[...] * ut_ref[:, p].astype(F32)) for p in parts]
    mas = [jnp.dot(oa_ref[p, :], wba_ref[...], preferred_element_type=F32) for p in parts]
    gates = [jnp.dot(wg_ref[...], a.astype(BF16), preferred_element_type=F32) + bg_ref[...] for a in acts]
    osts = [(a * _sigmoid(g) * zt_ref[:, p].astype(F32)).astype(BF16) for a, g, p in zip(acts, gates, parts)]
    mss = [lax.dot_general(o, wbs_ref[...], (((0,), (0,)), ((), ())), preferred_element_type=F32) for o in osts]
    for p, ms, ma in zip(parts, mss, mas):
        m_ref[p, :] = (gs_ref[p, :].astype(F32) * ms + ga_ref[p, :].astype(F32) * ma).astype(BF16)


def _outproj_kernel(m_ref, x_ref, wo_ref, out_ref):
    out_ref[...] = x_ref[...] + jnp.dot(m_ref[...], wo_ref[...], preferred_element_type=F32)


def _const_spec(shape):
    return pl.BlockSpec(shape, lambda i: (0,) * len(shape), pipeline_mode=pl.Buffered(1))


def _merge(yt, projt, oa, proj, dcol, bgcol, wgt, wbs, wba, *, sw, aw, tm):
    t = oa.shape[0]
    d = wbs.shape[1]
    assert (4 * aw) % d == 0
    n_gs = (4 * aw) // d
    return pl.pallas_call(
        _merge_kernel,
        grid=(t // tm,),
        in_specs=[
            pl.BlockSpec((sw, tm), lambda i: (0, i)),
            pl.BlockSpec((sw, tm), lambda i: (0, i)),
            pl.BlockSpec((sw, tm), lambda i: (1, i)),
            pl.BlockSpec((tm, aw), lambda i: (i, 0)),
            pl.BlockSpec((tm, d), lambda i: (i, n_gs)),
            pl.BlockSpec((tm, d), lambda i: (i, n_gs + 1)),
            _const_spec((sw, 1)), _const_spec((sw, 1)),
            _const_spec(wgt.shape), _const_spec(wbs.shape), _const_spec(wba.shape),
        ],
        out_specs=pl.BlockSpec((tm, d), lambda i: (i, 0)),
        out_shape=jax.ShapeDtypeStruct((t, d), BF16),
        compiler_params=_params("arbitrary"),
        name="merge",
    )(yt, projt, projt, oa, proj, proj, dcol, bgcol, wgt, wbs, wba)


def _outproj(m, x, wo, *, tm):
    t, d = x.shape
    return pl.pallas_call(
        _outproj_kernel,
        grid=(t // tm,),
        in_specs=[pl.BlockSpec((tm, d), lambda i: (i, 0)), pl.BlockSpec((tm, d), lambda i: (i, 0)),
                  _const_spec(wo.shape)],
        out_specs=pl.BlockSpec((tm, d), lambda i: (i, 0)),
        out_shape=jax.ShapeDtypeStruct((t, d), x.dtype),
        compiler_params=_params("arbitrary"),
        name="outproj",
    )(m, x, wo)


def _tile(n, pref):
    t = min(n, pref)
    while n % t:
        t //= 2
    return t


def _layer(xs, norm_g, w_in, a_re, a_im, log_dt, b_re, b_im, c_re, c_im, d_skip, w_glu, b_glu,
           q_g, k_g, rel_bias, w_bs, w_ba, w_out):
    d_model = w_in.shape[0]
    sw = d_skip.shape[0]
    groups, p_dim = a_re.shape[1], a_re.shape[2]
    h_dim = sw // groups
    hd = q_g.shape[0]
    aw = w_ba.shape[0]
    lc = CHUNK
    assert h_dim * lc % LANES == 0 and lc * 2 == LANES and h_dim % 2 == 0

    wr = w_in.astype(BF16)
    wt = wr[:, :2 * sw].T
    tn = _tile(math.gcd(sw, aw), INPROJ_TN)
    acts_t = ["none"] * (sw // tn) + ["silu"] * (sw // tn)
    acts_r = (["qnorm"] * (aw // tn) + ["knorm"] * (aw // tn) + ["none"] * (aw // tn) + ["silu"] * (aw // tn)
              + ["sigmoid"] * (2 * d_model // tn))
    pair = 2 * hd
    attn_cols = tuple(n * aw // pair for n in range(4))
    g2 = norm_g.reshape(1, d_model).astype(F32)
    dup = lambda v: jnp.concatenate([v, v], axis=-1)
    abase = jnp.stack([a_re, a_im, jnp.broadcast_to(log_dt[..., None], a_re.shape)], axis=1)
    abase = abase.transpose(2, 0, 1, 3).astype(F32)
    arow = dup(abase)[:, :, :, None, :]
    bt = dup(jnp.stack([b_re, b_im], axis=1).transpose(2, 0, 1, 4, 3).astype(F32))
    cc = jnp.stack([c_re, c_im], axis=1).transpose(2, 0, 1, 3, 4).astype(F32)
    c2 = dup(cc)
    ct = cc.transpose(0, 1, 2, 4, 3)
    assert tn % hd == 0
    qg_row = jnp.tile(q_g.astype(F32) * (hd ** -0.5 * LOG2E), tn // hd).reshape(1, tn)
    kg_row = jnp.tile(k_g.astype(F32), tn // hd).reshape(1, tn)
    dcol = d_skip.astype(F32).reshape(sw, 1)
    bgcol = b_glu.astype(F32).reshape(sw, 1)
    wgt = w_glu.astype(BF16).T
    wbs = w_bs.astype(BF16)
    wba = w_ba.astype(BF16)
    wo = w_out.astype(BF16)

    w_t, f_t, e_t, dec = _ssm_ops(arow, bt, c2, ct, groups=groups, h_dim=h_dim, p_dim=p_dim)
    bias = _attn_bias(rel_bias.astype(F32))

    projts, projs, u3s, ncs = [], [], [], []
    for x in xs:
        b, l, _ = x.shape
        t = b * l
        assert l % (GRID_W * ROW_UNROLL) == 0 and l % (WIN_ROWS * GRID_W) == 0
        assert (l // lc) & (l // lc - 1) == 0 and l // lc <= 2 ** SCAN_LEVELS
        projt, proj = _inproj(x.reshape(t, d_model), g2, wt, wr, qg_row, kg_row, acts_t=acts_t, acts_r=acts_r,
                              tm=_tile(t, INPROJ_TM), tn=tn, hd=hd)
        projts.append(projt)
        projs.append(proj)
        u3s.append(_to_chunks(projt, groups=groups, h_dim=h_dim, tk=_tile(t // LANES, RELAYOUT_TK)))
        ncs.append(l // lc)

    y4s = _ssm_apply(u3s, w_t.reshape(groups, h_dim * lc, h_dim * lc), f_t.reshape(groups, h_dim * lc, 2 * LANES),
                     e_t, dec, n_chunks=ncs)

    outs = []
    for x, projt, proj, y4 in zip(xs, projts, projs, y4s):
        b, l, _ = x.shape
        t = b * l
        yt = _from_chunks(y4, groups=groups, h_dim=h_dim, tk=_tile(t // LANES, RELAYOUT_TK))
        oa = _attention(proj.reshape(b, l, -1), bias, cols=attn_cols, aw=aw, hd=hd).reshape(t, aw)
        m = _merge(yt, projt, oa, proj, dcol, bgcol, wgt, wbs, wba, sw=sw, aw=aw, tm=_tile(t, MERGE_TM))
        out = _outproj(m, x.reshape(t, d_model), wo, tm=_tile(t, OUTPROJ_TM))
        outs.append(out.reshape(b, l, d_model))
    return outs


def kernel(x_prompt, x_sample, norm_g, w_in, ssm_a_re, ssm_a_im, ssm_log_dt, ssm_b_re, ssm_b_im, ssm_c_re, ssm_c_im, ssm_d, w_glu, b_glu, q_norm_g, k_norm_g, rel_bias, w_branch_ssm, w_branch_attn, w_out):
    xs = [x_prompt, x_sample]
    for layer in range(norm_g.shape[0]):
        xs = _layer(xs, norm_g[layer], w_in[layer], ssm_a_re[layer], ssm_a_im[layer], ssm_log_dt[layer],
                    ssm_b_re[layer], ssm_b_im[layer], ssm_c_re[layer], ssm_c_im[layer], ssm_d[layer],
                    w_glu[layer], b_glu[layer], q_norm_g[layer], k_norm_g[layer], rel_bias[layer],
                    w_branch_ssm[layer], w_branch_attn[layer], w_out[layer])
    return (xs[0], xs[1])
```

```python
import functools
import math

import jax
import jax.numpy as jnp
from jax import lax
from jax.experimental import pallas as pl
from jax.experimental.pallas import tpu as pltpu

NORM_EPS = 1e-6
MASK_VALUE = -1e30
LOG2E = math.log2(math.e)
GRID_W = 64
WIN_ROWS = 8
WIN_COLS = 16
CHUNK = 64
LANES = 128
ROW_UNROLL = 16
NORM_SPLIT = 4
MERGE_SPLIT = 2
SSM_GROUPS_PER_STEP = 4
SCAN_LEVELS = 7
VMEM_LIMIT_BYTES = 56 * 1024 * 1024
INPROJ_TM = 1024
INPROJ_TN = 1024
MERGE_TM = 512
OUTPROJ_TM = 512
RELAYOUT_TK = 8

F32 = jnp.float32
BF16 = jnp.bfloat16


def _params(*sem):
    return pltpu.CompilerParams(dimension_semantics=sem, vmem_limit_bytes=VMEM_LIMIT_BYTES)


def _sigmoid(x):
    return 0.5 * jnp.tanh(0.5 * x) + 0.5


def _head_norm(r, gain, hd):
    lane = lax.broadcasted_iota(jnp.int32, (1, 2 * hd), 1)
    left = lane < hd
    out = []
    for c in range(r.shape[1] // (2 * hd)):
        x = r[:, c * 2 * hd:(c + 1) * 2 * hd]
        x2 = x * x
        sa = jnp.sum(jnp.where(left, x2, 0.0), axis=-1, keepdims=True)
        sb = jnp.sum(jnp.where(left, 0.0, x2), axis=-1, keepdims=True)
        ms = jnp.where(left, sa, sb) * (1.0 / hd)
        out.append(x * lax.rsqrt(ms + NORM_EPS) * gain[:, c * 2 * hd:(c + 1) * 2 * hd])
    return jnp.concatenate(out, axis=1)


def _activate(r, act, gains, hd):
    if act == "silu":
        return r * _sigmoid(r)
    if act == "sigmoid":
        return _sigmoid(r)
    if act in gains:
        return _head_norm(r, gains[act][...], hd)
    return r


ACTS = ("none", "silu", "sigmoid", "qnorm", "knorm")


def _inproj_kernel(x_ref, g_ref, wt_ref, w_ref, qg_ref, kg_ref, outt_ref, out_ref, h_ref, *, acts_t, acts_r, hd):
    j = pl.program_id(1)
    n_t = len(acts_t)
    gains = {"qnorm": qg_ref, "knorm": kg_ref}

    @pl.when(j == 0)
    def _():
        x = x_ref[...]
        ms = jnp.mean(x * x, axis=-1, keepdims=True)
        h_ref[...] = (x * lax.rsqrt(ms + NORM_EPS) * g_ref[...]).astype(BF16)

    def tiles_with(acts, act, base):
        cond = None
        for idx, a in enumerate(acts):
            if a == act:
                c = j == base + idx
                cond = c if cond is None else cond | c
        return cond

    for act in ACTS:
        cond = tiles_with(acts_t, act, 0)
        if cond is not None:
            assert act not in gains
            @pl.when(cond)
            def _(act=act):
                r = lax.dot_general(wt_ref[...], h_ref[...], (((1,), (1,)), ((), ())), preferred_element_type=F32)
                outt_ref[...] = _activate(r, act, gains, hd).astype(BF16)
        cond = tiles_with(acts_r, act, n_t)
        if cond is not None:
            @pl.when(cond)
            def _(act=act):
                n_sub = NORM_SPLIT if act in gains else 1
                sub = h_ref.shape[0] // n_sub
                for p in range(n_sub):
                    rows = slice(p * sub, (p + 1) * sub)
                    r = jnp.dot(h_ref[rows, :], w_ref[...], preferred_element_type=F32)
                    out_ref[rows, :] = _activate(r, act, gains, hd).astype(BF16)


def _inproj(x, g, wt, w, qg, kg, *, acts_t, acts_r, tm, tn, hd):
    t, d = x.shape
    n_t, n_r = len(acts_t), len(acts_r)
    assert wt.shape == (n_t * tn, d) and w.shape == (d, (n_t + n_r) * tn) and set(acts_t) | set(acts_r) <= set(ACTS)
    kern = functools.partial(_inproj_kernel, acts_t=tuple(acts_t), acts_r=tuple(acts_r), hd=hd)
    return pl.pallas_call(
        kern,
        grid=(t // tm, n_t + n_r),
        in_specs=[
            pl.BlockSpec((tm, d), lambda i, j: (i, 0)),
            pl.BlockSpec((1, d), lambda i, j: (0, 0)),
            pl.BlockSpec((tn, d), lambda i, j: (jnp.minimum(j, n_t - 1), 0)),
            pl.BlockSpec((d, tn), lambda i, j: (0, jnp.maximum(j, n_t))),
            pl.BlockSpec((1, tn), lambda i, j: (0, 0)),
            pl.BlockSpec((1, tn), lambda i, j: (0, 0)),
        ],
        out_specs=[
            pl.BlockSpec((tn, tm), lambda i, j: (jnp.minimum(j, n_t - 1), i)),
            pl.BlockSpec((tm, tn), lambda i, j: (i, jnp.maximum(j - n_t, 0))),
        ],
        out_shape=[jax.ShapeDtypeStruct((n_t * tn, t), BF16), jax.ShapeDtypeStruct((t, n_r * tn), BF16)],
        scratch_shapes=[pltpu.VMEM((tm, d), BF16)],
        compiler_params=_params("arbitrary", "arbitrary"),
        name="inproj",
    )(x, g, wt, w, qg, kg)


def _cmul(ar, ai, br, bi):
    return ar * br - ai * bi, ar * bi + ai * br


def _hi_lo(x):
    if x.dtype == BF16:
        return x, None
    hi = x.astype(BF16)
    return hi, (x - hi.astype(F32)).astype(BF16)


def _dot_hi_lo(a, b):
    ah, al = _hi_lo(a)
    bh, bl = _hi_lo(b)
    acc = jnp.dot(ah, bh, preferred_element_type=F32)
    if bl is not None:
        acc = acc + jnp.dot(ah, bl, preferred_element_type=F32)
    if al is not None:
        acc = acc + jnp.dot(al, bh, preferred_element_type=F32)
    return acc


def _discretize(a_re, a_im, log_dt):
    lam_re = jnp.minimum(a_re, -1e-4)
    lam_im = a_im
    dt = jnp.exp(log_dt)
    mag = jnp.exp(lam_re * dt)
    abar_re = mag * jnp.cos(lam_im * dt)
    abar_im = mag * jnp.sin(lam_im * dt)
    den = lam_re * lam_re + lam_im * lam_im
    f_re = ((abar_re - 1.0) * lam_re + abar_im * lam_im) / den
    f_im = (abar_im * lam_re - (abar_re - 1.0) * lam_im) / den
    return abar_re, abar_im, f_re, f_im


def _squarings(ar, ai, n):
    out = [(ar, ai)]
    for _ in range(n - 1):
        ar, ai = _cmul(ar, ai, ar, ai)
        out.append((ar, ai))
    return out


def _power_table(sq, exps, shape):
    pr = pi = None
    for k, (ar, ai) in enumerate(sq):
        bit = ((exps >> k) & 1) == 1
        fr = jnp.broadcast_to(jnp.where(bit, ar, 1.0), shape)
        fi = jnp.broadcast_to(jnp.where(bit, ai, 0.0), shape)
        pr, pi = (fr, fi) if pr is None else _cmul(pr, pi, fr, fi)
    return pr, pi


def _ssm_ops_kernel(arow_ref, bt_ref, c_ref, ct_ref, w_ref, f_ref, e_ref, dec_ref, k_ref, *, h_dim, p_dim):
    lc = CHUNK
    nbits = lc.bit_length()
    lane = lax.broadcasted_iota(jnp.int32, (1, LANES), 1)
    left = lane < lc
    sub = lax.broadcasted_iota(jnp.int32, (lc, 1), 0)
    eye = lax.broadcasted_iota(jnp.int32, (p_dim, 1), 0) == lane

    q2 = []
    rmat = []
    bb_rows = []
    sq_rows = []
    for d in range(2):
        abr, abi, fr, fi = _discretize(arow_ref[d, 0], arow_ref[d, 1], arow_ref[d, 2])
        bbr, bbi = _cmul(fr, fi, bt_ref[d, 0], bt_ref[d, 1])
        bb_rows.append((bbr, bbi))
        sq_row = _squarings(abr, abi, nbits + SCAN_LEVELS)
        sq_rows.append(sq_row)
        cr, ci = c_ref[d, 0], c_ref[d, 1]
        blocks = []
        for h in range(h_dim):
            qr, qi = _cmul(cr, ci, bbr[h:h + 1, :], bbi[h:h + 1, :])
            blocks.append(jnp.where(left, qr, -qi))
        q2.append(jnp.concatenate(blocks, axis=0))
        cabr = jnp.sum(jnp.where(eye, abr, 0.0), axis=-1, keepdims=True)
        cabi = jnp.sum(jnp.where(eye, abi, 0.0), axis=-1, keepdims=True)
        sq_col = _squarings(cabr, cabi, nbits)
        lag = lane - lc
        if d == 0:
            exps, live = jnp.maximum(lag, 0), lag >= 0
        else:
            exps, live = jnp.maximum(-lag, 0), lag <= 0
        vr, vi = _power_table(sq_col, exps, (p_dim, LANES))
        rmat.append(jnp.concatenate([jnp.where(live, vr, 0.0), jnp.where(live, vi, 0.0)], axis=0))
        if d == 0:
            er, ei = _cmul(jnp.where(left, pltpu.roll(vr, lc, 1), vr), jnp.where(left, pltpu.roll(vi, lc, 1), vi),
                           cabr, cabi)
        else:
            er, ei = jnp.where(left, vr, pltpu.roll(vr, lc, 1)), jnp.where(left, vi, pltpu.roll(vi, lc, 1))
        n_rep = (h_dim * lc) // LANES
        er = jnp.concatenate([er] * n_rep, axis=1)
        ei = jnp.concatenate([ei] * n_rep, axis=1)
        rep_r = lax.broadcasted_iota(jnp.int32, (h_dim, h_dim * lc), 0)
        rep_c = lax.broadcasted_iota(jnp.int32, (h_dim, h_dim * lc), 1)
        rep = (_slot_channel(rep_c >> (lc.bit_length() - 1), h_dim) == rep_r).astype(BF16)
        cer = _dot_hi_lo(ct_ref[d, 0], rep)
        cei = _dot_hi_lo(ct_ref[d, 1], rep)
        gr, gi = _cmul(cer, cei, er, ei)
        e_ref[2 * d * p_dim:(2 * d + 1) * p_dim, :] = gr.astype(BF16)
        e_ref[(2 * d + 1) * p_dim:(2 * d + 2) * p_dim, :] = (-gi).astype(BF16)
        for k in range(SCAN_LEVELS):
            dr_, di_ = sq_row[nbits - 1 + k]
            dec_ref[(d * SCAN_LEVELS + k) * 2:(d * SCAN_LEVELS + k) * 2 + 1, :] = dr_
            dec_ref[(d * SCAN_LEVELS + k) * 2 + 1:(d * SCAN_LEVELS + k) * 2 + 2, :] = jnp.where(left, -di_, di_)

    kbi = _dot_hi_lo(q2[0], rmat[0]) + _dot_hi_lo(q2[1], rmat[1])
    k_ref[0] = kbi
    k_ref[1] = pltpu.roll(kbi, lc, 1)

    fexp = (lc - 1 - sub, sub)
    vrow = [_power_table(sq_rows[d][:nbits], fexp[d], (lc, LANES)) for d in range(2)]

    hh = h_dim // 2
    pre_roll_left = ((lane + sub) & (LANES - 1)) < lc

    def per_h(h, carry):
        slot = 2 * (h % hh) + h // hh
        for d in range(2):
            bbr, bbi = bb_rows[d]
            sel = (lax.broadcasted_iota(jnp.int32, (h_dim, 1), 0) == h).astype(F32)
            br = jnp.sum(bbr * sel, axis=0, keepdims=True)
            bi = jnp.sum(bbi * sel, axis=0, keepdims=True)
            zr, zi = _cmul(vrow[d][0], vrow[d][1], br, bi)
            f_ref[slot, :, d * LANES:(d + 1) * LANES] = jnp.where(left, zr, zi).astype(BF16)
        for m in range(hh):
            ra = k_ref[1, pl.ds(h * h_dim + m, 1), :]
            rb = k_ref[0, pl.ds(h * h_dim + m + hh, 1), :]
            src = jnp.where(pre_roll_left, jnp.broadcast_to(ra, (lc, LANES)), jnp.broadcast_to(rb, (lc, LANES)))
            tile = pltpu.roll(src, 0, 1, stride=1, stride_axis=0)
            w_ref[slot, :, m * LANES:(m + 1) * LANES] = tile.astype(BF16)
        return carry

    lax.fori_loop(0, h_dim, per_h, 0, unroll=4)


def _ssm_ops(arow, bt, c2, ct, *, groups, h_dim, p_dim):
    lc = CHUNK
    kern = functools.partial(_ssm_ops_kernel, h_dim=h_dim, p_dim=p_dim)
    blk = lambda shape: pl.BlockSpec((None,) + shape, lambda g: (g,) + (0,) * len(shape))
    return pl.pallas_call(
        kern,
        grid=(groups,),
        in_specs=[blk((2, 3, 1, LANES)), blk((2, 2, h_dim, LANES)),
                  blk((2, 2, h_dim, LANES)), blk((2, 2, p_dim, h_dim))],
        out_specs=[blk((h_dim, lc, h_dim * lc)), blk((h_dim, lc, 2 * LANES)),
                   blk((4 * p_dim, h_dim * lc)), blk((4 * SCAN_LEVELS, LANES))],
        out_shape=[jax.ShapeDtypeStruct((groups, h_dim, lc, h_dim * lc), BF16),
                   jax.ShapeDtypeStruct((groups, h_dim, lc, 2 * LANES), BF16),
                   jax.ShapeDtypeStruct((groups, 4 * p_dim, h_dim * lc), BF16),
                   jax.ShapeDtypeStruct((groups, 4 * SCAN_LEVELS, LANES), F32)],
        scratch_shapes=[pltpu.VMEM((2, h_dim * h_dim, LANES), F32)],
        compiler_params=_params("arbitrary"),
        name="ssm_ops",
    )(arow, bt, c2, ct)


def _decay_mul(x, dec_ref, d, level):
    row = (d * SCAN_LEVELS + level) * 2
    return x * dec_ref[row:row + 1, :] + pltpu.roll(x, CHUNK, 1) * dec_ref[row + 1:row + 2, :]


def _pair_scan(z, dec_ref, d, n_pairs):
    m = z.shape[0]
    cidx = lax.broadcasted_iota(jnp.int32, (m, 1), 0) & (n_pairs - 1)

    def shifted(x, sh):
        if d == 0:
            return jnp.where(cidx >= sh, pltpu.roll(x, sh, 0), 0.0)
        return jnp.where(cidx < n_pairs - sh, pltpu.roll(x, m - sh, 0), 0.0)

    x = z
    for k in range(n_pairs.bit_length() - 1):
        x = x + _decay_mul(shifted(x, 1 << k), dec_ref, d, k + 1)
    return shifted(x, 1)


def _lane_halves(a, b, left):
    ax = a.ndim - 1
    return jnp.where(left, a, pltpu.roll(b, CHUNK, ax)), jnp.where(left, pltpu.roll(a, CHUNK, ax), b)


def _chunk_rows(x, groups, h_dim):
    tk = x.shape[1] // LANES
    hh = h_dim // 2
    left = lax.broadcasted_iota(jnp.int32, (1, 1, LANES), 2) < CHUNK
    x = x.astype(F32).reshape(groups, 2, hh, tk * LANES)
    ev, od = [], []
    for k in range(tk):
        e, o = _lane_halves(x[:, 0, :, k * LANES:(k + 1) * LANES], x[:, 1, :, k * LANES:(k + 1) * LANES], left)
        ev.append(e)
        od.append(o)
    return [jnp.stack(parts, axis=1).reshape(groups, tk, hh * LANES).astype(BF16) for parts in (ev, od)]


def _ssm_apply_kernel(*refs, n_seq, n_chunks, h_dim):
    u_refs = refs[:n_seq]
    w_ref, f_ref, e_ref, dec_ref = refs[n_seq:n_seq + 4]
    y_refs = refs[n_seq + 4:]
    units = [(g, i) for g in range(SSM_GROUPS_PER_STEP) for i in range(n_seq)]
    rows = [_chunk_rows(u_ref[...], SSM_GROUPS_PER_STEP, h_dim) for u_ref in u_refs]
    us = [jnp.concatenate([rows[i][0][g], rows[i][1][g]], axis=0) for g, i in units]
    zs = [jnp.dot(u, f_ref[g], preferred_element_type=F32) for u, (g, _) in zip(us, units)]
    accs = [jnp.dot(u, w_ref[g], preferred_element_type=F32) for u, (g, _) in zip(us, units)]
    states = []
    for z, (g, i) in zip(zs, units):
        dec = dec_ref.at[g]
        nc = n_chunks[i]
        half = z.shape[0] // 2
        ze, zo = z[:half], z[half:]
        pf = _pair_scan(_decay_mul(ze[:, :LANES], dec, 0, 0) + zo[:, :LANES], dec, 0, nc // 2)
        sf = jnp.concatenate([pf, _decay_mul(pf, dec, 0, 0) + ze[:, :LANES]], axis=0)
        pb = _pair_scan(ze[:, LANES:] + _decay_mul(zo[:, LANES:], dec, 1, 0), dec, 1, nc // 2)
        sb = jnp.concatenate([_decay_mul(pb, dec, 1, 0) + zo[:, LANES:], pb], axis=0)
        states.append(jnp.concatenate([sf, sb], axis=1).astype(BF16))
    for (g, i), acc, s in zip(units, accs, states):
        acc = acc + jnp.dot(s, e_ref[g], preferred_element_type=F32)
        y_refs[i][g] = acc.astype(BF16).reshape(y_refs[i].shape[1:])


def _slot_channel(slot, h_dim):
    return (slot >> 1) + (h_dim // 2) * (slot & 1)


def _from_chunks_kernel(x_ref, o_ref, *, groups, h_dim):
    tk = x_ref.shape[2]
    hh = h_dim // 2
    left = lax.broadcasted_iota(jnp.int32, (1, 1, LANES), 2) < CHUNK
    ye = x_ref[:, 0].astype(F32).reshape(groups, tk, hh, LANES)
    yo = x_ref[:, 1].astype(F32).reshape(groups, tk, hh, LANES)
    lo, hi = zip(*[_lane_halves(ye[:, k], yo[:, k], left) for k in range(tk)])
    y = jnp.stack([jnp.concatenate(lo, axis=-1), jnp.concatenate(hi, axis=-1)], axis=1)
    o_ref[...] = y.reshape(groups * h_dim, tk * LANES).astype(o_ref.dtype)


def _from_chunks(y4, *, groups, h_dim, tk):
    t = y4.shape[2] * LANES
    return pl.pallas_call(
        functools.partial(_from_chunks_kernel, groups=groups, h_dim=h_dim),
        grid=(t // (tk * LANES),),
        in_specs=[pl.BlockSpec((groups, 2, tk, h_dim * CHUNK), lambda i: (0, 0, i, 0))],
        out_specs=pl.BlockSpec((groups * h_dim, tk * LANES), lambda i: (0, i)),
        out_shape=jax.ShapeDtypeStruct((groups * h_dim, t), BF16),
        compiler_params=_params("arbitrary"),
        name="from_chunks",
    )(y4)


def _ssm_apply(projts, w, f, e, dec, *, n_chunks, h_dim):
    groups = w.shape[0]
    assert groups % SSM_GROUPS_PER_STEP == 0
    blk = lambda shape: pl.BlockSpec((SSM_GROUPS_PER_STEP,) + shape, lambda g: (g,) + (0,) * len(shape))
    kern = functools.partial(_ssm_apply_kernel, n_seq=len(projts), n_chunks=tuple(n_chunks), h_dim=h_dim)
    y_shapes = [(groups, 2, p.shape[1] // LANES, h_dim * CHUNK) for p in projts]
    return pl.pallas_call(
        kern,
        grid=(groups // SSM_GROUPS_PER_STEP,),
        in_specs=[pl.BlockSpec((SSM_GROUPS_PER_STEP * h_dim, p.shape[1]), lambda g: (g, 0)) for p in projts]
        + [blk(w.shape[1:]), blk(f.shape[1:]), blk(e.shape[1:]), blk(dec.shape[1:])],
        out_specs=[blk(s[1:]) for s in y_shapes],
        out_shape=[jax.ShapeDtypeStruct(s, BF16) for s in y_shapes],
        compiler_params=_params("arbitrary"),
        name="ssm_apply",
    )(*projts, w, f, e, dec)


def _attn_bias_kernel(rb_ref, out_ref, pair_ref):
    gw = GRID_W
    cq = lax.broadcasted_iota(jnp.int32, (gw, LANES), 0)
    lane = lax.broadcasted_iota(jnp.int32, (gw, LANES), 1)
    ck = lane & (gw - 1)
    cstart = jnp.clip(cq - WIN_COLS // 2, 0, gw - WIN_COLS)
    valid = (ck >= cstart) & (ck < cstart + WIN_COLS)
    pre_roll_left = ((lane + cq) & (LANES - 1)) < gw
    n_dr = 2 * WIN_ROWS - 1
    for head in range(2):
        rows = rb_ref[head]
        rows_l = pltpu.roll(rows, LANES - (WIN_COLS - 1), 1)
        rows_r = pltpu.roll(rows, gw - (WIN_COLS - 1), 1)
        for dr in range(n_dr - 1):
            src = jnp.where(pre_roll_left, jnp.broadcast_to(rows_l[dr:dr + 1], (gw, LANES)),
                            jnp.broadcast_to(rows_r[dr + 1:dr + 2], (gw, LANES)))
            tile = pltpu.roll(src, 0, 1, stride=1, stride_axis=0)
            pair_ref[head, dr] = jnp.where(valid, tile * LOG2E, MASK_VALUE)
    for dr0 in range(WIN_ROWS):
        for head in range(2):
            for wq in range(WIN_ROWS // 2):
                out_ref[dr0, head * gw:(head + 1) * gw, wq * LANES:(wq + 1) * LANES] = pair_ref[head, dr0 + 2 * wq]


def _attn_bias(rel_bias):
    nh, n_dr, n_dc = rel_bias.shape
    gw = GRID_W
    rel_bias = jnp.pad(rel_bias, ((0, 0), (0, 0), (0, LANES - n_dc)))
    return pl.pallas_call(
        _attn_bias_kernel,
        grid=(nh // 2,),
        in_specs=[pl.BlockSpec((2, n_dr, LANES), lambda h: (h, 0, 0))],
        out_specs=pl.BlockSpec((None, WIN_ROWS, 2 * gw, WIN_ROWS * gw), lambda h: (h, 0, 0, 0)),
        out_shape=jax.ShapeDtypeStruct((nh // 2, WIN_ROWS, 2 * gw, WIN_ROWS * gw), F32),
        scratch_shapes=[pltpu.VMEM((2, 2 * WIN_ROWS - 2, gw, LANES), F32)],
        compiler_params=_params("arbitrary"),
        name="attn_bias",
    )(rel_bias)


def _attn_kernel(q_ref, k_ref, v_ref, za_ref, bias_ref, o_ref, *, rows, hd):
    gw = GRID_W
    win = WIN_ROWS * gw
    lane = lax.broadcasted_iota(jnp.int32, (1, 2 * hd), 1)
    left = lane < hd

    def rows_body(gi, carry):
        slices, scores = [], []
        for i in range(ROW_UNROLL):
            r = gi * ROW_UNROLL + i
            rstart = jnp.clip(r - WIN_ROWS // 2, 0, rows - WIN_ROWS)
            dr0 = rstart - r + (WIN_ROWS - 1)
            qs = pl.ds(pl.multiple_of(r * gw, gw), gw)
            ks = pl.ds(pl.multiple_of(rstart * gw, gw), win)
            slices.append((qs, ks))
            q = q_ref[qs, :]
            zero = jnp.zeros_like(q)
            q2 = jnp.concatenate([jnp.where(left, q, zero), jnp.where(left, zero, q)], axis=0)
            s = lax.dot_general(q2, k_ref[ks, :], (((1,), (1,)), ((), ())), preferred_element_type=F32)
            scores.append(s + bias_ref[dr0])
        for (qs, ks), s in zip(slices, scores):
            m = jnp.max(s, axis=-1, keepdims=True)
            p = jnp.exp2(s - m)
            l = jnp.sum(p, axis=-1, keepdims=True)
            o2 = jnp.dot(p.astype(BF16), v_ref[ks, :], preferred_element_type=F32) / l
            o = jnp.where(left, o2[:gw], o2[gw:]) * za_ref[qs, :].astype(F32)
            o_ref[qs, :] = o.astype(BF16)
        return carry

    lax.fori_loop(0, rows // ROW_UNROLL, rows_body, 0)


def _attention(proj3, bias, *, cols, aw, hd):
    b, l, _ = proj3.shape
    rows = l // GRID_W
    nhp = aw // (2 * hd)
    assert rows % ROW_UNROLL == 0
    col = lambda off: pl.BlockSpec((None, l, 2 * hd), lambda h, i, off=off: (i, 0, off + h))
    kern = functools.partial(_attn_kernel, rows=rows, hd=hd)
    return pl.pallas_call(
        kern,
        grid=(nhp, b),
        in_specs=[col(cols[0]), col(cols[1]), col(cols[2]), col(cols[3]),
                  pl.BlockSpec((None,) + bias.shape[1:], lambda h, i: (h, 0, 0, 0))],
        out_specs=pl.BlockSpec((None, l, 2 * hd), lambda h, i: (i, 0, h)),
        out_shape=jax.ShapeDtypeStruct((b, l, aw), BF16),
        compiler_params=_params("arbitrary", "arbitrary"),
        name="attention",
    )(proj3, proj3, proj3, proj3, bias)


def _gelu_tanh(x):
    c = math.sqrt(2.0 / math.pi)
    return 0.5 * x * (1.0 + jnp.tanh(c * (x + 0.044715 * (x * x * x))))


def _merge_kernel(yt_ref, ut_ref, zt_ref, oa_ref, gs_ref, ga_ref, d_ref, bg_ref, wg_ref, wbs_ref, wba_ref, m_ref):
    tm = m_ref.shape[0]
    sub = tm // MERGE_SPLIT
    parts = [slice(i * sub, (i + 1) * sub) for i in range(MERGE_SPLIT)]
    acts = [_gelu_tanh(yt_ref[:, p].astype(F32) + d_ref[...] * ut_ref[:, p].astype(F32)) for p in parts]
    mas = [jnp.dot(oa_ref[p, :], wba_ref[...], preferred_element_type=F32) for p in parts]
    gates = [jnp.dot(wg_ref[...], a.astype(BF16), preferred_element_type=F32) + bg_ref[...] for a in acts]
    osts = [(a * _sigmoid(g) * zt_ref[:, p].astype(F32)).astype(BF16) for a, g, p in zip(acts, gates, parts)]
    mss = [lax.dot_general(o, wbs_ref[...], (((0,), (0,)), ((), ())), preferred_element_type=F32) for o in osts]
    for p, ms, ma in zip(parts, mss, mas):
        m_ref[p, :] = (gs_ref[p, :].astype(F32) * ms + ga_ref[p, :].astype(F32) * ma).astype(BF16)


def _outproj_kernel(m_ref, x_ref, wo_ref, out_ref):
    out_ref[...] = x_ref[...] + jnp.dot(m_ref[...], wo_ref[...], preferred_element_type=F32)


def _const_spec(shape):
    return pl.BlockSpec(shape, lambda i: (0,) * len(shape), pipeline_mode=pl.Buffered(1))


def _merge(yt, projt, oa, proj, dcol, bgcol, wgt, wbs, wba, *, sw, aw, tm):
    t = oa.shape[0]
    d = wbs.shape[1]
    assert (4 * aw) % d == 0
    n_gs = (4 * aw) // d
    return pl.pallas_call(
        _merge_kernel,
        grid=(t // tm,),
        in_specs=[
            pl.BlockSpec((sw, tm), lambda i: (0, i)),
            pl.BlockSpec((sw, tm), lambda i: (0, i)),
            pl.BlockSpec((sw, tm), lambda i: (1, i)),
            pl.BlockSpec((tm, aw), lambda i: (i, 0)),
            pl.BlockSpec((tm, d), lambda i: (i, n_gs)),
            pl.BlockSpec((tm, d), lambda i: (i, n_gs + 1)),
            _const_spec((sw, 1)), _const_spec((sw, 1)),
            _const_spec(wgt.shape), _const_spec(wbs.shape), _const_spec(wba.shape),
        ],
        out_specs=pl.BlockSpec((tm, d), lambda i: (i, 0)),
        out_shape=jax.ShapeDtypeStruct((t, d), BF16),
        compiler_params=_params("arbitrary"),
        name="merge",
    )(yt, projt, projt, oa, proj, proj, dcol, bgcol, wgt, wbs, wba)


def _outproj(m, x, wo, *, tm):
    t, d = x.shape
    return pl.pallas_call(
        _outproj_kernel,
        grid=(t // tm,),
        in_specs=[pl.BlockSpec((tm, d), lambda i: (i, 0)), pl.BlockSpec((tm, d), lambda i: (i, 0)),
                  _const_spec(wo.shape)],
        out_specs=pl.BlockSpec((tm, d), lambda i: (i, 0)),
        out_shape=jax.ShapeDtypeStruct((t, d), x.dtype),
        compiler_params=_params("arbitrary"),
        name="outproj",
    )(m, x, wo)


def _tile(n, pref):
    t = min(n, pref)
    while n % t:
        t //= 2
    return t


def _layer(xs, norm_g, w_in, a_re, a_im, log_dt, b_re, b_im, c_re, c_im, d_skip, w_glu, b_glu,
           q_g, k_g, rel_bias, w_bs, w_ba, w_out):
    d_model = w_in.shape[0]
    sw = d_skip.shape[0]
    groups, p_dim = a_re.shape[1], a_re.shape[2]
    h_dim = sw // groups
    hd = q_g.shape[0]
    aw = w_ba.shape[0]
    lc = CHUNK
    assert h_dim * lc % LANES == 0 and lc * 2 == LANES and h_dim % 2 == 0

    wr = w_in.astype(BF16)
    wt = wr[:, :2 * sw].T
    tn = _tile(math.gcd(sw, aw), INPROJ_TN)
    acts_t = ["none"] * (sw // tn) + ["silu"] * (sw // tn)
    acts_r = (["qnorm"] * (aw // tn) + ["knorm"] * (aw // tn) + ["none"] * (aw // tn) + ["silu"] * (aw // tn)
              + ["sigmoid"] * (2 * d_model // tn))
    pair = 2 * hd
    attn_cols = tuple(n * aw // pair for n in range(4))
    g2 = norm_g.reshape(1, d_model).astype(F32)
    dup = lambda v: jnp.concatenate([v, v], axis=-1)
    abase = jnp.stack([a_re, a_im, jnp.broadcast_to(log_dt[..., None], a_re.shape)], axis=1)
    abase = abase.transpose(2, 0, 1, 3).astype(F32)
    arow = dup(abase)[:, :, :, None, :]
    bt = dup(jnp.stack([b_re, b_im], axis=1).transpose(2, 0, 1, 4, 3).astype(F32))
    cc = jnp.stack([c_re, c_im], axis=1).transpose(2, 0, 1, 3, 4).astype(F32)
    c2 = dup(cc)
    ct = cc.transpose(0, 1, 2, 4, 3)
    assert tn % hd == 0
    qg_row = jnp.tile(q_g.astype(F32) * (hd ** -0.5 * LOG2E), tn // hd).reshape(1, tn)
    kg_row = jnp.tile(k_g.astype(F32), tn // hd).reshape(1, tn)
    dcol = d_skip.astype(F32).reshape(sw, 1)
    bgcol = b_glu.astype(F32).reshape(sw, 1)
    wgt = w_glu.astype(BF16).T
    wbs = w_bs.astype(BF16)
    wba = w_ba.astype(BF16)
    wo = w_out.astype(BF16)

    w_t, f_t, e_t, dec = _ssm_ops(arow, bt, c2, ct, groups=groups, h_dim=h_dim, p_dim=p_dim)
    bias = _attn_bias(rel_bias.astype(F32))

    projts, projs, ncs = [], [], []
    for x in xs:
        b, l, _ = x.shape
        t = b * l
        assert l % (GRID_W * ROW_UNROLL) == 0 and l % (WIN_ROWS * GRID_W) == 0
        assert (l // lc) & (l // lc - 1) == 0 and l // lc <= 2 ** SCAN_LEVELS
        projt, proj = _inproj(x.reshape(t, d_model), g2, wt, wr, qg_row, kg_row, acts_t=acts_t, acts_r=acts_r,
                              tm=_tile(t, INPROJ_TM), tn=tn, hd=hd)
        projts.append(projt)
        projs.append(proj)
        ncs.append(l // lc)

    y4s = _ssm_apply(projts, w_t.reshape(groups, h_dim * lc, h_dim * lc), f_t.reshape(groups, h_dim * lc, 2 * LANES),
                     e_t, dec, n_chunks=ncs, h_dim=h_dim)

    outs = []
    for x, projt, proj, y4 in zip(xs, projts, projs, y4s):
        b, l, _ = x.shape
        t = b * l
        yt = _from_chunks(y4, groups=groups, h_dim=h_dim, tk=_tile(t // LANES, RELAYOUT_TK))
        oa = _attention(proj.reshape(b, l, -1), bias, cols=attn_cols, aw=aw, hd=hd).reshape(t, aw)
        m = _merge(yt, projt, oa, proj, dcol, bgcol, wgt, wbs, wba, sw=sw, aw=aw, tm=_tile(t, MERGE_TM))
        out = _outproj(m, x.reshape(t, d_model), wo, tm=_tile(t, OUTPROJ_TM))
        outs.append(out.reshape(b, l, d_model))
    return outs


def kernel(x_prompt, x_sample, norm_g, w_in, ssm_a_re, ssm_a_im, ssm_log_dt, ssm_b_re, ssm_b_im, ssm_c_re, ssm_c_im, ssm_d, w_glu, b_glu, q_norm_g, k_norm_g, rel_bias, w_branch_ssm, w_branch_attn, w_out):
    xs = [x_prompt, x_sample]
    for layer in range(norm_g.shape[0]):
        xs = _layer(xs, norm_g[layer], w_in[layer], ssm_a_re[layer], ssm_a_im[layer], ssm_log_dt[layer],
                    ssm_b_re[layer], ssm_b_im[layer], ssm_c_re[layer], ssm_c_im[layer], ssm_d[layer],
                    w_glu[layer], b_glu[layer], q_norm_g[layer], k_norm_g[layer], rel_bias[layer],
                    w_branch_ssm[layer], w_branch_attn[layer], w_out[layer])
    return (xs[0], xs[1])
```
